```python
import math
import jax
import jax.numpy as jnp
from jax import lax
import numpy as np

D_MODEL = 2048
BATCH = 16
SEQ = 256
DEPTH = 4
DEC_BATCH = 4
DEC_SEQ = 2048
PAST_LEN = 512

GRID_W = 64
NA_HEADS = 8
NA_HD = 128
NA_ROWS = 8
NA_COLS = 16
GQA_HEADS = 8
GQA_KV = 2
GQA_HD = 128
ROPE_BASE = 10000.0
SSD_INNER = 1024
SSD_HEADDIM = 64
SSD_HEADS = SSD_INNER // SSD_HEADDIM
SSD_STATE = 128
SSD_GROUPS = 2
SSD_CONV = 3
SSD_CHUNK = 128
SSD_CONV_CH = SSD_INNER + 2 * SSD_GROUPS * SSD_STATE
BRANCH_W = 1024
N_IN = 3 * NA_HEADS * NA_HD + (GQA_HEADS + 2 * GQA_KV) * GQA_HD + SSD_INNER + SSD_CONV_CH + 2 * SSD_HEADS + 3 * D_MODEL
FFN_HIDDEN = -(-8 * D_MODEL // (3 * 256)) * 256
Q_BLOCK = 128
EPS = 1e-6
NEG = -1e30

kernel_name = "hybrid_dit_prefix_step"


def rms_norm(x, w):
    xf = x.astype(jnp.float32)
    xf = xf * lax.rsqrt(jnp.mean(xf * xf, axis=-1, keepdims=True) + EPS)
    return (xf * w.astype(jnp.float32)).astype(x.dtype)


def proj_split_points():
    sizes = [NA_HEADS * NA_HD] * 3 + [GQA_HEADS * GQA_HD, GQA_KV * GQA_HD, GQA_KV * GQA_HD,
                                      SSD_INNER, SSD_CONV_CH, 2 * SSD_HEADS]
    return [int(v) for v in np.cumsum(sizes)]


def axial_rope(x):
    L, d = x.shape[1], x.shape[-1]
    half = d // 2
    nf = half // 2
    t = jnp.arange(L)
    inv = ROPE_BASE ** (-jnp.arange(nf, dtype=jnp.float32) / nf)

    def rot(xp, pos):
        ang = pos.astype(jnp.float32)[:, None] * inv[None, :]
        cos = jnp.cos(ang)[None, :, None, :]
        sin = jnp.sin(ang)[None, :, None, :]
        x1 = xp[..., :nf].astype(jnp.float32)
        x2 = xp[..., nf:].astype(jnp.float32)
        return jnp.concatenate([x1 * cos - x2 * sin, x2 * cos + x1 * sin], axis=-1)

    out = jnp.concatenate([rot(x[..., :half], t // GRID_W), rot(x[..., half:], t % GRID_W)], axis=-1)
    return out.astype(x.dtype)


def blocked_attention(q, k, v, n_kv):
    b, L, H, d = q.shape
    g = H // n_kv
    nb = L // Q_BLOCK
    scale = d ** -0.5
    qb = q.reshape(b, nb, Q_BLOCK, n_kv, g, d).transpose(1, 0, 2, 3, 4, 5)

    def one_block(qi):
        s = jnp.einsum("bqkgd,bskd->bkgqs", qi, k).astype(jnp.float32) * scale
        pr = jax.nn.softmax(s, axis=-1).astype(v.dtype)
        return jnp.einsum("bkgqs,bskd->bqkgd", pr, v)

    o = lax.map(one_block, qb)
    return o.transpose(1, 0, 2, 3, 4, 5).reshape(b, L, H, d)


def na_latent(q, k, v, ck, cv, rpb):
    b, L, H, d = q.shape
    R = L // GRID_W
    WR = min(NA_ROWS, R)
    scale = d ** -0.5
    qg = q.reshape(b, R, GRID_W, H, d)
    kg = k.reshape(b, R, GRID_W, H, d)
    vg = v.reshape(b, R, GRID_W, H, d)
    r = jnp.arange(R)
    rs = jnp.clip(r - WR // 2, 0, R - WR)
    row_idx = rs[:, None] + jnp.arange(WR)[None, :]
    kw = kg[:, row_idx]
    vw = vg[:, row_idx]
    cidx = jnp.arange(GRID_W)
    cs = jnp.clip(cidx - NA_COLS // 2, 0, GRID_W - NA_COLS)
    cmask = (cidx[None, :] >= cs[:, None]) & (cidx[None, :] < cs[:, None] + NA_COLS)
    dr_idx = row_idx - r[:, None] + NA_ROWS - 1
    dc_idx = jnp.clip(cidx[None, :] - cidx[:, None] + NA_COLS - 1, 0, 2 * NA_COLS - 2)
    bias = rpb[:, dr_idx[:, :, None, None], dc_idx[None, None, :, :]]
    bias = bias.transpose(1, 0, 3, 2, 4).astype(jnp.float32)
    s_loc = jnp.einsum("brqhd,briwhd->brhqiw", qg, kw).astype(jnp.float32) * scale + bias[None]
    s_loc = jnp.where(cmask[:, None, :], s_loc, NEG).reshape(b, R, H, GRID_W, WR * GRID_W)
    s_ctx = jnp.einsum("brqhd,bshd->brhqs", qg, ck).astype(jnp.float32) * scale
    pr = jax.nn.softmax(jnp.concatenate([s_loc, s_ctx], axis=-1), axis=-1).astype(v.dtype)
    p_loc = pr[..., :WR * GRID_W].reshape(b, R, H, GRID_W, WR, GRID_W)
    p_ctx = pr[..., WR * GRID_W:]
    o = jnp.einsum("brhqiw,briwhd->brqhd", p_loc, vw) + jnp.einsum("brhqs,bshd->brqhd", p_ctx, cv)
    return o.reshape(b, L, H, d)


def ssd_scan(x, dt, A, B, C, h0):
    b, L, H, P = x.shape
    G, N = B.shape[2], B.shape[3]
    nc = L // SSD_CHUNK
    rep = H // G
    f32 = jnp.float32
    x = x.astype(f32).reshape(b, nc, SSD_CHUNK, H, P)
    dt = dt.astype(f32).reshape(b, nc, SSD_CHUNK, H)
    B = jnp.repeat(B.astype(f32), rep, axis=2).reshape(b, nc, SSD_CHUNK, H, N)
    C = jnp.repeat(C.astype(f32), rep, axis=2).reshape(b, nc, SSD_CHUNK, H, N)
    a_cum = jnp.cumsum(dt * A, axis=2)
    seg = a_cum[:, :, :, None, :] - a_cum[:, :, None, :, :]
    tril = jnp.tril(jnp.ones((SSD_CHUNK, SSD_CHUNK), dtype=bool))
    lmat = jnp.exp(jnp.where(tril[:, :, None], seg, -jnp.inf))
    scores = jnp.einsum("bclhn,bcshn->bclsh", C, B) * lmat * dt[:, :, None, :, :]
    y_diag = jnp.einsum("bclsh,bcshp->bclhp", scores, x)
    decay_end = jnp.exp(a_cum[:, :, -1:, :] - a_cum)
    states = jnp.einsum("bcshn,bcsh,bcshp->bchpn", B, decay_end * dt, x)
    chunk_decay = jnp.exp(a_cum[:, :, -1, :])

    def step(h, inp):
        s_c, d_c = inp
        return d_c[:, :, None, None] * h + s_c, h

    h_final, h_starts = lax.scan(step, h0.astype(f32),
                                 (states.transpose(1, 0, 2, 3, 4), chunk_decay.transpose(1, 0, 2)))
    h_starts = h_starts.transpose(1, 0, 2, 3, 4)
    y_off = jnp.einsum("bclhn,bchpn,bclh->bclhp", C, h_starts, jnp.exp(a_cum))
    return (y_diag + y_off).reshape(b, L, H, P), h_final


def ssd_mixer(z, xbc, dt_raw, conv_w, conv_b, dt_bias, a_log, d_skip, norm_w, h0f, h0b):
    b, L, _ = z.shape
    xbc = lax.conv_general_dilated(xbc, conv_w[:, None, :], (1,), [(SSD_CONV // 2, SSD_CONV // 2)],
                                   dimension_numbers=("NWC", "WIO", "NWC"),
                                   feature_group_count=SSD_CONV_CH) + conv_b
    xbc = jax.nn.silu(xbc)
    xs, bm, cm = jnp.split(xbc, [SSD_INNER, SSD_INNER + SSD_GROUPS * SSD_STATE], axis=-1)
    xs = xs.reshape(b, L, SSD_HEADS, SSD_HEADDIM)
    bm = bm.reshape(b, L, SSD_GROUPS, SSD_STATE)
    cm = cm.reshape(b, L, SSD_GROUPS, SSD_STATE)
    dt = jax.nn.softplus(dt_raw.reshape(b, L, 2, SSD_HEADS).astype(jnp.float32) + dt_bias.astype(jnp.float32))
    A = -jnp.exp(a_log.astype(jnp.float32))
    y_f, h_f = ssd_scan(xs, dt[:, :, 0], A[0], bm, cm, h0f)
    y_b, h_b = ssd_scan(xs[:, ::-1], dt[:, ::-1, 1], A[1], bm[:, ::-1], cm[:, ::-1], h0b)
    d_tot = (d_skip[0] + d_skip[1]).astype(jnp.float32)[:, None]
    y = y_f + y_b[:, ::-1] + d_tot * xs.astype(jnp.float32)
    y = y.reshape(b, L, SSD_INNER).astype(z.dtype)
    y = rms_norm(y * jax.nn.silu(z), norm_w)
    return y, h_f, h_b


def trunk_layer(x, cvec, p, ctx):
    b, L, _ = x.shape
    mods = jax.nn.silu(cvec) @ p["ada_w"] + p["ada_b"]
    sh1, sc1, g1, sh2, sc2, g2 = [m[:, None, :] for m in jnp.split(mods, 6, axis=-1)]
    h = rms_norm(x, p["n_mix_pre"]) * (1 + sc1) + sh1
    u = h @ p["w_in"]
    qa, ka, va, qc, kc, vc, z, xbc, dt_raw, gl = jnp.split(u, proj_split_points(), axis=-1)
    qa = qa.reshape(b, L, NA_HEADS, NA_HD)
    ka = ka.reshape(b, L, NA_HEADS, NA_HD)
    va = va.reshape(b, L, NA_HEADS, NA_HD)
    qc = rms_norm(qc.reshape(b, L, GQA_HEADS, GQA_HD), p["q_norm"])
    kc = rms_norm(kc.reshape(b, L, GQA_KV, GQA_HD), p["k_norm"])
    vc = vc.reshape(b, L, GQA_KV, GQA_HD)
    if ctx is None:
        oa = blocked_attention(qa, ka, va, NA_HEADS)
        oc = blocked_attention(qc, kc, vc, GQA_KV)
        h0 = jnp.zeros((b, SSD_HEADS, SSD_HEADDIM, SSD_STATE), jnp.float32)
        ob, h_f, h_b = ssd_mixer(z, xbc, dt_raw, p["conv_w"], p["conv_b"], p["dt_bias"], p["a_log"],
                                 p["d_skip"], p["ssd_norm_w"], h0, h0)
        new = (ka, va, kc, vc, jnp.stack([h_f, h_b], axis=1).astype(x.dtype))
    else:
        ck_a, cv_a, ck_c, cv_c, st = ctx
        oa = na_latent(qa, ka, va, ck_a, cv_a, p["rpb"])
        qc = axial_rope(qc)
        kc = axial_rope(kc)
        oc = blocked_attention(qc, jnp.concatenate([kc, ck_c], axis=1),
                               jnp.concatenate([vc, cv_c], axis=1), GQA_KV)
        ob, _, _ = ssd_mixer(z, xbc, dt_raw, p["conv_w"], p["conv_b"], p["dt_bias"], p["a_log"],
                             p["d_skip"], p["ssd_norm_w"], st[:, 0], st[:, 1])
        new = None
    pa = oa.reshape(b, L, BRANCH_W) @ p["w_branch"][0]
    pb = ob @ p["w_branch"][1]
    pc = oc.reshape(b, L, BRANCH_W) @ p["w_branch"][2]
    ga, gb, gc = jnp.split(jax.nn.sigmoid(gl + p["gate_b"]), 3, axis=-1)
    mix = (ga * pa + gb * pb + gc * pc) @ p["w_out"]
    x = x + g1 * rms_norm(mix, p["n_mix_post"])
    h = rms_norm(x, p["n_ffn_pre"]) * (1 + sc2) + sh2
    gate, up = jnp.split(h @ p["ffn_w_up"], 2, axis=-1)
    f = (jax.nn.silu(gate) * up) @ p["ffn_w_down"]
    x = x + g2 * rms_norm(f, p["n_ffn_post"])
    return x, new


def setup_inputs(seed: int = 0) -> dict:
    key = jax.random.key(seed)
    ks = iter(jax.random.split(key, 40))
    f32 = jnp.float32
    D = D_MODEL

    def nrm(shape, scale):
        return scale * jax.random.normal(next(ks), shape, f32)

    def gain(shape):
        return 1.0 + nrm(shape, 0.02)

    dt0 = jnp.exp(jax.random.uniform(next(ks), (DEPTH, 2, SSD_HEADS), f32,
                                     minval=math.log(1e-3), maxval=math.log(1e-1)))
    dt_bias = dt0 + jnp.log(-jnp.expm1(-dt0))
    a_log = jnp.log(jax.random.uniform(next(ks), (DEPTH, 2, SSD_HEADS), f32, minval=1.0, maxval=16.0))
    return {
        "x_prompt": nrm((BATCH, SEQ, D), 1.0),
        "x_sample": nrm((DEC_BATCH, DEC_SEQ, D), 1.0),
        "cache_na_k": nrm((DEC_BATCH, DEPTH, PAST_LEN, NA_HEADS, NA_HD), 1.0),
        "cache_na_v": nrm((DEC_BATCH, DEPTH, PAST_LEN, NA_HEADS, NA_HD), 1.0),
        "cache_gqa_k": nrm((DEC_BATCH, DEPTH, PAST_LEN, GQA_KV, GQA_HD), 1.0),
        "cache_gqa_v": nrm((DEC_BATCH, DEPTH, PAST_LEN, GQA_KV, GQA_HD), 1.0),
        "state_ssd": nrm((DEC_BATCH, DEPTH, 2, SSD_HEADS, SSD_HEADDIM, SSD_STATE), 0.1),
        "c": nrm((DEC_BATCH, D), 1.0),
        "c_ctx": nrm((D,), 1.0),
        "ada_w": nrm((DEPTH, D, 6 * D), 0.5 * D ** -0.5),
        "ada_b": nrm((DEPTH, 6 * D), 0.02),
        "norm_mix_pre": gain((DEPTH, D)),
        "norm_mix_post": gain((DEPTH, D)),
        "norm_ffn_pre": gain((DEPTH, D)),
        "norm_ffn_post": gain((DEPTH, D)),
        "w_in": nrm((DEPTH, D, N_IN), D ** -0.5),
        "gate_b": nrm((DEPTH, 3 * D), 0.02),
        "na_rpb": nrm((DEPTH, NA_HEADS, 2 * NA_ROWS - 1, 2 * NA_COLS - 1), 0.1),
        "gqa_q_norm": gain((DEPTH, GQA_HD)),
        "gqa_k_norm": gain((DEPTH, GQA_HD)),
        "ssd_conv_w": nrm((DEPTH, SSD_CONV, SSD_CONV_CH), SSD_CONV ** -0.5),
        "ssd_conv_b": nrm((DEPTH, SSD_CONV_CH), 0.02),
        "ssd_dt_bias": dt_bias,
        "ssd_a_log": a_log,
        "ssd_d": 0.5 + nrm((DEPTH, 2, SSD_HEADS), 0.05),
        "ssd_norm_w": gain((DEPTH, SSD_INNER)),
        "w_branch": nrm((DEPTH, 3, BRANCH_W, D), BRANCH_W ** -0.5),
        "w_out": nrm((DEPTH, D, D), D ** -0.5),
        "ffn_w_up": nrm((DEPTH, D, 2 * FFN_HIDDEN), D ** -0.5),
        "ffn_w_down": nrm((DEPTH, FFN_HIDDEN, D), FFN_HIDDEN ** -0.5),
    }


def reference(x_prompt, x_sample, cache_na_k, cache_na_v, cache_gqa_k, cache_gqa_v, state_ssd,
              c, c_ctx, ada_w, ada_b, norm_mix_pre, norm_mix_post, norm_ffn_pre, norm_ffn_post,
              w_in, gate_b, na_rpb, gqa_q_norm, gqa_k_norm, ssd_conv_w, ssd_conv_b, ssd_dt_bias,
              ssd_a_log, ssd_d, ssd_norm_w, w_branch, w_out, ffn_w_up, ffn_w_down):
    xp = x_prompt
    xs = x_sample
    na_k, na_v, gqa_k, gqa_v, ssd_st = [], [], [], [], []
    for l in range(DEPTH):
        p = {
            "ada_w": ada_w[l], "ada_b": ada_b[l],
            "n_mix_pre": norm_mix_pre[l], "n_mix_post": norm_mix_post[l],
            "n_ffn_pre": norm_ffn_pre[l], "n_ffn_post": norm_ffn_post[l],
            "w_in": w_in[l], "gate_b": gate_b[l], "rpb": na_rpb[l],
            "q_norm": gqa_q_norm[l], "k_norm": gqa_k_norm[l],
            "conv_w": ssd_conv_w[l], "conv_b": ssd_conv_b[l], "dt_bias": ssd_dt_bias[l],
            "a_log": ssd_a_log[l], "d_skip": ssd_d[l], "ssd_norm_w": ssd_norm_w[l],
            "w_branch": w_branch[l], "w_out": w_out[l],
            "ffn_w_up": ffn_w_up[l], "ffn_w_down": ffn_w_down[l],
        }
        xp, (ka, va, kc, vc, st) = trunk_layer(xp, c_ctx[None, :], p, None)
        na_k.append(ka)
        na_v.append(va)
        gqa_k.append(kc)
        gqa_v.append(vc)
        ssd_st.append(st)
        ctx = (cache_na_k[:, l], cache_na_v[:, l], cache_gqa_k[:, l], cache_gqa_v[:, l], state_ssd[:, l])
        xs, _ = trunk_layer(xs, c, p, ctx)
    new_na_k = jnp.stack(na_k, axis=1)
    new_na_v = jnp.stack(na_v, axis=1)
    new_gqa_k = jnp.stack(gqa_k, axis=1)
    new_gqa_v = jnp.stack(gqa_v, axis=1)
    new_ssd = jnp.stack(ssd_st, axis=1)
    return (xp, xs, new_na_k, new_na_v, new_gqa_k, new_gqa_v, new_ssd)
```

```python
import functools

import jax
import jax.numpy as jnp
from jax import lax
from jax.experimental import pallas as pl
from jax.experimental.pallas import tpu as pltpu

F32 = jnp.float32
BF16 = jnp.bfloat16
HI = lax.Precision.HIGHEST
EPS = 1e-6
NEG = -1e30

D = 2048
DEPTH = 4
B_CTX, L_CTX = 16, 256
B_LAT, L_LAT = 4, 2048
PAST = 512
GRID_W = 64
GRID_R = L_LAT // GRID_W
NA_H = 8
HD = 128
NA_ROWS, NA_COLS = 8, 16
GQA_H, GQA_KV = 8, 2
GQA_G = GQA_H // GQA_KV
ROPE_BASE = 10000.0
SSD_INNER = 1024
SSD_P = 64
SSD_H = SSD_INNER // SSD_P
SSD_N = 128
SSD_G = 2
CHUNK = 128
CONV_CH = SSD_INNER + 2 * SSD_G * SSD_N
BRANCH_W = 1024
FFN_H = 5632
T_CTX = B_CTX * L_CTX
T_LAT = B_LAT * L_LAT
T = T_CTX + T_LAT

C_QA, C_KA, C_VA = 0, NA_H * HD, 2 * NA_H * HD
C_QC = 3 * NA_H * HD
C_KC = C_QC + GQA_H * HD
C_VC = C_KC + GQA_KV * HD
C_Z = C_VC + GQA_KV * HD
C_XBC = C_Z + SSD_INNER
C_DT = C_XBC + CONV_CH
C_GL = C_DT + 2 * SSD_H
N_MAIN = C_DT

NT_DIMS = (((1,), (1,)), ((), ()))
TN_DIMS = (((0,), (0,)), ((), ()))

ROW_TILE = 256


def _cp(n_axes, vmem_mb=None):
    kw = dict(dimension_semantics=("arbitrary",) * n_axes)
    if vmem_mb is not None:
        kw["vmem_limit_bytes"] = vmem_mb * 1024 * 1024
    return pltpu.CompilerParams(**kw)


def _mod_row(i, tm):
    n_ctx = T_CTX // tm
    n_lat = L_LAT // tm
    return jnp.where(i < n_ctx, 0, 1 + (i - n_ctx) // n_lat)


def _rms(x, g):
    return x * lax.rsqrt(jnp.mean(x * x, axis=-1, keepdims=True) + EPS) * g


def _silu(x):
    return x * jax.nn.sigmoid(x)


def _mods_kernel(c_ref, w_ref, b_ref, o_ref):
    s = _silu(c_ref[...]).astype(BF16)
    o_ref[...] = jnp.dot(s, w_ref[...].astype(BF16), preferred_element_type=F32) + b_ref[...]


def _compute_mods(cvec, ada_w, ada_b):
    tn = 1024
    n = 6 * D
    return pl.pallas_call(
        _mods_kernel,
        grid=(DEPTH, n // tn),
        in_specs=[
            pl.BlockSpec((8, D), lambda l, j: (0, 0)),
            pl.BlockSpec((None, D, tn), lambda l, j: (l, 0, j)),
            pl.BlockSpec((None, 1, tn), lambda l, j: (l, 0, j)),
        ],
        out_specs=pl.BlockSpec((None, 8, tn), lambda l, j: (l, 0, j)),
        out_shape=jax.ShapeDtypeStruct((DEPTH, 8, n), F32),
        compiler_params=_cp(2),
        name="mods",
    )(cvec, ada_w, ada_b.reshape(DEPTH, 1, n))


def _mod_spec(l, k, tm):
    return pl.BlockSpec((None, None, None, 1, D), lambda i: (l, _mod_row(i, tm), k, 0, 0))


def _gain_spec(l):
    return pl.BlockSpec((None, 1, D), lambda i: (l, 0, 0))


def _normmod_kernel(x_ref, g_ref, sh_ref, sc_ref, h_ref):
    h_ref[...] = (_rms(x_ref[...], g_ref[...]) * (1.0 + sc_ref[...]) + sh_ref[...]).astype(BF16)


def _normmod(x, gains, mods, l, k_sh, k_sc):
    tm = ROW_TILE
    return pl.pallas_call(
        _normmod_kernel,
        grid=(T // tm,),
        in_specs=[
            pl.BlockSpec((tm, D), lambda i: (i, 0)),
            _gain_spec(l),
            _mod_spec(l, k_sh, tm),
            _mod_spec(l, k_sc, tm),
        ],
        out_specs=pl.BlockSpec((tm, D), lambda i: (i, 0)),
        out_shape=jax.ShapeDtypeStruct((T, D), BF16),
        compiler_params=_cp(1),
        name="normmod",
    )(x, gains.reshape(DEPTH, 1, D), mods, mods)


def _resid_kernel(x_ref, y_ref, gpost_ref, gate_ref, *rest, with_h):
    xn = x_ref[...] + gate_ref[...] * _rms(y_ref[...], gpost_ref[...])
    if with_h:
        gpre_ref, sh_ref, sc_ref, xo_ref, h_ref = rest
        xo_ref[...] = xn
        h_ref[...] = (_rms(xn, gpre_ref[...]) * (1.0 + sc_ref[...]) + sh_ref[...]).astype(BF16)
    else:
        (xo_ref,) = rest
        xo_ref[...] = xn


def _resid(x, y, post_gains, mods, l, k_gate, nxt=None):
    tm = ROW_TILE
    row = pl.BlockSpec((tm, D), lambda i: (i, 0))
    in_specs = [row, row, _gain_spec(l), _mod_spec(l, k_gate, tm)]
    args = [x, y, post_gains.reshape(DEPTH, 1, D), mods]
    out_specs = [row]
    out_shape = [jax.ShapeDtypeStruct((T, D), F32)]
    if nxt is not None:
        gains, ln, k_sh, k_sc = nxt
        in_specs += [_gain_spec(ln), _mod_spec(ln, k_sh, tm), _mod_spec(ln, k_sc, tm)]
        args += [gains.reshape(DEPTH, 1, D), mods, mods]
        out_specs.append(row)
        out_shape.append(jax.ShapeDtypeStruct((T, D), BF16))
    out = pl.pallas_call(
        functools.partial(_resid_kernel, with_h=nxt is not None),
        grid=(T // tm,),
        in_specs=in_specs,
        out_specs=out_specs,
        out_shape=out_shape,
        compiler_params=_cp(1),
        name="resid",
    )(*args)
    return out if nxt is not None else (out[0], None)


def _mm_kernel(x_ref, w_ref, o_ref, wb_ref):
    @pl.when(pl.program_id(1) == 0)
    def _():
        wb_ref[...] = w_ref[...].astype(BF16)

    o_ref[...] = jnp.dot(x_ref[...], wb_ref[...], preferred_element_type=F32).astype(o_ref.dtype)


def _matmul(x, w, l, col0, ncols, tn, tm, out_dtype, name, vmem_mb=None):
    m, k = x.shape
    jb = col0 // tn
    return pl.pallas_call(
        _mm_kernel,
        grid=(ncols // tn, m // tm),
        in_specs=[
            pl.BlockSpec((tm, k), lambda j, i: (i, 0)),
            pl.BlockSpec((None, k, tn), lambda j, i: (l, 0, jb + j)),
        ],
        out_specs=pl.BlockSpec((tm, tn), lambda j, i: (i, j)),
        out_shape=jax.ShapeDtypeStruct((m, ncols), out_dtype),
        scratch_shapes=[pltpu.VMEM((k, tn), BF16)],
        compiler_params=_cp(2, vmem_mb),
        name=name,
    )(x, w)


def _mm_swiglu_kernel(x_ref, wg_ref, wu_ref, o_ref, wgb_ref, wub_ref):
    @pl.when(pl.program_id(1) == 0)
    def _():
        wgb_ref[...] = wg_ref[...].astype(BF16)
        wub_ref[...] = wu_ref[...].astype(BF16)

    x = x_ref[...]
    g = jnp.dot(x, wgb_ref[...], preferred_element_type=F32)
    u = jnp.dot(x, wub_ref[...], preferred_element_type=F32)
    o_ref[...] = (_silu(g) * u).astype(o_ref.dtype)


def _matmul_swiglu(x, w, l, tn, tm):
    m, k = x.shape
    nb = FFN_H // tn
    return pl.pallas_call(
        _mm_swiglu_kernel,
        grid=(nb, m // tm),
        in_specs=[
            pl.BlockSpec((tm, k), lambda j, i: (i, 0)),
            pl.BlockSpec((None, k, tn), lambda j, i: (l, 0, j)),
            pl.BlockSpec((None, k, tn), lambda j, i: (l, 0, nb + j)),
        ],
        out_specs=pl.BlockSpec((tm, tn), lambda j, i: (i, j)),
        out_shape=jax.ShapeDtypeStruct((m, FFN_H), BF16),
        scratch_shapes=[pltpu.VMEM((k, tn), BF16), pltpu.VMEM((k, tn), BF16)],
        compiler_params=_cp(2),
        name="ffn_up",
    )(x, w, w)


def _merge_kernel(h_ref, oa_ref, ob_ref, oc_ref, wg0, wg1, wg2, wb0, wb1, wb2, gb0, gb1, gb2, o_ref, wgb_ref, wbb_ref):
    wg = (wg0, wg1, wg2)
    wb = (wb0, wb1, wb2)
    gb = (gb0, gb1, gb2)

    @pl.when(pl.program_id(1) == 0)
    def _():
        for g in range(3):
            wgb_ref[g] = wg[g][...].astype(BF16)
            wbb_ref[g] = wb[g][...].astype(BF16)

    h = h_ref[...]
    acc = None
    for g, o_r in enumerate((oa_ref, ob_ref, oc_ref)):
        gl = jnp.dot(h, wgb_ref[g], preferred_element_type=F32) + gb[g][...]
        p = jnp.dot(o_r[...], wbb_ref[g], preferred_element_type=F32)
        t = jax.nn.sigmoid(gl) * p
        acc = t if acc is None else acc + t
    o_ref[...] = acc.astype(BF16)


def _merge(h, oa, ob, oc, w_gl, w_branch, gate_b, l, tn=256, tm=512):
    nb = D // tn
    xrow = pl.BlockSpec((tm, D), lambda j, i: (i, 0))
    orow = pl.BlockSpec((tm, BRANCH_W), lambda j, i: (i, 0))

    def wg_spec(g):
        return pl.BlockSpec((None, D, tn), lambda j, i: (l, 0, g * nb + j))

    def wb_spec(g):
        return pl.BlockSpec((None, None, BRANCH_W, tn), lambda j, i: (l, g, 0, j))

    def gb_spec(g):
        return pl.BlockSpec((None, 1, tn), lambda j, i: (l, 0, g * nb + j))

    gate_b3 = gate_b.reshape(DEPTH, 1, 3 * D)
    return pl.pallas_call(
        _merge_kernel,
        grid=(nb, T // tm),
        in_specs=[xrow, orow, orow, orow]
        + [wg_spec(g) for g in range(3)]
        + [wb_spec(g) for g in range(3)]
        + [gb_spec(g) for g in range(3)],
        out_specs=pl.BlockSpec((tm, tn), lambda j, i: (i, j)),
        out_shape=jax.ShapeDtypeStruct((T, D), BF16),
        scratch_shapes=[pltpu.VMEM((3, D, tn), BF16), pltpu.VMEM((3, BRANCH_W, tn), BF16)],
        compiler_params=_cp(2),
        name="merge",
    )(h, oa, ob, oc, w_gl, w_gl, w_gl, w_branch, w_branch, w_branch, gate_b3, gate_b3, gate_b3)


ATT_SCALE = HD ** -0.5


def _softmax_pv(s, v):
    m = jnp.max(s, axis=-1, keepdims=True)
    e = jnp.exp(s - m)
    p = e * (1.0 / jnp.sum(e, axis=-1, keepdims=True))
    return jnp.dot(p.astype(BF16), v, preferred_element_type=F32)


def _ctx_attn_kernel(a_ref, c_ref, qn_ref, kn_ref, oa_ref, oc_ref, kcn_ref):
    for h in range(NA_H):
        q = a_ref[:, C_QA + h * HD:C_QA + (h + 1) * HD].astype(BF16)
        k = a_ref[:, C_KA + h * HD:C_KA + (h + 1) * HD].astype(BF16)
        v = a_ref[:, C_VA + h * HD:C_VA + (h + 1) * HD].astype(BF16)
        s = lax.dot_general(q, k, NT_DIMS, preferred_element_type=F32) * ATT_SCALE
        oa_ref[:, h * HD:(h + 1) * HD] = _softmax_pv(s, v).astype(BF16)
    kc0 = C_KC - C_QC
    vc0 = C_VC - C_QC
    for kv in range(GQA_KV):
        k = _rms(c_ref[:, kc0 + kv * HD:kc0 + (kv + 1) * HD], kn_ref[...])
        kcn_ref[:, kv * HD:(kv + 1) * HD] = k
        kb = k.astype(BF16)
        v = c_ref[:, vc0 + kv * HD:vc0 + (kv + 1) * HD].astype(BF16)
        for g in range(GQA_G):
            h = kv * GQA_G + g
            q = _rms(c_ref[:, h * HD:(h + 1) * HD], qn_ref[...]).astype(BF16)
            s = lax.dot_general(q, kb, NT_DIMS, preferred_element_type=F32) * ATT_SCALE
            oc_ref[:, h * HD:(h + 1) * HD] = _softmax_pv(s, v).astype(BF16)


def _ctx_attn(u, qn, kn, l):
    wa = 3 * NA_H * HD
    wc = C_Z - C_QC
    return pl.pallas_call(
        _ctx_attn_kernel,
        grid=(B_CTX,),
        in_specs=[
            pl.BlockSpec((L_CTX, wa), lambda b: (b, 0)),
            pl.BlockSpec((L_CTX, wc), lambda b: (b, C_QC // wc)),
            pl.BlockSpec((None, 1, HD), lambda b: (l, 0, 0)),
            pl.BlockSpec((None, 1, HD), lambda b: (l, 0, 0)),
        ],
        out_specs=[
            pl.BlockSpec((L_CTX, BRANCH_W), lambda b: (b, 0)),
            pl.BlockSpec((L_CTX, BRANCH_W), lambda b: (b, 0)),
            pl.BlockSpec((L_CTX, GQA_KV * HD), lambda b: (b, 0)),
        ],
        out_shape=[
            jax.ShapeDtypeStruct((T_CTX, BRANCH_W), BF16),
            jax.ShapeDtypeStruct((T_CTX, BRANCH_W), BF16),
            jax.ShapeDtypeStruct((T_CTX, GQA_KV * HD), F32),
        ],
        compiler_params=_cp(1),
        name="ctx_attn",
    )(u, u, qn.reshape(DEPTH, 1, HD), kn.reshape(DEPTH, 1, HD))


def _lat_prep_kernel(a_ref, c_ref, qn_ref, kn_ref, cos_ref, sin_ref, ao_ref, qc_ref, kc_ref, vc_ref):
    ao_ref[...] = a_ref[...].astype(BF16)
    cos = cos_ref[...]
    sin = sin_ref[...]
    lane = lax.broadcasted_iota(jnp.int32, cos.shape, 1)
    first = (lane & (HD // 2 - 1)) < (HD // 4)

    def rope(x):
        rot = jnp.where(first, -pltpu.roll(x, HD - HD // 4, 1), pltpu.roll(x, HD // 4, 1))
        return x * cos + rot * sin

    for h in range(GQA_H):
        qc_ref[:, h * HD:(h + 1) * HD] = rope(_rms(c_ref[:, h * HD:(h + 1) * HD], qn_ref[...])).astype(BF16)
    kc0 = C_KC - C_QC
    vc0 = C_VC - C_QC
    for kv in range(GQA_KV):
        kc_ref[:, kv * HD:(kv + 1) * HD] = rope(
            _rms(c_ref[:, kc0 + kv * HD:kc0 + (kv + 1) * HD], kn_ref[...])).astype(BF16)
    vc_ref[...] = c_ref[:, vc0:vc0 + GQA_KV * HD].astype(BF16)


def _lat_prep(u, qn, kn, cos_t, sin_t, l):
    tm = ROW_TILE
    wa = 3 * NA_H * HD
    wc = C_Z - C_QC
    r0 = T_CTX // tm
    nper = L_LAT // tm
    return pl.pallas_call(
        _lat_prep_kernel,
        grid=(T_LAT // tm,),
        in_specs=[
            pl.BlockSpec((tm, wa), lambda i: (r0 + i, 0)),
            pl.BlockSpec((tm, wc), lambda i: (r0 + i, C_QC // wc)),
            pl.BlockSpec((None, 1, HD), lambda i: (l, 0, 0)),
            pl.BlockSpec((None, 1, HD), lambda i: (l, 0, 0)),
            pl.BlockSpec((tm, HD), lambda i: (i % nper, 0)),
            pl.BlockSpec((tm, HD), lambda i: (i % nper, 0)),
        ],
        out_specs=[
            pl.BlockSpec((tm, wa), lambda i: (i, 0)),
            pl.BlockSpec((tm, GQA_H * HD), lambda i: (i, 0)),
            pl.BlockSpec((tm, GQA_KV * HD), lambda i: (i, 0)),
            pl.BlockSpec((tm, GQA_KV * HD), lambda i: (i, 0)),
        ],
        out_shape=[
            jax.ShapeDtypeStruct((T_LAT, wa), BF16),
            jax.ShapeDtypeStruct((T_LAT, GQA_H * HD), BF16),
            jax.ShapeDtypeStruct((T_LAT, GQA_KV * HD), BF16),
            jax.ShapeDtypeStruct((T_LAT, GQA_KV * HD), BF16),
        ],
        compiler_params=_cp(1),
        name="lat_prep",
    )(u, u, qn.reshape(DEPTH, 1, HD), kn.reshape(DEPTH, 1, HD), cos_t, sin_t)


def _lat_gqa_kernel(q_ref, k_ref, v_ref, o_ref):
    k = k_ref[...]
    v = v_ref[...]
    for g in range(GQA_G):
        s = lax.dot_general(q_ref[:, g * HD:(g + 1) * HD], k, NT_DIMS, preferred_element_type=F32) * ATT_SCALE
        o_ref[:, g * HD:(g + 1) * HD] = _softmax_pv(s, v).astype(BF16)


def _lat_gqa(qc, kcat, vcat, tq=256):
    nq = L_LAT // tq
    s_all = kcat.shape[1]
    return pl.pallas_call(
        _lat_gqa_kernel,
        grid=(B_LAT, GQA_KV, nq),
        in_specs=[
            pl.BlockSpec((tq, GQA_G * HD), lambda b, kv, i: (b * nq + i, kv)),
            pl.BlockSpec((None, s_all, HD), lambda b, kv, i: (b, 0, kv)),
            pl.BlockSpec((None, s_all, HD), lambda b, kv, i: (b, 0, kv)),
        ],
        out_specs=pl.BlockSpec((tq, GQA_G * HD), lambda b, kv, i: (b * nq + i, kv)),
        out_shape=jax.ShapeDtypeStruct((T_LAT, GQA_H * HD), BF16),
        compiler_params=_cp(3),
        name="lat_gqa",
    )(qc, kcat, vcat)


NA_QROWS = 4
NA_UROWS = 12
NA_TQ = NA_QROWS * GRID_W
NA_TK = NA_UROWS * GRID_W


def _lat_na_kernel(q_ref, k_ref, v_ref, ck_ref, cv_ref, t2_ref, o_ref, bias_ref):
    j = pl.program_id(2)
    ustart = jnp.clip(NA_QROWS * j - NA_ROWS // 2, 0, GRID_R - NA_UROWS)
    left = lax.broadcasted_iota(jnp.int32, (GRID_W, 2 * GRID_W), 1) < GRID_W
    for a in range(NA_QROWS):
        r = NA_QROWS * j + a
        rs = jnp.clip(r - NA_ROWS // 2, 0, GRID_R - NA_ROWS)
        for m in range(NA_UROWS // 2):
            kr0 = ustart + 2 * m
            v0 = ((kr0 >= rs) & (kr0 < rs + NA_ROWS)).astype(jnp.int32)
            v1 = ((kr0 + 1 >= rs) & (kr0 + 1 < rs + NA_ROWS)).astype(jnp.int32)
            idx = jnp.clip(kr0 - r + NA_ROWS, 0, 2 * NA_ROWS - 1)
            ok = jnp.where(left, v0, v1) > 0
            bias_ref[a * GRID_W:(a + 1) * GRID_W, m * 2 * GRID_W:(m + 1) * 2 * GRID_W] = jnp.where(ok, t2_ref[idx], NEG)
    start = pl.multiple_of(ustart * GRID_W, GRID_W)
    q = q_ref[...]
    kl = k_ref[pl.ds(start, NA_TK), :]
    vl = v_ref[pl.ds(start, NA_TK), :]
    s_loc = lax.dot_general(q, kl, NT_DIMS, preferred_element_type=F32) * ATT_SCALE + bias_ref[...]
    s_ctx = lax.dot_general(q, ck_ref[...], NT_DIMS, preferred_element_type=F32) * ATT_SCALE
    m = jnp.maximum(jnp.max(s_loc, axis=-1, keepdims=True), jnp.max(s_ctx, axis=-1, keepdims=True))
    e_loc = jnp.exp(s_loc - m)
    e_ctx = jnp.exp(s_ctx - m)
    inv = 1.0 / (jnp.sum(e_loc, axis=-1, keepdims=True) + jnp.sum(e_ctx, axis=-1, keepdims=True))
    o = jnp.dot((e_loc * inv).astype(BF16), vl, preferred_element_type=F32)
    o = o + jnp.dot((e_ctx * inv).astype(BF16), cv_ref[...], preferred_element_type=F32)
    o_ref[...] = o.astype(BF16)


def _lat_na(a, ck, cv, t2, l):
    nj = GRID_R // NA_QROWS
    return pl.pallas_call(
        _lat_na_kernel,
        grid=(B_LAT, NA_H, nj),
        in_specs=[
            pl.BlockSpec((NA_TQ, HD), lambda b, h, j: (b * nj + j, h)),
            pl.BlockSpec((L_LAT, HD), lambda b, h, j: (b, NA_H + h)),
            pl.BlockSpec((L_LAT, HD), lambda b, h, j: (b, 2 * NA_H + h)),
            pl.BlockSpec((None, PAST, HD), lambda b, h, j: (b, 0, h)),
            pl.BlockSpec((None, PAST, HD), lambda b, h, j: (b, 0, h)),
            pl.BlockSpec((None, None, 2 * NA_ROWS, GRID_W, 2 * GRID_W), lambda b, h, j: (l, h, 0, 0, 0)),
        ],
        out_specs=pl.BlockSpec((NA_TQ, HD), lambda b, h, j: (b * nj + j, h)),
        out_shape=jax.ShapeDtypeStruct((T_LAT, NA_H * HD), BF16),
        scratch_shapes=[pltpu.VMEM((NA_TQ, NA_TK), F32)],
        compiler_params=_cp(3),
        name="lat_na",
    )(a, a, a, ck, cv, t2)


def _na_bias_tiles(rpb):
    w = jnp.arange(GRID_W)
    cs = jnp.clip(w - NA_COLS // 2, 0, GRID_W - NA_COLS)
    cmask = (w[None, :] >= cs[:, None]) & (w[None, :] < cs[:, None] + NA_COLS)
    dc = jnp.clip(w[None, :] - w[:, None] + NA_COLS - 1, 0, 2 * NA_COLS - 2)
    tm = jnp.where(cmask, rpb[:, :, :, dc], NEG)
    pad = jnp.full(tm.shape[:2] + (1,) + tm.shape[3:], NEG, F32)
    lo = jnp.concatenate([pad, tm], axis=2)
    hi = jnp.concatenate([tm, pad], axis=2)
    return jnp.concatenate([lo, hi], axis=-1).astype(F32)


def _rope_tables():
    nf = HD // 4
    t = jnp.arange(L_LAT)
    inv = ROPE_BASE ** (-jnp.arange(nf, dtype=F32) / nf)
    ar = (t // GRID_W).astype(F32)[:, None] * inv[None, :]
    ac = (t % GRID_W).astype(F32)[:, None] * inv[None, :]
    cos_t = jnp.concatenate([jnp.cos(ar), jnp.cos(ar), jnp.cos(ac), jnp.cos(ac)], axis=-1)
    sin_t = jnp.concatenate([jnp.sin(ar), jnp.sin(ar), jnp.sin(ac), jnp.sin(ac)], axis=-1)
    return cos_t, sin_t


def _conv_kernel(x_ref, w_ref, b_ref, o_ref):
    x = x_ref[...]
    n = x.shape[0]
    row = lax.broadcasted_iota(jnp.int32, x.shape, 0)
    xm = jnp.where(row == 0, 0.0, pltpu.roll(x, 1, 0))
    xp = jnp.where(row == n - 1, 0.0, pltpu.roll(x, n - 1, 0))
    y = w_ref[0:1, :] * xm + w_ref[1:2, :] * x + w_ref[2:3, :] * xp + b_ref[...]
    o_ref[...] = _silu(y)


def _conv(u, conv_w, conv_b, l, nseq, seq_len, row0):
    tc = 512
    nc = CONV_CH // tc
    rb0 = row0 // seq_len
    cb0 = C_XBC // tc
    return pl.pallas_call(
        _conv_kernel,
        grid=(nseq, nc),
        in_specs=[
            pl.BlockSpec((seq_len, tc), lambda b, j: (rb0 + b, cb0 + j)),
            pl.BlockSpec((None, 3, tc), lambda b, j: (l, 0, j)),
            pl.BlockSpec((None, 1, tc), lambda b, j: (l, 0, j)),
        ],
        out_specs=pl.BlockSpec((seq_len, tc), lambda b, j: (b, j)),
        out_shape=jax.ShapeDtypeStruct((nseq * seq_len, CONV_CH), F32),
        compiler_params=_cp(2),
        name="conv",
    )(u, conv_w, conv_b.reshape(DEPTH, 1, CONV_CH))


def _softplus(x):
    return jnp.maximum(x, 0.0) + jnp.log1p(jnp.exp(-jnp.abs(x)))


def _ssd_chunk(xbc, dt_col, dt_row, a_row, a_col, h_ref, reverse):
    x = xbc[:, :SSD_INNER]
    bm = xbc[:, SSD_INNER:SSD_INNER + SSD_G * SSD_N]
    cm = xbc[:, SSD_INNER + SSD_G * SSD_N:]
    li = lax.broadcasted_iota(jnp.int32, (CHUNK, CHUNK), 0)
    si = lax.broadcasted_iota(jnp.int32, (CHUNK, CHUNK), 1)
    mask = (si >= li) if reverse else (si <= li)
    tri = mask.astype(F32)
    tri_t = ((li >= si) if reverse else (li <= si)).astype(F32)
    cum_col = jnp.dot(tri, dt_col * a_row, precision=HI, preferred_element_type=F32)
    cum_row = jnp.dot(dt_row * a_col, tri_t, precision=HI, preferred_element_type=F32)
    tot_row = cum_col[0:1, :] if reverse else cum_col[CHUNK - 1:CHUNK, :]
    col = lax.broadcasted_iota(jnp.int32, (SSD_H, SSD_INNER), 1)
    lo_col = lax.broadcasted_iota(jnp.int32, (SSD_H, SSD_INNER), 0) * SSD_P
    expand = ((col >= lo_col) & (col < lo_col + SSD_P)).astype(F32)
    w_exp = jnp.dot(jnp.exp(tot_row - cum_col) * dt_col, expand, precision=HI, preferred_element_type=F32)
    e_exp = jnp.dot(jnp.exp(cum_col), expand, precision=HI, preferred_element_type=F32)
    dec = jnp.dot(jnp.broadcast_to(jnp.exp(tot_row), (8, SSD_H)), expand, precision=HI,
                  preferred_element_type=F32)[0:1, :]
    hpg = SSD_H // SSD_G
    xb = x.astype(BF16)
    bgs = [bm[:, g * SSD_N:(g + 1) * SSD_N].astype(BF16) for g in range(SSD_G)]
    cgs = [cm[:, g * SSD_N:(g + 1) * SSD_N].astype(BF16) for g in range(SSD_G)]
    gmat = [lax.dot_general(cgs[g], bgs[g], NT_DIMS, preferred_element_type=F32) for g in range(SSD_G)]
    lo = lax.broadcasted_iota(jnp.int32, (CHUNK, 2 * SSD_P), 1) < SSD_P
    zero = jnp.zeros((CHUNK, 2 * SSD_P), BF16)
    ys = []
    for pr in range(SSD_H // 2):
        g = (2 * pr) // hpg
        ms = []
        for h in (2 * pr, 2 * pr + 1):
            seg = cum_col[:, h:h + 1] - cum_row[h:h + 1, :]
            lm = jnp.exp(jnp.where(mask, seg, NEG))
            ms.append((gmat[g] * lm * dt_row[h:h + 1, :]).astype(BF16))
        xp = xb[:, pr * 2 * SSD_P:(pr + 1) * 2 * SSD_P]
        rhs = jnp.concatenate([jnp.where(lo, xp, zero), jnp.where(lo, zero, xp)], axis=0)
        ys.append(jnp.dot(jnp.concatenate(ms, axis=1), rhs, preferred_element_type=F32))
    y = jnp.concatenate(ys, axis=1)
    h_t = h_ref[...]
    hb = h_t.astype(BF16)
    xw = (x * w_exp).astype(BF16)
    wpg = SSD_INNER // SSD_G
    yo = [jnp.dot(cgs[g], hb[:, g * wpg:(g + 1) * wpg], preferred_element_type=F32) for g in range(SSD_G)]
    st = [lax.dot_general(bgs[g], xw[:, g * wpg:(g + 1) * wpg], TN_DIMS, preferred_element_type=F32)
          for g in range(SSD_G)]
    h_ref[...] = h_t * dec + jnp.concatenate(st, axis=1)
    return y + jnp.concatenate(yo, axis=1) * e_exp


def _ssd_kernel(xf_ref, xb_ref, dtf_ref, dtb_ref, dttf_ref, dttb_ref, prow_ref, pcol_ref, dtot_ref, h0_ref,
                yf_ref, yb_ref, hfin_ref, hf_ref, hb_ref):
    c = pl.program_id(1)

    @pl.when(c == 0)
    def _():
        hf_ref[...] = h0_ref[0]
        hb_ref[...] = h0_ref[1]

    prow = prow_ref[...]
    pcol = pcol_ref[...]
    xf = xf_ref[...]
    dt_col = _softplus(dtf_ref[:, 0:SSD_H] + prow[0:1, :])
    dt_row = _softplus(dttf_ref[0:SSD_H, :] + pcol[:, 0:1])
    yf = _ssd_chunk(xf, dt_col, dt_row, -jnp.exp(prow[2:3, :]), -jnp.exp(pcol[:, 2:3]), hf_ref, False)
    yf_ref[...] = yf + dtot_ref[...] * xf[:, :SSD_INNER]
    dt_col = _softplus(dtb_ref[:, SSD_H:2 * SSD_H] + prow[1:2, :])
    dt_row = _softplus(dttb_ref[SSD_H:2 * SSD_H, :] + pcol[:, 1:2])
    yb_ref[...] = _ssd_chunk(xb_ref[...], dt_col, dt_row, -jnp.exp(prow[3:4, :]), -jnp.exp(pcol[:, 3:4]),
                             hb_ref, True)

    @pl.when(c == pl.num_programs(1) - 1)
    def _():
        hfin_ref[0] = hf_ref[...]
        hfin_ref[1] = hb_ref[...]


def _ssd(xbc, u_dt, dt_t, prow, pcol, dtot, h0, nseq, seq_len, row0):
    nc = seq_len // CHUNK
    cb0 = row0 // CHUNK
    rows = nseq * seq_len
    fwd = lambda b, c: b * nc + c
    bwd = lambda b, c: b * nc + (nc - 1 - c)
    return pl.pallas_call(
        _ssd_kernel,
        grid=(nseq, nc),
        in_specs=[
            pl.BlockSpec((CHUNK, CONV_CH), lambda b, c: (fwd(b, c), 0)),
            pl.BlockSpec((CHUNK, CONV_CH), lambda b, c: (bwd(b, c), 0)),
            pl.BlockSpec((CHUNK, 128), lambda b, c: (cb0 + fwd(b, c), 0)),
            pl.BlockSpec((CHUNK, 128), lambda b, c: (cb0 + bwd(b, c), 0)),
            pl.BlockSpec((2 * SSD_H, CHUNK), lambda b, c: (0, cb0 + fwd(b, c))),
            pl.BlockSpec((2 * SSD_H, CHUNK), lambda b, c: (0, cb0 + bwd(b, c))),
            pl.BlockSpec((4, SSD_H), lambda b, c: (0, 0)),
            pl.BlockSpec((SSD_H, 4), lambda b, c: (0, 0)),
            pl.BlockSpec((1, SSD_INNER), lambda b, c: (0, 0)),
            pl.BlockSpec((None, 2, SSD_N, SSD_INNER), lambda b, c: (b, 0, 0, 0)),
        ],
        out_specs=[
            pl.BlockSpec((CHUNK, SSD_INNER), lambda b, c: (fwd(b, c), 0)),
            pl.BlockSpec((CHUNK, SSD_INNER), lambda b, c: (bwd(b, c), 0)),
            pl.BlockSpec((None, 2, SSD_N, SSD_INNER), lambda b, c: (b, 0, 0, 0)),
        ],
        out_shape=[
            jax.ShapeDtypeStruct((rows, SSD_INNER), F32),
            jax.ShapeDtypeStruct((rows, SSD_INNER), F32),
            jax.ShapeDtypeStruct((nseq, 2, SSD_N, SSD_INNER), F32),
        ],
        scratch_shapes=[pltpu.VMEM((SSD_N, SSD_INNER), F32), pltpu.VMEM((SSD_N, SSD_INNER), F32)],
        compiler_params=_cp(2),
        name="ssd",
    )(xbc, xbc, u_dt, u_dt, dt_t, dt_t, prow, pcol, dtot, h0)


def _ssd_gate_kernel(yf_ref, yb_ref, z0_ref, z1_ref, g_ref, o_ref):
    z = jnp.concatenate([z0_ref[...], z1_ref[...]], axis=1)
    y = (yf_ref[...] + yb_ref[...]) * _silu(z)
    o_ref[...] = _rms(y, g_ref[...]).astype(BF16)


def _ssd_gate(yf, yb, u, norm_w, l, row0):
    tm = ROW_TILE
    rows = yf.shape[0]
    rb0 = row0 // tm
    half = SSD_INNER // 2
    zb = C_Z // half
    row = pl.BlockSpec((tm, SSD_INNER), lambda i: (i, 0))
    return pl.pallas_call(
        _ssd_gate_kernel,
        grid=(rows // tm,),
        in_specs=[
            row, row,
            pl.BlockSpec((tm, half), lambda i: (rb0 + i, zb)),
            pl.BlockSpec((tm, half), lambda i: (rb0 + i, zb + 1)),
            pl.BlockSpec((None, 1, SSD_INNER), lambda i: (l, 0, 0)),
        ],
        out_specs=row,
        out_shape=jax.ShapeDtypeStruct((rows, SSD_INNER), BF16),
        compiler_params=_cp(1),
        name="ssd_gate",
    )(yf, yb, u, u, norm_w.reshape(DEPTH, 1, SSD_INNER))


def kernel(x_prompt, x_sample, cache_na_k, cache_na_v, cache_gqa_k, cache_gqa_v, state_ssd, c, c_ctx, ada_w, ada_b, norm_mix_pre, norm_mix_post, norm_ffn_pre, norm_ffn_post, w_in, gate_b, na_rpb, gqa_q_norm, gqa_k_norm, ssd_conv_w, ssd_conv_b, ssd_dt_bias, ssd_a_log, ssd_d, ssd_norm_w, w_branch, w_out, ffn_w_up, ffn_w_down):
    x = jnp.concatenate([x_prompt.reshape(T_CTX, D), x_sample.reshape(T_LAT, D)], axis=0)
    cvec = jnp.concatenate([c_ctx[None, :], c, jnp.zeros((8 - 1 - B_LAT, D), F32)], axis=0)
    mods = _compute_mods(cvec, ada_w, ada_b).reshape(DEPTH, 8, 6, 1, D)
    w_gl = w_in[:, :, C_GL:]
    t2 = _na_bias_tiles(na_rpb)
    cos_t, sin_t = _rope_tables()
    ck_a = cache_na_k.reshape(B_LAT, DEPTH, PAST, NA_H * HD).astype(BF16)
    cv_a = cache_na_v.reshape(B_LAT, DEPTH, PAST, NA_H * HD).astype(BF16)
    ck_c = cache_gqa_k.reshape(B_LAT, DEPTH, PAST, GQA_KV * HD).astype(BF16)
    cv_c = cache_gqa_v.reshape(B_LAT, DEPTH, PAST, GQA_KV * HD).astype(BF16)
    h0_lat = state_ssd.transpose(0, 1, 2, 5, 3, 4).reshape(B_LAT, DEPTH, 2, SSD_N, SSD_INNER)
    h0_ctx = jnp.zeros((B_CTX, 2, SSD_N, SSD_INNER), F32)
    prow = jnp.concatenate([ssd_dt_bias, ssd_a_log], axis=1)
    pcol = prow.transpose(0, 2, 1)
    dtot = jnp.repeat(ssd_d[:, 0] + ssd_d[:, 1], SSD_P, axis=-1)[:, None, :]

    h = _normmod(x, norm_mix_pre, mods, 0, 0, 1)
    na_k, na_v, gqa_k, gqa_v, ssd_st = [], [], [], [], []
    for l in range(DEPTH):
        u = _matmul(h, w_in, l, 0, N_MAIN, 512, 512, F32, "in_proj")
        u_dt = _matmul(h, w_in, l, C_DT, 128, 128, 512, F32, "dt_proj")
        dt_t = u_dt[:, :2 * SSD_H].T
        oa_c, oc_c, kcn = _ctx_attn(u, gqa_q_norm, gqa_k_norm, l)
        a_l, qc_l, kc_l, vc_l = _lat_prep(u, gqa_q_norm, gqa_k_norm, cos_t, sin_t, l)
        kcat = jnp.concatenate([kc_l.reshape(B_LAT, L_LAT, GQA_KV * HD), ck_c[:, l]], axis=1)
        vcat = jnp.concatenate([vc_l.reshape(B_LAT, L_LAT, GQA_KV * HD), cv_c[:, l]], axis=1)
        oc_l = _lat_gqa(qc_l, kcat, vcat)
        oa_l = _lat_na(a_l, ck_a[:, l], cv_a[:, l], t2, l)
        xbc_c = _conv(u, ssd_conv_w, ssd_conv_b, l, B_CTX, L_CTX, 0)
        xbc_l = _conv(u, ssd_conv_w, ssd_conv_b, l, B_LAT, L_LAT, T_CTX)
        yf_c, yb_c, hfin = _ssd(xbc_c, u_dt, dt_t, prow[l], pcol[l], dtot[l], h0_ctx, B_CTX, L_CTX, 0)
        yf_l, yb_l, _ = _ssd(xbc_l, u_dt, dt_t, prow[l], pcol[l], dtot[l], h0_lat[:, l], B_LAT, L_LAT, T_CTX)
        ob_c = _ssd_gate(yf_c, yb_c, u, ssd_norm_w, l, 0)
        ob_l = _ssd_gate(yf_l, yb_l, u, ssd_norm_w, l, T_CTX)
        oa = jnp.concatenate([oa_c, oa_l], axis=0)
        ob = jnp.concatenate([ob_c, ob_l], axis=0)
        oc = jnp.concatenate([oc_c, oc_l], axis=0)
        mix = _merge(h, oa, ob, oc, w_gl, w_branch, gate_b, l)
        y = _matmul(mix, w_out, l, 0, D, 512, 512, F32, "out_proj")
        x, h2 = _resid(x, y, norm_mix_post, mods, l, 2, nxt=(norm_ffn_pre, l, 3, 4))
        act = _matmul_swiglu(h2, ffn_w_up, l, 512, 512)
        f = _matmul(act, ffn_w_down, l, 0, D, 256, 512, F32, "ffn_down")
        nxt = (norm_mix_pre, l + 1, 0, 1) if l + 1 < DEPTH else None
        x, h = _resid(x, f, norm_ffn_post, mods, l, 5, nxt=nxt)
        uc = u[:T_CTX]
        na_k.append(uc[:, C_KA:C_KA + NA_H * HD].reshape(B_CTX, L_CTX, NA_H, HD))
        na_v.append(uc[:, C_VA:C_VA + NA_H * HD].reshape(B_CTX, L_CTX, NA_H, HD))
        gqa_k.append(kcn.reshape(B_CTX, L_CTX, GQA_KV, HD))
        gqa_v.append(uc[:, C_VC:C_VC + GQA_KV * HD].reshape(B_CTX, L_CTX, GQA_KV, HD))
        ssd_st.append(hfin.reshape(B_CTX, 2, SSD_N, SSD_H, SSD_P).transpose(0, 1, 3, 4, 2))
    y_prompt = x[:T_CTX].reshape(B_CTX, L_CTX, D)
    y_sample = x[T_CTX:].reshape(B_LAT, L_LAT, D)
    return (y_prompt, y_sample, jnp.stack(na_k, axis=1), jnp.stack(na_v, axis=1), jnp.stack(gqa_k, axis=1),
            jnp.stack(gqa_v, axis=1), jnp.stack(ssd_st, axis=1))
```

```python
import functools
import math

import jax
import jax.numpy as jnp
from jax import lax
from jax.experimental import pallas as pl
from jax.experimental.pallas import tpu as pltpu

F32 = jnp.float32
BF16 = jnp.bfloat16
HI = lax.Precision.HIGHEST
EPS = 1e-6
NEG = -1e30

D = 2048
DEPTH = 4
B_CTX, L_CTX = 16, 256
B_LAT, L_LAT = 4, 2048
PAST = 512
GRID_W = 64
GRID_R = L_LAT // GRID_W
NA_H = 8
HD = 128
NA_ROWS, NA_COLS = 8, 16
GQA_H, GQA_KV = 8, 2
GQA_G = GQA_H // GQA_KV
ROPE_BASE = 10000.0
SSD_INNER = 1024
SSD_P = 64
SSD_H = SSD_INNER // SSD_P
SSD_N = 128
SSD_G = 2
CHUNK = 128
CONV_CH = SSD_INNER + 2 * SSD_G * SSD_N
BRANCH_W = 1024
FFN_H = 5632
T_CTX = B_CTX * L_CTX
T_LAT = B_LAT * L_LAT
T = T_CTX + T_LAT

C_QA, C_KA, C_VA = 0, NA_H * HD, 2 * NA_H * HD
C_QC = 3 * NA_H * HD
C_KC = C_QC + GQA_H * HD
C_VC = C_KC + GQA_KV * HD
C_Z = C_VC + GQA_KV * HD
C_XBC = C_Z + SSD_INNER
C_DT = C_XBC + CONV_CH
C_GL = C_DT + 2 * SSD_H
N_MAIN = C_DT

NT_DIMS = (((1,), (1,)), ((), ()))
TN_DIMS = (((0,), (0,)), ((), ()))

ROW_TILE = 256
SUBLANES = 8
VMEM_BIG = 56


def _cp(n_axes, vmem_mb=None):
    kw = dict(dimension_semantics=("arbitrary",) * n_axes)
    if vmem_mb is not None:
        kw["vmem_limit_bytes"] = vmem_mb * 1024 * 1024
    return pltpu.CompilerParams(**kw)


def _mod_row(i, tm):
    n_ctx = T_CTX // tm
    n_lat = L_LAT // tm
    return jnp.where(i < n_ctx, 0, 1 + (i - n_ctx) // n_lat)


def _rms(x, g):
    return x * lax.rsqrt(jnp.mean(x * x, axis=-1, keepdims=True) + EPS) * g


def _silu(x):
    return x * jax.nn.sigmoid(x)


def _mods_kernel(c_ref, w_ref, b_ref, o_ref):
    s = _silu(c_ref[...]).astype(BF16)
    o_ref[...] = jnp.dot(s, w_ref[...].astype(BF16), preferred_element_type=F32) + b_ref[...]


def _compute_mods(cvec, ada_w, ada_b):
    tn = 1024
    n = 6 * D
    return pl.pallas_call(
        _mods_kernel,
        grid=(DEPTH, n // tn),
        in_specs=[
            pl.BlockSpec((8, D), lambda l, j: (0, 0)),
            pl.BlockSpec((None, D, tn), lambda l, j: (l, 0, j)),
            pl.BlockSpec((None, 1, tn), lambda l, j: (l, 0, j)),
        ],
        out_specs=pl.BlockSpec((None, 8, tn), lambda l, j: (l, 0, j)),
        out_shape=jax.ShapeDtypeStruct((DEPTH, 8, n), F32),
        compiler_params=_cp(2),
        name="mods",
    )(cvec, ada_w, ada_b.reshape(DEPTH, 1, n))


def _mod_spec(l, k, tm):
    return pl.BlockSpec((None, None, None, 1, D), lambda i: (l, _mod_row(i, tm), k, 0, 0))


def _gain_spec(l):
    return pl.BlockSpec((None, 1, D), lambda i: (l, 0, 0))


def _normmod_kernel(x_ref, g_ref, sh_ref, sc_ref, h_ref):
    h_ref[...] = (_rms(x_ref[...], g_ref[...]) * (1.0 + sc_ref[...]) + sh_ref[...]).astype(BF16)


def _normmod(x, gains, mods, l, k_sh, k_sc):
    tm = ROW_TILE
    return pl.pallas_call(
        _normmod_kernel,
        grid=(T // tm,),
        in_specs=[
            pl.BlockSpec((tm, D), lambda i: (i, 0)),
            _gain_spec(l),
            _mod_spec(l, k_sh, tm),
            _mod_spec(l, k_sc, tm),
        ],
        out_specs=pl.BlockSpec((tm, D), lambda i: (i, 0)),
        out_shape=jax.ShapeDtypeStruct((T, D), BF16),
        compiler_params=_cp(1),
        name="normmod",
    )(x, gains.reshape(DEPTH, 1, D), mods, mods)


def _mm_t_kernel(x_ref, wt_ref, o_ref, wb_ref):
    @pl.when(pl.program_id(1) == 0)
    def _():
        wb_ref[...] = wt_ref[...].astype(BF16).T

    o_ref[...] = jnp.dot(x_ref[...], wb_ref[...], preferred_element_type=F32).astype(o_ref.dtype)


def _matmul_t(x, wt, l, ncols, tn, tm, out_dtype, name):
    m, k = x.shape
    return pl.pallas_call(
        _mm_t_kernel,
        grid=(ncols // tn, m // tm),
        in_specs=[
            pl.BlockSpec((tm, k), lambda j, i: (i, 0)),
            pl.BlockSpec((None, tn, k), lambda j, i: (l, j, 0)),
        ],
        out_specs=pl.BlockSpec((tm, tn), lambda j, i: (i, j)),
        out_shape=jax.ShapeDtypeStruct((m, ncols), out_dtype),
        scratch_shapes=[pltpu.VMEM((k, tn), BF16)],
        compiler_params=_cp(2, 40),
        name=name,
    )(x, wt)


def _dt_kernel(x_ref, wt_ref, o_ref, ot_ref):
    w = wt_ref[...].astype(BF16)
    x = x_ref[...]
    o_ref[...] = lax.dot_general(x, w, NT_DIMS, preferred_element_type=F32)
    ot_ref[...] = lax.dot_general(w, x, NT_DIMS, preferred_element_type=F32)


def _dt_proj(x, wt, l):
    tm = 512
    nd = 2 * SSD_H
    return pl.pallas_call(
        _dt_kernel,
        grid=(T // tm,),
        in_specs=[
            pl.BlockSpec((tm, D), lambda i: (i, 0)),
            pl.BlockSpec((None, nd, D), lambda i: (l, C_DT // nd, 0)),
        ],
        out_specs=[pl.BlockSpec((tm, nd), lambda i: (i, 0)), pl.BlockSpec((nd, tm), lambda i: (0, i))],
        out_shape=[jax.ShapeDtypeStruct((T, nd), F32), jax.ShapeDtypeStruct((nd, T), F32)],
        compiler_params=_cp(1),
        name="dt_proj",
    )(x, wt)


def _mm_swiglu_kernel(x_ref, wg_ref, wu_ref, o_ref, wgb_ref, wub_ref):
    @pl.when(pl.program_id(1) == 0)
    def _():
        wgb_ref[...] = wg_ref[...].astype(BF16)
        wub_ref[...] = wu_ref[...].astype(BF16)

    x = x_ref[...]
    g = jnp.dot(x, wgb_ref[...], preferred_element_type=F32)
    u = jnp.dot(x, wub_ref[...], preferred_element_type=F32)
    o_ref[...] = (_silu(g) * u).astype(o_ref.dtype)


def _matmul_swiglu(x, w, l, tn=512, tm=1024):
    m, k = x.shape
    nb = FFN_H // tn
    return pl.pallas_call(
        _mm_swiglu_kernel,
        grid=(nb, m // tm),
        in_specs=[
            pl.BlockSpec((tm, k), lambda j, i: (i, 0)),
            pl.BlockSpec((None, k, tn), lambda j, i: (l, 0, j)),
            pl.BlockSpec((None, k, tn), lambda j, i: (l, 0, nb + j)),
        ],
        out_specs=pl.BlockSpec((tm, tn), lambda j, i: (i, j)),
        out_shape=jax.ShapeDtypeStruct((m, FFN_H), BF16),
        scratch_shapes=[pltpu.VMEM((k, tn), BF16), pltpu.VMEM((k, tn), BF16)],
        compiler_params=_cp(2, 40),
        name="ffn_up",
    )(x, w, w)


def _merge_kernel(h_ref, oa_ref, ob_ref, oc_ref, wg0, wg1, wg2, wb0, wb1, wb2, gb0, gb1, gb2, o_ref, wgb_ref, wbb_ref):
    wg = (wg0, wg1, wg2)
    wb = (wb0, wb1, wb2)
    gb = (gb0, gb1, gb2)

    @pl.when(pl.program_id(1) == 0)
    def _():
        for g in range(3):
            wgb_ref[g] = wg[g][0].astype(BF16).T
            wbb_ref[g] = wb[g][...].astype(BF16)

    h = h_ref[...]
    acc = None
    for g, o_r in enumerate((oa_ref, ob_ref, oc_ref)):
        gl = jnp.dot(h, wgb_ref[g], preferred_element_type=F32) + gb[g][...]
        p = jnp.dot(o_r[...], wbb_ref[g], preferred_element_type=F32)
        t = jax.nn.sigmoid(gl) * p
        acc = t if acc is None else acc + t
    o_ref[...] = acc.astype(BF16)


def _merge(h, oa, ob, oc, wt_in, w_branch, gate_b, l, tn=512, tm=512):
    nb = D // tn
    xrow = pl.BlockSpec((tm, D), lambda j, i: (i, 0))
    orow = pl.BlockSpec((tm, BRANCH_W), lambda j, i: (i, 0))
    once = pl.Buffered(1)
    assert C_GL % SUBLANES == 0 and tn % SUBLANES == 0

    def wg_spec(g):
        row0 = (C_GL + g * D) // SUBLANES
        return pl.BlockSpec((pl.Element(1), pl.Element(tn), pl.Element(D)),
                            lambda j, i: (l, SUBLANES * (row0 + j * (tn // SUBLANES)), 0), pipeline_mode=once)

    def wb_spec(g):
        return pl.BlockSpec((None, None, BRANCH_W, tn), lambda j, i: (l, g, 0, j), pipeline_mode=once)

    def gb_spec(g):
        return pl.BlockSpec((None, 1, tn), lambda j, i: (l, 0, g * nb + j))

    gate_b3 = gate_b.reshape(DEPTH, 1, 3 * D)
    return pl.pallas_call(
        _merge_kernel,
        grid=(nb, T // tm),
        in_specs=[xrow, orow, orow, orow]
        + [wg_spec(g) for g in range(3)]
        + [wb_spec(g) for g in range(3)]
        + [gb_spec(g) for g in range(3)],
        out_specs=pl.BlockSpec((tm, tn), lambda j, i: (i, j)),
        out_shape=jax.ShapeDtypeStruct((T, D), BF16),
        scratch_shapes=[pltpu.VMEM((3, D, tn), BF16), pltpu.VMEM((3, BRANCH_W, tn), BF16)],
        compiler_params=_cp(2, 48),
        name="merge",
    )(h, oa, ob, oc, wt_in, wt_in, wt_in, w_branch, w_branch, w_branch, gate_b3, gate_b3, gate_b3)


def _proj_resid_kernel(a_ref, w_ref, x_ref, gpost_ref, gate_ref, *rest, with_h, cast_w):
    if cast_w:
        wb_ref = rest[-1]
        rest = rest[:-1]

        @pl.when(pl.program_id(0) == 0)
        def _():
            wb_ref[...] = w_ref[...].astype(BF16)

        w = wb_ref[...]
    else:
        w = w_ref[...]
    y = jnp.dot(a_ref[...], w, preferred_element_type=F32)
    xn = x_ref[...] + gate_ref[...] * _rms(y, gpost_ref[...])
    if with_h:
        gpre_ref, sh_ref, sc_ref, xo_ref, h_ref = rest
        xo_ref[...] = xn
        h_ref[...] = (_rms(xn, gpre_ref[...]) * (1.0 + sc_ref[...]) + sh_ref[...]).astype(BF16)
    else:
        (xo_ref,) = rest
        xo_ref[...] = xn


def _proj_resid(a, w, x, post_gains, mods, l, k_gate, tm, nxt, name):
    k = a.shape[1]
    cast_w = w.dtype != BF16
    row = pl.BlockSpec((tm, D), lambda i: (i, 0))
    in_specs = [
        pl.BlockSpec((tm, k), lambda i: (i, 0)),
        pl.BlockSpec((None, k, D), lambda i: (l, 0, 0), pipeline_mode=pl.Buffered(1)),
        row, _gain_spec(l), _mod_spec(l, k_gate, tm),
    ]
    args = [a, w, x, post_gains.reshape(DEPTH, 1, D), mods]
    out_specs = [row]
    out_shape = [jax.ShapeDtypeStruct((T, D), F32)]
    if nxt is not None:
        gains, ln, k_sh, k_sc = nxt
        in_specs += [_gain_spec(ln), _mod_spec(ln, k_sh, tm), _mod_spec(ln, k_sc, tm)]
        args += [gains.reshape(DEPTH, 1, D), mods, mods]
        out_specs.append(row)
        out_shape.append(jax.ShapeDtypeStruct((T, D), BF16))
    out = pl.pallas_call(
        functools.partial(_proj_resid_kernel, with_h=nxt is not None, cast_w=cast_w),
        grid=(T // tm,),
        in_specs=in_specs,
        out_specs=out_specs,
        out_shape=out_shape,
        scratch_shapes=[pltpu.VMEM((k, D), BF16)] if cast_w else [],
        compiler_params=_cp(1, VMEM_BIG),
        name=name,
    )(*args)
    return out if nxt is not None else (out[0], None)


ATT_SCALE = HD ** -0.5
ATT_SCALE_LOG2 = ATT_SCALE * math.log2(math.e)


def _softmax_pv(s, v):
    m = jnp.max(s, axis=-1, keepdims=True)
    e = jnp.exp(s - m)
    p = e * (1.0 / jnp.sum(e, axis=-1, keepdims=True))
    return jnp.dot(p.astype(BF16), v, preferred_element_type=F32)


def _softmax_pv2(s1, v1, s2, v2):
    m = jnp.maximum(jnp.max(s1, axis=-1, keepdims=True), jnp.max(s2, axis=-1, keepdims=True)) * ATT_SCALE_LOG2
    e1 = jnp.exp2(s1 * ATT_SCALE_LOG2 - m)
    e2 = jnp.exp2(s2 * ATT_SCALE_LOG2 - m)
    tot = jnp.sum(e1, axis=-1, keepdims=True) + jnp.sum(e2, axis=-1, keepdims=True)
    o = jnp.dot(e1.astype(BF16), v1, preferred_element_type=F32) + jnp.dot(e2.astype(BF16), v2,
                                                                           preferred_element_type=F32)
    return o * (1.0 / tot)


def _ctx_attn_kernel(a_ref, c_ref, qn_ref, kn_ref, oa_ref, oc_ref, nk_ref, nv_ref, kcn_ref, vcn_ref):
    nk_ref[...] = a_ref[:, C_KA:C_KA + NA_H * HD].astype(F32)
    nv_ref[...] = a_ref[:, C_VA:C_VA + NA_H * HD].astype(F32)
    for h in range(NA_H):
        q = a_ref[:, C_QA + h * HD:C_QA + (h + 1) * HD]
        k = a_ref[:, C_KA + h * HD:C_KA + (h + 1) * HD]
        v = a_ref[:, C_VA + h * HD:C_VA + (h + 1) * HD]
        s = lax.dot_general(q, k, NT_DIMS, preferred_element_type=F32) * ATT_SCALE
        oa_ref[:, h * HD:(h + 1) * HD] = _softmax_pv(s, v).astype(BF16)
    kc0 = C_KC - C_QC
    vc0 = C_VC - C_QC
    vcn_ref[...] = c_ref[:, vc0:vc0 + GQA_KV * HD].astype(F32)
    for kv in range(GQA_KV):
        k = _rms(c_ref[:, kc0 + kv * HD:kc0 + (kv + 1) * HD].astype(F32), kn_ref[...])
        kcn_ref[:, kv * HD:(kv + 1) * HD] = k
        kb = k.astype(BF16)
        v = c_ref[:, vc0 + kv * HD:vc0 + (kv + 1) * HD]
        for g in range(GQA_G):
            h = kv * GQA_G + g
            q = _rms(c_ref[:, h * HD:(h + 1) * HD].astype(F32), qn_ref[...]).astype(BF16)
            s = lax.dot_general(q, kb, NT_DIMS, preferred_element_type=F32) * ATT_SCALE
            oc_ref[:, h * HD:(h + 1) * HD] = _softmax_pv(s, v).astype(BF16)


def _ctx_attn(u, qn, kn, l):
    wa = 3 * NA_H * HD
    wc = C_Z - C_QC
    rows = lambda w: pl.BlockSpec((L_CTX, w), lambda b: (b, 0))
    return pl.pallas_call(
        _ctx_attn_kernel,
        grid=(B_CTX,),
        in_specs=[
            pl.BlockSpec((L_CTX, wa), lambda b: (b, 0)),
            pl.BlockSpec((L_CTX, wc), lambda b: (b, C_QC // wc)),
            pl.BlockSpec((None, 1, HD), lambda b: (l, 0, 0)),
            pl.BlockSpec((None, 1, HD), lambda b: (l, 0, 0)),
        ],
        out_specs=[rows(BRANCH_W), rows(BRANCH_W), rows(NA_H * HD), rows(NA_H * HD), rows(GQA_KV * HD),
                   rows(GQA_KV * HD)],
        out_shape=[
            jax.ShapeDtypeStruct((T, BRANCH_W), BF16),
            jax.ShapeDtypeStruct((T, BRANCH_W), BF16),
            jax.ShapeDtypeStruct((T_CTX, NA_H * HD), F32),
            jax.ShapeDtypeStruct((T_CTX, NA_H * HD), F32),
            jax.ShapeDtypeStruct((T_CTX, GQA_KV * HD), F32),
            jax.ShapeDtypeStruct((T_CTX, GQA_KV * HD), F32),
        ],
        compiler_params=_cp(1),
        name="ctx_attn",
    )(u, u, qn.reshape(DEPTH, 1, HD), kn.reshape(DEPTH, 1, HD))


def _lat_prep_kernel(c_ref, qn_ref, kn_ref, cos_ref, sin_ref, qc_ref, kc_ref):
    cos = cos_ref[...]
    sin = sin_ref[...]
    lane = lax.broadcasted_iota(jnp.int32, cos.shape, 1)
    first = (lane & (HD // 2 - 1)) < (HD // 4)

    def rope(x):
        rot = jnp.where(first, -pltpu.roll(x, HD - HD // 4, 1), pltpu.roll(x, HD // 4, 1))
        return x * cos + rot * sin

    for h in range(GQA_H):
        x = c_ref[:, h * HD:(h + 1) * HD].astype(F32)
        qc_ref[:, h * HD:(h + 1) * HD] = rope(_rms(x, qn_ref[...])).astype(BF16)
    kc0 = C_KC - C_QC
    for kv in range(GQA_KV):
        x = c_ref[:, kc0 + kv * HD:kc0 + (kv + 1) * HD].astype(F32)
        kc_ref[:, kv * HD:(kv + 1) * HD] = rope(_rms(x, kn_ref[...])).astype(BF16)


def _lat_prep(u, qn, kn, cos_t, sin_t, l):
    tm = ROW_TILE
    wc = C_Z - C_QC
    r0 = T_CTX // tm
    nper = L_LAT // tm
    return pl.pallas_call(
        _lat_prep_kernel,
        grid=(T_LAT // tm,),
        in_specs=[
            pl.BlockSpec((tm, wc), lambda i: (r0 + i, C_QC // wc)),
            pl.BlockSpec((None, 1, HD), lambda i: (l, 0, 0)),
            pl.BlockSpec((None, 1, HD), lambda i: (l, 0, 0)),
            pl.BlockSpec((tm, HD), lambda i: (i % nper, 0)),
            pl.BlockSpec((tm, HD), lambda i: (i % nper, 0)),
        ],
        out_specs=[
            pl.BlockSpec((tm, GQA_H * HD), lambda i: (i, 0)),
            pl.BlockSpec((tm, GQA_KV * HD), lambda i: (i, 0)),
        ],
        out_shape=[
            jax.ShapeDtypeStruct((T_LAT, GQA_H * HD), BF16),
            jax.ShapeDtypeStruct((T_LAT, GQA_KV * HD), BF16),
        ],
        compiler_params=_cp(1),
        name="lat_prep",
    )(u, qn.reshape(DEPTH, 1, HD), kn.reshape(DEPTH, 1, HD), cos_t, sin_t)


def _lat_gqa_kernel(q_ref, kl_ref, vl_ref, ck_ref, cv_ref, oc_in_ref, o_ref):
    del oc_in_ref
    kl = kl_ref[...]
    vl = vl_ref[...]
    ck = ck_ref[...]
    cv = cv_ref[...]
    for g in range(GQA_G):
        q = q_ref[:, g * HD:(g + 1) * HD]
        s1 = lax.dot_general(q, kl, NT_DIMS, preferred_element_type=F32)
        s2 = lax.dot_general(q, ck, NT_DIMS, preferred_element_type=F32)
        o_ref[:, g * HD:(g + 1) * HD] = _softmax_pv2(s1, vl, s2, cv).astype(BF16)


def _lat_gqa(qc, kc, u, ck, cv, oc, l, tq=512):
    nq = L_LAT // tq
    r0 = T_CTX // tq
    return pl.pallas_call(
        _lat_gqa_kernel,
        grid=(B_LAT, GQA_KV, nq),
        in_specs=[
            pl.BlockSpec((tq, GQA_G * HD), lambda b, kv, i: (b * nq + i, kv)),
            pl.BlockSpec((L_LAT, HD), lambda b, kv, i: (b, kv)),
            pl.BlockSpec((L_LAT, HD), lambda b, kv, i: (T_CTX // L_LAT + b, C_VC // HD + kv)),
            pl.BlockSpec((None, None, PAST, HD), lambda b, kv, i: (b, l, 0, kv)),
            pl.BlockSpec((None, None, PAST, HD), lambda b, kv, i: (b, l, 0, kv)),
            pl.BlockSpec(memory_space=pl.ANY),
        ],
        out_specs=pl.BlockSpec((tq, GQA_G * HD), lambda b, kv, i: (r0 + b * nq + i, kv)),
        out_shape=jax.ShapeDtypeStruct((T, GQA_H * HD), BF16),
        input_output_aliases={5: 0},
        compiler_params=_cp(3),
        name="lat_gqa",
    )(qc, kc, u, ck, cv, oc)


NA_QROWS = 4
NA_UROWS = 12
NA_TQ = NA_QROWS * GRID_W
NA_TK = NA_UROWS * GRID_W


def _lat_na_kernel(q_ref, k_ref, v_ref, ck_ref, cv_ref, t2_ref, oa_in_ref, o_ref, bias_ref):
    del oa_in_ref
    j = pl.program_id(2)
    ustart = jnp.clip(NA_QROWS * j - NA_ROWS // 2, 0, GRID_R - NA_UROWS)
    left = lax.broadcasted_iota(jnp.int32, (GRID_W, 2 * GRID_W), 1) < GRID_W
    for a in range(NA_QROWS):
        r = NA_QROWS * j + a
        rs = jnp.clip(r - NA_ROWS // 2, 0, GRID_R - NA_ROWS)
        for m in range(NA_UROWS // 2):
            kr0 = ustart + 2 * m
            v0 = ((kr0 >= rs) & (kr0 < rs + NA_ROWS)).astype(jnp.int32)
            v1 = ((kr0 + 1 >= rs) & (kr0 + 1 < rs + NA_ROWS)).astype(jnp.int32)
            idx = jnp.clip(kr0 - r + NA_ROWS, 0, 2 * NA_ROWS - 1)
            ok = jnp.where(left, v0, v1) > 0
            bias_ref[a * GRID_W:(a + 1) * GRID_W, m * 2 * GRID_W:(m + 1) * 2 * GRID_W] = jnp.where(ok, t2_ref[idx], NEG)
    start = pl.multiple_of(ustart * GRID_W, GRID_W)
    q = q_ref[...]
    kl = k_ref[pl.ds(start, NA_TK), :]
    vl = v_ref[pl.ds(start, NA_TK), :]
    s_loc = lax.dot_general(q, kl, NT_DIMS, preferred_element_type=F32) + bias_ref[...]
    s_ctx = lax.dot_general(q, ck_ref[...], NT_DIMS, preferred_element_type=F32)
    o_ref[...] = _softmax_pv2(s_loc, vl, s_ctx, cv_ref[...]).astype(BF16)


def _lat_na(u, ck, cv, t2, oa, l):
    nj = GRID_R // NA_QROWS
    r0 = T_CTX // NA_TQ
    s0 = T_CTX // L_LAT
    return pl.pallas_call(
        _lat_na_kernel,
        grid=(B_LAT, NA_H, nj),
        in_specs=[
            pl.BlockSpec((NA_TQ, HD), lambda b, h, j: (r0 + b * nj + j, h)),
            pl.BlockSpec((L_LAT, HD), lambda b, h, j: (s0 + b, NA_H + h)),
            pl.BlockSpec((L_LAT, HD), lambda b, h, j: (s0 + b, 2 * NA_H + h)),
            pl.BlockSpec((None, None, PAST, HD), lambda b, h, j: (b, l, 0, h)),
            pl.BlockSpec((None, None, PAST, HD), lambda b, h, j: (b, l, 0, h)),
            pl.BlockSpec((None, None, 2 * NA_ROWS, GRID_W, 2 * GRID_W), lambda b, h, j: (l, h, 0, 0, 0)),
            pl.BlockSpec(memory_space=pl.ANY),
        ],
        out_specs=pl.BlockSpec((NA_TQ, HD), lambda b, h, j: (r0 + b * nj + j, h)),
        out_shape=jax.ShapeDtypeStruct((T, NA_H * HD), BF16),
        scratch_shapes=[pltpu.VMEM((NA_TQ, NA_TK), F32)],
        input_output_aliases={6: 0},
        compiler_params=_cp(3),
        name="lat_na",
    )(u, u, u, ck, cv, t2, oa)


def _na_bias_tiles(rpb):
    w = jnp.arange(GRID_W)
    cs = jnp.clip(w - NA_COLS // 2, 0, GRID_W - NA_COLS)
    cmask = (w[None, :] >= cs[:, None]) & (w[None, :] < cs[:, None] + NA_COLS)
    dc = jnp.clip(w[None, :] - w[:, None] + NA_COLS - 1, 0, 2 * NA_COLS - 2)
    tm = jnp.where(cmask, rpb[:, :, :, dc] / ATT_SCALE, NEG)
    pad = jnp.full(tm.shape[:2] + (1,) + tm.shape[3:], NEG, F32)
    lo = jnp.concatenate([pad, tm], axis=2)
    hi = jnp.concatenate([tm, pad], axis=2)
    return jnp.concatenate([lo, hi], axis=-1).astype(F32)


def _rope_tables():
    nf = HD // 4
    t = jnp.arange(L_LAT)
    inv = ROPE_BASE ** (-jnp.arange(nf, dtype=F32) / nf)
    ar = (t // GRID_W).astype(F32)[:, None] * inv[None, :]
    ac = (t % GRID_W).astype(F32)[:, None] * inv[None, :]
    cos_t = jnp.concatenate([jnp.cos(ar), jnp.cos(ar), jnp.cos(ac), jnp.cos(ac)], axis=-1)
    sin_t = jnp.concatenate([jnp.sin(ar), jnp.sin(ar), jnp.sin(ac), jnp.sin(ac)], axis=-1)
    return cos_t, sin_t


def _conv_kernel(x_ref, w_ref, b_ref, o_ref):
    x = x_ref[...].astype(F32)
    n = x.shape[0]
    row = lax.broadcasted_iota(jnp.int32, x.shape, 0)
    xm = jnp.where(row == 0, 0.0, pltpu.roll(x, 1, 0))
    xp = jnp.where(row == n - 1, 0.0, pltpu.roll(x, n - 1, 0))
    y = w_ref[0:1, :] * xm + w_ref[1:2, :] * x + w_ref[2:3, :] * xp + b_ref[...]
    o_ref[...] = _silu(y)


def _conv(u, conv_w, conv_b, l, nseq, seq_len, row0):
    tc = 512
    nc = CONV_CH // tc
    rb0 = row0 // seq_len
    cb0 = C_XBC // tc
    return pl.pallas_call(
        _conv_kernel,
        grid=(nseq, nc),
        in_specs=[
            pl.BlockSpec((seq_len, tc), lambda b, j: (rb0 + b, cb0 + j)),
            pl.BlockSpec((None, 3, tc), lambda b, j: (l, 0, j)),
            pl.BlockSpec((None, 1, tc), lambda b, j: (l, 0, j)),
        ],
        out_specs=pl.BlockSpec((seq_len, tc), lambda b, j: (b, j)),
        out_shape=jax.ShapeDtypeStruct((nseq * seq_len, CONV_CH), F32),
        compiler_params=_cp(2),
        name="conv",
    )(u, conv_w, conv_b.reshape(DEPTH, 1, CONV_CH))


def _softplus(x):
    return jnp.maximum(x, 0.0) + jnp.log1p(jnp.exp(-jnp.abs(x)))


def _expand_heads(v, expand, terms):
    out = None
    rem = v
    for _ in range(terms):
        piece = rem.astype(BF16)
        rem = rem - piece.astype(F32)
        t = jnp.dot(piece, expand, preferred_element_type=F32)
        out = t if out is None else out + t
    return out


def _ssd_chunk(xbc, dt_col, dt_row, a_row, a_col, h_ref, reverse):
    x = xbc[:, :SSD_INNER]
    bm = xbc[:, SSD_INNER:SSD_INNER + SSD_G * SSD_N]
    cm = xbc[:, SSD_INNER + SSD_G * SSD_N:]
    li = lax.broadcasted_iota(jnp.int32, (CHUNK, CHUNK), 0)
    si = lax.broadcasted_iota(jnp.int32, (CHUNK, CHUNK), 1)
    mask = (si >= li) if reverse else (si <= li)
    tri = mask.astype(F32)
    tri_t = ((li >= si) if reverse else (li <= si)).astype(F32)
    cum_col = jnp.dot(tri, dt_col * a_row, precision=HI, preferred_element_type=F32)
    cum_row = jnp.dot(dt_row * a_col, tri_t, precision=HI, preferred_element_type=F32)
    tot_row = cum_col[0:1, :] if reverse else cum_col[CHUNK - 1:CHUNK, :]
    col = lax.broadcasted_iota(jnp.int32, (SSD_H, SSD_INNER), 1)
    lo_col = lax.broadcasted_iota(jnp.int32, (SSD_H, SSD_INNER), 0) * SSD_P
    expand = ((col >= lo_col) & (col < lo_col + SSD_P)).astype(BF16)
    w_exp = _expand_heads(jnp.exp(tot_row - cum_col) * dt_col, expand, 2)
    e_exp = _expand_heads(jnp.exp(cum_col), expand, 2)
    dec = _expand_heads(jnp.broadcast_to(jnp.exp(tot_row), (8, SSD_H)), expand, 3)[0:1, :]
    hpg = SSD_H // SSD_G
    xb = x.astype(BF16)
    bgs = [bm[:, g * SSD_N:(g + 1) * SSD_N].astype(BF16) for g in range(SSD_G)]
    cgs = [cm[:, g * SSD_N:(g + 1) * SSD_N].astype(BF16) for g in range(SSD_G)]
    gmat = [lax.dot_general(cgs[g], bgs[g], NT_DIMS, preferred_element_type=F32) for g in range(SSD_G)]
    lo = lax.broadcasted_iota(jnp.int32, (CHUNK, 2 * SSD_P), 1) < SSD_P
    zero = jnp.zeros((CHUNK, 2 * SSD_P), BF16)
    ys = []
    for pr in range(SSD_H // 2):
        g = (2 * pr) // hpg
        ms = []
        for h in (2 * pr, 2 * pr + 1):
            seg = cum_col[:, h:h + 1] - cum_row[h:h + 1, :]
            lm = jnp.exp(jnp.where(mask, seg, NEG))
            ms.append((gmat[g] * lm * dt_row[h:h + 1, :]).astype(BF16))
        xp = xb[:, pr * 2 * SSD_P:(pr + 1) * 2 * SSD_P]
        rhs = jnp.concatenate([jnp.where(lo, xp, zero), jnp.where(lo, zero, xp)], axis=0)
        ys.append(jnp.dot(jnp.concatenate(ms, axis=1), rhs, preferred_element_type=F32))
    y = jnp.concatenate(ys, axis=1)
    h_t = h_ref[...]
    hb = h_t.astype(BF16)
    xw = (x * w_exp).astype(BF16)
    wpg = SSD_INNER // SSD_G
    yo = [jnp.dot(cgs[g], hb[:, g * wpg:(g + 1) * wpg], preferred_element_type=F32) for g in range(SSD_G)]
    st = [lax.dot_general(bgs[g], xw[:, g * wpg:(g + 1) * wpg], TN_DIMS, preferred_element_type=F32)
          for g in range(SSD_G)]
    h_ref[...] = h_t * dec + jnp.concatenate(st, axis=1)
    return y + jnp.concatenate(yo, axis=1) * e_exp


def _ssd_kernel(xf_ref, xb_ref, dtf_ref, dtb_ref, dttf_ref, dttb_ref, prow_ref, pcol_ref, dtot_ref, h0_ref,
                yf_ref, yb_ref, hfin_ref, hf_ref, hb_ref):
    c = pl.program_id(1)

    @pl.when(c == 0)
    def _():
        hf_ref[...] = h0_ref[0]
        hb_ref[...] = h0_ref[1]

    prow = prow_ref[...]
    pcol = pcol_ref[...]
    xf = xf_ref[...]
    dt_col = _softplus(dtf_ref[:, 0:SSD_H] + prow[0:1, :])
    dt_row = _softplus(dttf_ref[0:SSD_H, :] + pcol[:, 0:1])
    yf = _ssd_chunk(xf, dt_col, dt_row, -jnp.exp(prow[2:3, :]), -jnp.exp(pcol[:, 2:3]), hf_ref, False)
    yf_ref[...] = yf + dtot_ref[...] * xf[:, :SSD_INNER]
    dt_col = _softplus(dtb_ref[:, SSD_H:2 * SSD_H] + prow[1:2, :])
    dt_row = _softplus(dttb_ref[SSD_H:2 * SSD_H, :] + pcol[:, 1:2])
    yb_ref[...] = _ssd_chunk(xb_ref[...], dt_col, dt_row, -jnp.exp(prow[3:4, :]), -jnp.exp(pcol[:, 3:4]),
                             hb_ref, True)

    @pl.when(c == pl.num_programs(1) - 1)
    def _():
        hfin_ref[0] = hf_ref[...]
        hfin_ref[1] = hb_ref[...]


def _ssd(xbc, u_dt, dt_t, prow, pcol, dtot, h0, nseq, seq_len, row0):
    nc = seq_len // CHUNK
    cb0 = row0 // CHUNK
    rows = nseq * seq_len
    nd = 2 * SSD_H
    fwd = lambda b, c: b * nc + c
    bwd = lambda b, c: b * nc + (nc - 1 - c)
    return pl.pallas_call(
        _ssd_kernel,
        grid=(nseq, nc),
        in_specs=[
            pl.BlockSpec((CHUNK, CONV_CH), lambda b, c: (fwd(b, c), 0)),
            pl.BlockSpec((CHUNK, CONV_CH), lambda b, c: (bwd(b, c), 0)),
            pl.BlockSpec((CHUNK, nd), lambda b, c: (cb0 + fwd(b, c), 0)),
            pl.BlockSpec((CHUNK, nd), lambda b, c: (cb0 + bwd(b, c), 0)),
            pl.BlockSpec((nd, CHUNK), lambda b, c: (0, cb0 + fwd(b, c))),
            pl.BlockSpec((nd, CHUNK), lambda b, c: (0, cb0 + bwd(b, c))),
            pl.BlockSpec((4, SSD_H), lambda b, c: (0, 0)),
            pl.BlockSpec((SSD_H, 4), lambda b, c: (0, 0)),
            pl.BlockSpec((1, SSD_INNER), lambda b, c: (0, 0)),
            pl.BlockSpec((None, 2, SSD_N, SSD_INNER), lambda b, c: (b, 0, 0, 0)),
        ],
        out_specs=[
            pl.BlockSpec((CHUNK, SSD_INNER), lambda b, c: (fwd(b, c), 0)),
            pl.BlockSpec((CHUNK, SSD_INNER), lambda b, c: (bwd(b, c), 0)),
            pl.BlockSpec((None, 2, SSD_N, SSD_INNER), lambda b, c: (b, 0, 0, 0)),
        ],
        out_shape=[
            jax.ShapeDtypeStruct((rows, SSD_INNER), F32),
            jax.ShapeDtypeStruct((rows, SSD_INNER), F32),
            jax.ShapeDtypeStruct((nseq, 2, SSD_N, SSD_INNER), F32),
        ],
        scratch_shapes=[pltpu.VMEM((SSD_N, SSD_INNER), F32), pltpu.VMEM((SSD_N, SSD_INNER), F32)],
        compiler_params=_cp(2),
        name="ssd",
    )(xbc, xbc, u_dt, u_dt, dt_t, dt_t, prow, pcol, dtot, h0)


def _ssd_gate_kernel(yf_ref, yb_ref, z_ref, g_ref, *rest):
    o_ref = rest[-1]
    y = (yf_ref[...] + yb_ref[...]) * _silu(z_ref[...].astype(F32))
    o_ref[...] = _rms(y, g_ref[...]).astype(BF16)


def _ssd_gate(yf, yb, u, norm_w, l, row0, ob=None):
    tm = ROW_TILE
    rows = yf.shape[0]
    rb0 = row0 // tm
    row = pl.BlockSpec((tm, SSD_INNER), lambda i: (i, 0))
    in_specs = [
        row, row,
        pl.BlockSpec((pl.Element(tm), pl.Element(SSD_INNER)), lambda i: (tm * (rb0 + i), C_Z)),
        pl.BlockSpec((None, 1, SSD_INNER), lambda i: (l, 0, 0)),
    ]
    args = [yf, yb, u, norm_w.reshape(DEPTH, 1, SSD_INNER)]
    aliases = {}
    if ob is not None:
        in_specs.append(pl.BlockSpec(memory_space=pl.ANY))
        args.append(ob)
        aliases = {4: 0}
    return pl.pallas_call(
        _ssd_gate_kernel,
        grid=(rows // tm,),
        in_specs=in_specs,
        out_specs=pl.BlockSpec((tm, SSD_INNER), lambda i: (rb0 + i, 0)),
        out_shape=jax.ShapeDtypeStruct((T, SSD_INNER), BF16),
        input_output_aliases=aliases,
        compiler_params=_cp(1),
        name="ssd_gate",
    )(*args)


def kernel(x_prompt, x_sample, cache_na_k, cache_na_v, cache_gqa_k, cache_gqa_v, state_ssd, c, c_ctx, ada_w, ada_b, norm_mix_pre, norm_mix_post, norm_ffn_pre, norm_ffn_post, w_in, gate_b, na_rpb, gqa_q_norm, gqa_k_norm, ssd_conv_w, ssd_conv_b, ssd_dt_bias, ssd_a_log, ssd_d, ssd_norm_w, w_branch, w_out, ffn_w_up, ffn_w_down):
    x = jnp.concatenate([x_prompt.reshape(T_CTX, D), x_sample.reshape(T_LAT, D)], axis=0)
    cvec = jnp.concatenate([c_ctx[None, :], c, jnp.zeros((8 - 1 - B_LAT, D), F32)], axis=0)
    mods = _compute_mods(cvec, ada_w, ada_b).reshape(DEPTH, 8, 6, 1, D)
    wt_in = jnp.swapaxes(w_in, 1, 2)
    w_down = ffn_w_down.astype(BF16)
    t2 = _na_bias_tiles(na_rpb)
    cos_t, sin_t = _rope_tables()
    ck_a = cache_na_k.reshape(B_LAT, DEPTH, PAST, NA_H * HD).astype(BF16)
    cv_a = cache_na_v.reshape(B_LAT, DEPTH, PAST, NA_H * HD).astype(BF16)
    ck_c = cache_gqa_k.reshape(B_LAT, DEPTH, PAST, GQA_KV * HD).astype(BF16)
    cv_c = cache_gqa_v.reshape(B_LAT, DEPTH, PAST, GQA_KV * HD).astype(BF16)
    h0_lat = state_ssd.transpose(0, 1, 2, 5, 3, 4).reshape(B_LAT, DEPTH, 2, SSD_N, SSD_INNER)
    h0_ctx = jnp.zeros((B_CTX, 2, SSD_N, SSD_INNER), F32)
    prow = jnp.concatenate([ssd_dt_bias, ssd_a_log], axis=1)
    pcol = prow.transpose(0, 2, 1)
    dtot = jnp.repeat(ssd_d[:, 0] + ssd_d[:, 1], SSD_P, axis=-1)[:, None, :]

    h = _normmod(x, norm_mix_pre, mods, 0, 0, 1)
    na_k, na_v, gqa_k, gqa_v, ssd_st = [], [], [], [], []
    for l in range(DEPTH):
        u = _matmul_t(h, wt_in, l, N_MAIN, 1024, 1024, BF16, "in_proj")
        u_dt, dt_t = _dt_proj(h, wt_in, l)
        oa, oc, nk, nv, kcn, vcn = _ctx_attn(u, gqa_q_norm, gqa_k_norm, l)
        qc_l, kc_l = _lat_prep(u, gqa_q_norm, gqa_k_norm, cos_t, sin_t, l)
        oc = _lat_gqa(qc_l, kc_l, u, ck_c, cv_c, oc, l)
        oa = _lat_na(u, ck_a, cv_a, t2, oa, l)
        xbc_c = _conv(u, ssd_conv_w, ssd_conv_b, l, B_CTX, L_CTX, 0)
        xbc_l = _conv(u, ssd_conv_w, ssd_conv_b, l, B_LAT, L_LAT, T_CTX)
        yf_c, yb_c, hfin = _ssd(xbc_c, u_dt, dt_t, prow[l], pcol[l], dtot[l], h0_ctx, B_CTX, L_CTX, 0)
        yf_l, yb_l, _ = _ssd(xbc_l, u_dt, dt_t, prow[l], pcol[l], dtot[l], h0_lat[:, l], B_LAT, L_LAT, T_CTX)
        ob = _ssd_gate(yf_c, yb_c, u, ssd_norm_w, l, 0)
        ob = _ssd_gate(yf_l, yb_l, u, ssd_norm_w, l, T_CTX, ob)
        mix = _merge(h, oa, ob, oc, wt_in, w_branch, gate_b, l)
        x, h2 = _proj_resid(mix, w_out, x, norm_mix_post, mods, l, 2, 512, (norm_ffn_pre, l, 3, 4), "out_proj")
        act = _matmul_swiglu(h2, ffn_w_up, l)
        nxt = (norm_mix_pre, l + 1, 0, 1) if l + 1 < DEPTH else None
        x, h = _proj_resid(act, w_down, x, norm_ffn_post, mods, l, 5, 256, nxt, "ffn_down")
        na_k.append(nk.reshape(B_CTX, L_CTX, NA_H, HD))
        na_v.append(nv.reshape(B_CTX, L_CTX, NA_H, HD))
        gqa_k.append(kcn.reshape(B_CTX, L_CTX, GQA_KV, HD))
        gqa_v.append(vcn.reshape(B_CTX, L_CTX, GQA_KV, HD))
        ssd_st.append(hfin.reshape(B_CTX, 2, SSD_N, SSD_H, SSD_P).transpose(0, 1, 3, 4, 2))
    y_prompt = x[:T_CTX].reshape(B_CTX, L_CTX, D)
    y_sample = x[T_CTX:].reshape(B_LAT, L_LAT, D)
    return (y_prompt, y_sample, jnp.stack(na_k, axis=1), jnp.stack(na_v, axis=1), jnp.stack(gqa_k, axis=1),
            jnp.stack(gqa_v, axis=1), jnp.stack(ssd_st, axis=1))
```

```python
import functools
import math

import jax
import jax.numpy as jnp
from jax import lax
from jax.experimental import pallas as pl
from jax.experimental.pallas import tpu as pltpu

F32 = jnp.float32
BF16 = jnp.bfloat16
HI = lax.Precision.HIGHEST
EPS = 1e-6
NEG = -1e30

D = 2048
DEPTH = 4
B_CTX, L_CTX = 16, 256
B_LAT, L_LAT = 4, 2048
PAST = 512
GRID_W = 64
GRID_R = L_LAT // GRID_W
NA_H = 8
HD = 128
NA_ROWS, NA_COLS = 8, 16
GQA_H, GQA_KV = 8, 2
GQA_G = GQA_H // GQA_KV
ROPE_BASE = 10000.0
SSD_INNER = 1024
SSD_P = 64
SSD_H = SSD_INNER // SSD_P
SSD_N = 128
SSD_G = 2
CHUNK = 128
CONV_CH = SSD_INNER + 2 * SSD_G * SSD_N
BRANCH_W = 1024
FFN_H = 5632
T_CTX = B_CTX * L_CTX
T_LAT = B_LAT * L_LAT
T = T_CTX + T_LAT

C_QA, C_KA, C_VA = 0, NA_H * HD, 2 * NA_H * HD
C_QC = 3 * NA_H * HD
C_KC = C_QC + GQA_H * HD
C_VC = C_KC + GQA_KV * HD
C_Z = C_VC + GQA_KV * HD
C_XBC = C_Z + SSD_INNER
C_DT = C_XBC + CONV_CH
C_GL = C_DT + 2 * SSD_H
N_MAIN = C_DT

NT_DIMS = (((1,), (1,)), ((), ()))
TN_DIMS = (((0,), (0,)), ((), ()))

ROW_TILE = 256
SUBLANES = 8
VMEM_BIG = 56


def _cp(n_axes, vmem_mb=None):
    kw = dict(dimension_semantics=("arbitrary",) * n_axes)
    if vmem_mb is not None:
        kw["vmem_limit_bytes"] = vmem_mb * 1024 * 1024
    return pltpu.CompilerParams(**kw)


def _mod_row(i, tm):
    n_ctx = T_CTX // tm
    n_lat = L_LAT // tm
    return jnp.where(i < n_ctx, 0, 1 + (i - n_ctx) // n_lat)


def _rms(x, g):
    return x * lax.rsqrt(jnp.mean(x * x, axis=-1, keepdims=True) + EPS) * g


def _silu(x):
    return x * jax.nn.sigmoid(x)


def _mods_kernel(c_ref, w_ref, b_ref, o_ref):
    s = _silu(c_ref[...]).astype(BF16)
    o_ref[...] = jnp.dot(s, w_ref[...].astype(BF16), preferred_element_type=F32) + b_ref[...]


def _compute_mods(cvec, ada_w, ada_b):
    tn = 1024
    n = 6 * D
    return pl.pallas_call(
        _mods_kernel,
        grid=(DEPTH, n // tn),
        in_specs=[
            pl.BlockSpec((8, D), lambda l, j: (0, 0)),
            pl.BlockSpec((None, D, tn), lambda l, j: (l, 0, j)),
            pl.BlockSpec((None, 1, tn), lambda l, j: (l, 0, j)),
        ],
        out_specs=pl.BlockSpec((None, 8, tn), lambda l, j: (l, 0, j)),
        out_shape=jax.ShapeDtypeStruct((DEPTH, 8, n), F32),
        compiler_params=_cp(2),
        name="mods",
    )(cvec, ada_w, ada_b.reshape(DEPTH, 1, n))


def _mod_spec(l, k, tm):
    return pl.BlockSpec((None, None, None, 1, D), lambda i: (l, _mod_row(i, tm), k, 0, 0))


def _gain_spec(l):
    return pl.BlockSpec((None, 1, D), lambda i: (l, 0, 0))


def _normmod_kernel(xc_ref, xl_ref, g_ref, sh_ref, sc_ref, x_ref, h_ref):
    x = jnp.where(pl.program_id(0) < T_CTX // ROW_TILE, xc_ref[...], xl_ref[...])
    x_ref[...] = x
    h_ref[...] = (_rms(x, g_ref[...]) * (1.0 + sc_ref[...]) + sh_ref[...]).astype(BF16)


def _normmod(x_ctx, x_lat, gains, mods, l, k_sh, k_sc):
    tm = ROW_TILE
    n_ctx = T_CTX // tm
    row = pl.BlockSpec((tm, D), lambda i: (i, 0))
    return pl.pallas_call(
        _normmod_kernel,
        grid=(T // tm,),
        in_specs=[
            pl.BlockSpec((tm, D), lambda i: (jnp.minimum(i, n_ctx - 1), 0)),
            pl.BlockSpec((tm, D), lambda i: (jnp.maximum(i - n_ctx, 0), 0)),
            _gain_spec(l),
            _mod_spec(l, k_sh, tm),
            _mod_spec(l, k_sc, tm),
        ],
        out_specs=[row, row],
        out_shape=[jax.ShapeDtypeStruct((T, D), F32), jax.ShapeDtypeStruct((T, D), BF16)],
        compiler_params=_cp(1),
        name="normmod",
    )(x_ctx, x_lat, gains.reshape(DEPTH, 1, D), mods, mods)


def _mm_t_kernel(x_ref, wt_ref, o_ref, wb_ref):
    @pl.when(pl.program_id(1) == 0)
    def _():
        wb_ref[...] = wt_ref[...].astype(BF16).T

    o_ref[...] = jnp.dot(x_ref[...], wb_ref[...], preferred_element_type=F32).astype(o_ref.dtype)


def _matmul_t(x, wt, l, ncols, tn, tm, out_dtype, name):
    m, k = x.shape
    return pl.pallas_call(
        _mm_t_kernel,
        grid=(ncols // tn, m // tm),
        in_specs=[
            pl.BlockSpec((tm, k), lambda j, i: (i, 0)),
            pl.BlockSpec((None, tn, k), lambda j, i: (l, j, 0)),
        ],
        out_specs=pl.BlockSpec((tm, tn), lambda j, i: (i, j)),
        out_shape=jax.ShapeDtypeStruct((m, ncols), out_dtype),
        scratch_shapes=[pltpu.VMEM((k, tn), BF16)],
        compiler_params=_cp(2, 40),
        name=name,
    )(x, wt)


def _dt_kernel(x_ref, wt_ref, o_ref, ot_ref):
    w = wt_ref[...].astype(BF16)
    x = x_ref[...]
    o_ref[...] = lax.dot_general(x, w, NT_DIMS, preferred_element_type=F32)
    ot_ref[...] = lax.dot_general(w, x, NT_DIMS, preferred_element_type=F32)


def _dt_proj(x, wt, l):
    tm = 512
    nd = 2 * SSD_H
    return pl.pallas_call(
        _dt_kernel,
        grid=(T // tm,),
        in_specs=[
            pl.BlockSpec((tm, D), lambda i: (i, 0)),
            pl.BlockSpec((None, nd, D), lambda i: (l, C_DT // nd, 0)),
        ],
        out_specs=[pl.BlockSpec((tm, nd), lambda i: (i, 0)), pl.BlockSpec((nd, tm), lambda i: (0, i))],
        out_shape=[jax.ShapeDtypeStruct((T, nd), F32), jax.ShapeDtypeStruct((nd, T), F32)],
        compiler_params=_cp(1),
        name="dt_proj",
    )(x, wt)


def _mm_swiglu_kernel(x_ref, wg_ref, wu_ref, o_ref, wgb_ref, wub_ref):
    @pl.when(pl.program_id(1) == 0)
    def _():
        wgb_ref[...] = wg_ref[...].astype(BF16)
        wub_ref[...] = wu_ref[...].astype(BF16)

    x = x_ref[...]
    g = jnp.dot(x, wgb_ref[...], preferred_element_type=F32)
    u = jnp.dot(x, wub_ref[...], preferred_element_type=F32)
    o_ref[...] = (_silu(g) * u).astype(o_ref.dtype)


def _matmul_swiglu(x, w, l, tn=512, tm=1024):
    m, k = x.shape
    nb = FFN_H // tn
    return pl.pallas_call(
        _mm_swiglu_kernel,
        grid=(nb, m // tm),
        in_specs=[
            pl.BlockSpec((tm, k), lambda j, i: (i, 0)),
            pl.BlockSpec((None, k, tn), lambda j, i: (l, 0, j)),
            pl.BlockSpec((None, k, tn), lambda j, i: (l, 0, nb + j)),
        ],
        out_specs=pl.BlockSpec((tm, tn), lambda j, i: (i, j)),
        out_shape=jax.ShapeDtypeStruct((m, FFN_H), BF16),
        scratch_shapes=[pltpu.VMEM((k, tn), BF16), pltpu.VMEM((k, tn), BF16)],
        compiler_params=_cp(2, 40),
        name="ffn_up",
    )(x, w, w)


def _merge_kernel(h_ref, oa_ref, ob_ref, oc_ref, wg0, wg1, wg2, wb0, wb1, wb2, gb0, gb1, gb2, o_ref, wgb_ref, wbb_ref):
    wg = (wg0, wg1, wg2)
    wb = (wb0, wb1, wb2)
    gb = (gb0, gb1, gb2)

    @pl.when(pl.program_id(1) == 0)
    def _():
        for g in range(3):
            wgb_ref[g] = wg[g][0].astype(BF16).T
            wbb_ref[g] = wb[g][...].astype(BF16)

    h = h_ref[...]
    acc = None
    for g, o_r in enumerate((oa_ref, ob_ref, oc_ref)):
        gl = jnp.dot(h, wgb_ref[g], preferred_element_type=F32) + gb[g][...]
        p = jnp.dot(o_r[...], wbb_ref[g], preferred_element_type=F32)
        t = jax.nn.sigmoid(gl) * p
        acc = t if acc is None else acc + t
    o_ref[...] = acc.astype(BF16)


def _merge(h, oa, ob, oc, wt_in, w_branch, gate_b, l, tn=512, tm=512):
    nb = D // tn
    xrow = pl.BlockSpec((tm, D), lambda j, i: (i, 0))
    orow = pl.BlockSpec((tm, BRANCH_W), lambda j, i: (i, 0))
    once = pl.Buffered(1)
    assert C_GL % SUBLANES == 0 and tn % SUBLANES == 0

    def wg_spec(g):
        row0 = (C_GL + g * D) // SUBLANES
        return pl.BlockSpec((pl.Element(1), pl.Element(tn), pl.Element(D)),
                            lambda j, i: (l, SUBLANES * (row0 + j * (tn // SUBLANES)), 0), pipeline_mode=once)

    def wb_spec(g):
        return pl.BlockSpec((None, None, BRANCH_W, tn), lambda j, i: (l, g, 0, j), pipeline_mode=once)

    def gb_spec(g):
        return pl.BlockSpec((None, 1, tn), lambda j, i: (l, 0, g * nb + j))

    gate_b3 = gate_b.reshape(DEPTH, 1, 3 * D)
    return pl.pallas_call(
        _merge_kernel,
        grid=(nb, T // tm),
        in_specs=[xrow, orow, orow, orow]
        + [wg_spec(g) for g in range(3)]
        + [wb_spec(g) for g in range(3)]
        + [gb_spec(g) for g in range(3)],
        out_specs=pl.BlockSpec((tm, tn), lambda j, i: (i, j)),
        out_shape=jax.ShapeDtypeStruct((T, D), BF16),
        scratch_shapes=[pltpu.VMEM((3, D, tn), BF16), pltpu.VMEM((3, BRANCH_W, tn), BF16)],
        compiler_params=_cp(2, 48),
        name="merge",
    )(h, oa, ob, oc, wt_in, wt_in, wt_in, w_branch, w_branch, w_branch, gate_b3, gate_b3, gate_b3)


def _proj_resid_kernel(a_ref, w_ref, x_ref, gpost_ref, gate_ref, *rest, with_h, cast_w):
    if cast_w:
        wb_ref = rest[-1]
        rest = rest[:-1]

        @pl.when(pl.program_id(0) == 0)
        def _():
            wb_ref[...] = w_ref[...].astype(BF16)

        w = wb_ref[...]
    else:
        w = w_ref[...]
    y = jnp.dot(a_ref[...], w, preferred_element_type=F32)
    xn = x_ref[...] + gate_ref[...] * _rms(y, gpost_ref[...])
    if with_h:
        gpre_ref, sh_ref, sc_ref, xo_ref, h_ref = rest
        xo_ref[...] = xn
        h_ref[...] = (_rms(xn, gpre_ref[...]) * (1.0 + sc_ref[...]) + sh_ref[...]).astype(BF16)
    else:
        oc_ref, ol_ref = rest
        n_ctx = T_CTX // xn.shape[0]

        @pl.when(pl.program_id(0) < n_ctx)
        def _():
            oc_ref[...] = xn

        @pl.when(pl.program_id(0) >= n_ctx)
        def _():
            ol_ref[...] = xn


def _proj_resid(a, w, x, post_gains, mods, l, k_gate, tm, nxt, name):
    k = a.shape[1]
    cast_w = w.dtype != BF16
    row = pl.BlockSpec((tm, D), lambda i: (i, 0))
    in_specs = [
        pl.BlockSpec((tm, k), lambda i: (i, 0)),
        pl.BlockSpec((None, k, D), lambda i: (l, 0, 0), pipeline_mode=pl.Buffered(1)),
        row, _gain_spec(l), _mod_spec(l, k_gate, tm),
    ]
    args = [a, w, x, post_gains.reshape(DEPTH, 1, D), mods]
    out_specs = [row]
    out_shape = [jax.ShapeDtypeStruct((T, D), F32)]
    if nxt is not None:
        gains, ln, k_sh, k_sc = nxt
        in_specs += [_gain_spec(ln), _mod_spec(ln, k_sh, tm), _mod_spec(ln, k_sc, tm)]
        args += [gains.reshape(DEPTH, 1, D), mods, mods]
        out_specs.append(row)
        out_shape.append(jax.ShapeDtypeStruct((T, D), BF16))
    else:
        n_ctx = T_CTX // tm
        out_specs = [pl.BlockSpec((tm, D), lambda i: (jnp.minimum(i, n_ctx - 1), 0)),
                     pl.BlockSpec((tm, D), lambda i: (jnp.maximum(i - n_ctx, 0), 0))]
        out_shape = [jax.ShapeDtypeStruct((T_CTX, D), F32), jax.ShapeDtypeStruct((T_LAT, D), F32)]
    out = pl.pallas_call(
        functools.partial(_proj_resid_kernel, with_h=nxt is not None, cast_w=cast_w),
        grid=(T // tm,),
        in_specs=in_specs,
        out_specs=out_specs,
        out_shape=out_shape,
        scratch_shapes=[pltpu.VMEM((k, D), BF16)] if cast_w else [],
        compiler_params=_cp(1, VMEM_BIG),
        name=name,
    )(*args)
    return out


ATT_SCALE = HD ** -0.5
ATT_SCALE_LOG2 = ATT_SCALE * math.log2(math.e)


def _softmax_pv(s, v):
    m = jnp.max(s, axis=-1, keepdims=True)
    e = jnp.exp(s - m)
    p = e * (1.0 / jnp.sum(e, axis=-1, keepdims=True))
    return jnp.dot(p.astype(BF16), v, preferred_element_type=F32)


def _softmax_pv2(s1, v1, s2, v2):
    m = jnp.maximum(jnp.max(s1, axis=-1, keepdims=True), jnp.max(s2, axis=-1, keepdims=True)) * ATT_SCALE_LOG2
    e1 = jnp.exp2(s1 * ATT_SCALE_LOG2 - m)
    e2 = jnp.exp2(s2 * ATT_SCALE_LOG2 - m)
    tot = jnp.sum(e1, axis=-1, keepdims=True) + jnp.sum(e2, axis=-1, keepdims=True)
    o = jnp.dot(e1.astype(BF16), v1, preferred_element_type=F32) + jnp.dot(e2.astype(BF16), v2,
                                                                           preferred_element_type=F32)
    return o * (1.0 / tot)


def _ctx_attn_kernel(a_ref, c_ref, qn_ref, kn_ref, *rest):
    oa_ref, oc_ref, nk_ref, nv_ref, kcn_ref, vcn_ref = rest[-6:]
    nk_ref[...] = a_ref[:, C_KA:C_KA + NA_H * HD].astype(F32)
    nv_ref[...] = a_ref[:, C_VA:C_VA + NA_H * HD].astype(F32)
    for h in range(NA_H):
        q = a_ref[:, C_QA + h * HD:C_QA + (h + 1) * HD]
        k = a_ref[:, C_KA + h * HD:C_KA + (h + 1) * HD]
        v = a_ref[:, C_VA + h * HD:C_VA + (h + 1) * HD]
        s = lax.dot_general(q, k, NT_DIMS, preferred_element_type=F32) * ATT_SCALE
        oa_ref[:, h * HD:(h + 1) * HD] = _softmax_pv(s, v).astype(BF16)
    kc0 = C_KC - C_QC
    vc0 = C_VC - C_QC
    vcn_ref[...] = c_ref[:, vc0:vc0 + GQA_KV * HD].astype(F32)
    for kv in range(GQA_KV):
        k = _rms(c_ref[:, kc0 + kv * HD:kc0 + (kv + 1) * HD].astype(F32), kn_ref[...])
        kcn_ref[:, kv * HD:(kv + 1) * HD] = k
        kb = k.astype(BF16)
        v = c_ref[:, vc0 + kv * HD:vc0 + (kv + 1) * HD]
        for g in range(GQA_G):
            h = kv * GQA_G + g
            q = _rms(c_ref[:, h * HD:(h + 1) * HD].astype(F32), qn_ref[...]).astype(BF16)
            s = lax.dot_general(q, kb, NT_DIMS, preferred_element_type=F32) * ATT_SCALE
            oc_ref[:, h * HD:(h + 1) * HD] = _softmax_pv(s, v).astype(BF16)


def _ctx_attn(u, qn, kn, l, caches):
    wa = 3 * NA_H * HD
    wc = C_Z - C_QC
    rows = lambda w: pl.BlockSpec((L_CTX, w), lambda b: (b, 0))
    lay = lambda w: pl.BlockSpec((None, None, L_CTX, w), lambda b: (b, l, 0, 0))
    cache_shape = lambda w: jax.ShapeDtypeStruct((B_CTX, DEPTH, L_CTX, w), F32)
    in_specs = [
        pl.BlockSpec((L_CTX, wa), lambda b: (b, 0)),
        pl.BlockSpec((L_CTX, wc), lambda b: (b, C_QC // wc)),
        pl.BlockSpec((None, 1, HD), lambda b: (l, 0, 0)),
        pl.BlockSpec((None, 1, HD), lambda b: (l, 0, 0)),
    ]
    args = [u, u, qn.reshape(DEPTH, 1, HD), kn.reshape(DEPTH, 1, HD)]
    aliases = {}
    if caches is not None:
        in_specs += [pl.BlockSpec(memory_space=pl.ANY)] * 4
        args += list(caches)
        aliases = {4 + n: 2 + n for n in range(4)}
    out = pl.pallas_call(
        _ctx_attn_kernel,
        grid=(B_CTX,),
        in_specs=in_specs,
        out_specs=[rows(BRANCH_W), rows(BRANCH_W), lay(NA_H * HD), lay(NA_H * HD), lay(GQA_KV * HD),
                   lay(GQA_KV * HD)],
        out_shape=[
            jax.ShapeDtypeStruct((T, BRANCH_W), BF16),
            jax.ShapeDtypeStruct((T, BRANCH_W), BF16),
            cache_shape(NA_H * HD), cache_shape(NA_H * HD), cache_shape(GQA_KV * HD), cache_shape(GQA_KV * HD),
        ],
        input_output_aliases=aliases,
        compiler_params=_cp(1),
        name="ctx_attn",
    )(*args)
    return out[0], out[1], tuple(out[2:])


def _rope(x, cos, sin):
    lane = lax.broadcasted_iota(jnp.int32, x.shape, 1)
    first = (lane & (HD // 2 - 1)) < (HD // 4)
    rot = jnp.where(first, -pltpu.roll(x, HD - HD // 4, 1), pltpu.roll(x, HD // 4, 1))
    return x * cos + rot * sin


def _lat_gqa_kernel(q_ref, k_ref, vl_ref, ck_ref, cv_ref, qn_ref, kn_ref, cos_ref, sin_ref, oc_in_ref, o_ref, kr_ref):
    del oc_in_ref
    i = pl.program_id(2)
    tq = q_ref.shape[0]

    @pl.when(i == 0)
    def _():
        k = _rms(k_ref[...].astype(F32), kn_ref[...])
        kr_ref[...] = _rope(k, cos_ref[...], sin_ref[...]).astype(BF16)

    start = pl.multiple_of(i * tq, tq)
    cos = cos_ref[pl.ds(start, tq), :]
    sin = sin_ref[pl.ds(start, tq), :]
    kl = kr_ref[...]
    vl = vl_ref[...]
    ck = ck_ref[...]
    cv = cv_ref[...]
    for g in range(GQA_G):
        q = _rope(_rms(q_ref[:, g * HD:(g + 1) * HD].astype(F32), qn_ref[...]), cos, sin).astype(BF16)
        s1 = lax.dot_general(q, kl, NT_DIMS, preferred_element_type=F32)
        s2 = lax.dot_general(q, ck, NT_DIMS, preferred_element_type=F32)
        o_ref[:, g * HD:(g + 1) * HD] = _softmax_pv2(s1, vl, s2, cv).astype(BF16)


def _lat_gqa(u, ck, cv, qn, kn, cos_t, sin_t, oc, l, tq=512):
    nq = L_LAT // tq
    r0 = T_CTX // tq
    s0 = T_CTX // L_LAT
    wq = GQA_G * HD
    full = lambda b, kv, i: (0, 0)
    return pl.pallas_call(
        _lat_gqa_kernel,
        grid=(B_LAT, GQA_KV, nq),
        in_specs=[
            pl.BlockSpec((tq, wq), lambda b, kv, i: (r0 + b * nq + i, C_QC // wq + kv)),
            pl.BlockSpec((L_LAT, HD), lambda b, kv, i: (s0 + b, C_KC // HD + kv)),
            pl.BlockSpec((L_LAT, HD), lambda b, kv, i: (s0 + b, C_VC // HD + kv)),
            pl.BlockSpec((None, None, PAST, HD), lambda b, kv, i: (b, l, 0, kv)),
            pl.BlockSpec((None, None, PAST, HD), lambda b, kv, i: (b, l, 0, kv)),
            pl.BlockSpec((None, 1, HD), lambda b, kv, i: (l, 0, 0)),
            pl.BlockSpec((None, 1, HD), lambda b, kv, i: (l, 0, 0)),
            pl.BlockSpec((L_LAT, HD), full),
            pl.BlockSpec((L_LAT, HD), full),
            pl.BlockSpec(memory_space=pl.ANY),
        ],
        out_specs=pl.BlockSpec((tq, wq), lambda b, kv, i: (r0 + b * nq + i, kv)),
        out_shape=jax.ShapeDtypeStruct((T, GQA_H * HD), BF16),
        scratch_shapes=[pltpu.VMEM((L_LAT, HD), BF16)],
        input_output_aliases={9: 0},
        compiler_params=_cp(3),
        name="lat_gqa",
    )(u, u, u, ck, cv, qn.reshape(DEPTH, 1, HD), kn.reshape(DEPTH, 1, HD), cos_t, sin_t, oc)


NA_QROWS = 4
NA_UROWS = 12
NA_TQ = NA_QROWS * GRID_W
NA_TK = NA_UROWS * GRID_W
NA_HP = 2


def _lat_na_kernel(q_ref, k_ref, v_ref, ck_ref, cv_ref, t2_ref, oa_in_ref, o_ref, bias_ref):
    del oa_in_ref
    j = pl.program_id(2)
    ustart = jnp.clip(NA_QROWS * j - NA_ROWS // 2, 0, GRID_R - NA_UROWS)
    left = lax.broadcasted_iota(jnp.int32, (GRID_W, 2 * GRID_W), 1) < GRID_W
    for a in range(NA_QROWS):
        r = NA_QROWS * j + a
        rs = jnp.clip(r - NA_ROWS // 2, 0, GRID_R - NA_ROWS)
        for m in range(NA_UROWS // 2):
            kr0 = ustart + 2 * m
            v0 = ((kr0 >= rs) & (kr0 < rs + NA_ROWS)).astype(jnp.int32)
            v1 = ((kr0 + 1 >= rs) & (kr0 + 1 < rs + NA_ROWS)).astype(jnp.int32)
            idx = jnp.clip(kr0 - r + NA_ROWS, 0, 2 * NA_ROWS - 1)
            ok = jnp.where(left, v0, v1) > 0
            for hh in range(NA_HP):
                bias_ref[hh, a * GRID_W:(a + 1) * GRID_W, m * 2 * GRID_W:(m + 1) * 2 * GRID_W] = jnp.where(
                    ok, t2_ref[hh, idx], NEG)
    start = pl.multiple_of(ustart * GRID_W, GRID_W)
    for hh in range(NA_HP):
        cs = slice(hh * HD, (hh + 1) * HD)
        q = q_ref[:, cs]
        kl = k_ref[pl.ds(start, NA_TK), cs]
        vl = v_ref[pl.ds(start, NA_TK), cs]
        s_loc = lax.dot_general(q, kl, NT_DIMS, preferred_element_type=F32) + bias_ref[hh]
        s_ctx = lax.dot_general(q, ck_ref[:, cs], NT_DIMS, preferred_element_type=F32)
        o_ref[:, cs] = _softmax_pv2(s_loc, vl, s_ctx, cv_ref[:, cs]).astype(BF16)


def _lat_na(u, ck, cv, t2, oa, l):
    nj = GRID_R // NA_QROWS
    r0 = T_CTX // NA_TQ
    s0 = T_CTX // L_LAT
    w = NA_HP * HD
    return pl.pallas_call(
        _lat_na_kernel,
        grid=(B_LAT, NA_H // NA_HP, nj),
        in_specs=[
            pl.BlockSpec((NA_TQ, w), lambda b, h, j: (r0 + b * nj + j, C_QA // w + h)),
            pl.BlockSpec((L_LAT, w), lambda b, h, j: (s0 + b, C_KA // w + h)),
            pl.BlockSpec((L_LAT, w), lambda b, h, j: (s0 + b, C_VA // w + h)),
            pl.BlockSpec((None, None, PAST, w), lambda b, h, j: (b, l, 0, h)),
            pl.BlockSpec((None, None, PAST, w), lambda b, h, j: (b, l, 0, h)),
            pl.BlockSpec((None, NA_HP, 2 * NA_ROWS, GRID_W, 2 * GRID_W), lambda b, h, j: (l, h, 0, 0, 0)),
            pl.BlockSpec(memory_space=pl.ANY),
        ],
        out_specs=pl.BlockSpec((NA_TQ, w), lambda b, h, j: (r0 + b * nj + j, h)),
        out_shape=jax.ShapeDtypeStruct((T, NA_H * HD), BF16),
        scratch_shapes=[pltpu.VMEM((NA_HP, NA_TQ, NA_TK), F32)],
        input_output_aliases={6: 0},
        compiler_params=_cp(3),
        name="lat_na",
    )(u, u, u, ck, cv, t2, oa)


def _na_bias_tiles(rpb):
    w = jnp.arange(GRID_W)
    cs = jnp.clip(w - NA_COLS // 2, 0, GRID_W - NA_COLS)
    cmask = (w[None, :] >= cs[:, None]) & (w[None, :] < cs[:, None] + NA_COLS)
    dc = jnp.clip(w[None, :] - w[:, None] + NA_COLS - 1, 0, 2 * NA_COLS - 2)
    tm = jnp.where(cmask, rpb[:, :, :, dc] / ATT_SCALE, NEG)
    pad = jnp.full(tm.shape[:2] + (1,) + tm.shape[3:], NEG, F32)
    lo = jnp.concatenate([pad, tm], axis=2)
    hi = jnp.concatenate([tm, pad], axis=2)
    return jnp.concatenate([lo, hi], axis=-1).astype(F32)


def _rope_tables():
    nf = HD // 4
    t = jnp.arange(L_LAT)
    inv = ROPE_BASE ** (-jnp.arange(nf, dtype=F32) / nf)
    ar = (t // GRID_W).astype(F32)[:, None] * inv[None, :]
    ac = (t % GRID_W).astype(F32)[:, None] * inv[None, :]
    cos_t = jnp.concatenate([jnp.cos(ar), jnp.cos(ar), jnp.cos(ac), jnp.cos(ac)], axis=-1)
    sin_t = jnp.concatenate([jnp.sin(ar), jnp.sin(ar), jnp.sin(ac), jnp.sin(ac)], axis=-1)
    return cos_t, sin_t


def _conv_kernel(x_ref, w_ref, b_ref, o_ref):
    x = x_ref[...].astype(F32)
    n = x.shape[0]
    row = lax.broadcasted_iota(jnp.int32, (SUBLANES, x.shape[1]), 0)
    xm = pltpu.roll(x, 1, 0)
    xm = jnp.concatenate([jnp.where(row == 0, 0.0, xm[:SUBLANES]), xm[SUBLANES:]], axis=0)
    xp = pltpu.roll(x, n - 1, 0)
    xp = jnp.concatenate([xp[:n - SUBLANES], jnp.where(row == SUBLANES - 1, 0.0, xp[n - SUBLANES:])], axis=0)
    y = w_ref[0:1, :] * xm + w_ref[1:2, :] * x + w_ref[2:3, :] * xp + b_ref[...]
    o_ref[...] = _silu(y)


def _conv(u, conv_w, conv_b, l, nseq, seq_len, row0):
    tc = 512
    nc = CONV_CH // tc
    rb0 = row0 // seq_len
    cb0 = C_XBC // tc
    return pl.pallas_call(
        _conv_kernel,
        grid=(nseq, nc),
        in_specs=[
            pl.BlockSpec((seq_len, tc), lambda b, j: (rb0 + b, cb0 + j)),
            pl.BlockSpec((None, 3, tc), lambda b, j: (l, 0, j)),
            pl.BlockSpec((None, 1, tc), lambda b, j: (l, 0, j)),
        ],
        out_specs=pl.BlockSpec((seq_len, tc), lambda b, j: (b, j)),
        out_shape=jax.ShapeDtypeStruct((nseq * seq_len, CONV_CH), F32),
        compiler_params=_cp(2),
        name="conv",
    )(u, conv_w, conv_b.reshape(DEPTH, 1, CONV_CH))


def _softplus(x):
    return jnp.maximum(x, 0.0) + jnp.log1p(jnp.exp(-jnp.abs(x)))


def _expand_heads(v, expand, terms):
    out = None
    rem = v
    for _ in range(terms):
        piece = rem.astype(BF16)
        rem = rem - piece.astype(F32)
        t = jnp.dot(piece, expand, preferred_element_type=F32)
        out = t if out is None else out + t
    return out


def _ssd_chunk(xbc, dt_col, dt_row, a_row, a_col, h_ref, reverse):
    x = xbc[:, :SSD_INNER]
    bm = xbc[:, SSD_INNER:SSD_INNER + SSD_G * SSD_N]
    cm = xbc[:, SSD_INNER + SSD_G * SSD_N:]
    li = lax.broadcasted_iota(jnp.int32, (CHUNK, CHUNK), 0)
    si = lax.broadcasted_iota(jnp.int32, (CHUNK, CHUNK), 1)
    mask = (si >= li) if reverse else (si <= li)
    tri = mask.astype(F32)
    tri_t = ((li >= si) if reverse else (li <= si)).astype(F32)
    cum_col = jnp.dot(tri, dt_col * a_row, precision=HI, preferred_element_type=F32)
    cum_row = jnp.dot(dt_row * a_col, tri_t, precision=HI, preferred_element_type=F32)
    tot_row = cum_col[0:1, :] if reverse else cum_col[CHUNK - 1:CHUNK, :]
    col = lax.broadcasted_iota(jnp.int32, (SSD_H, SSD_INNER), 1)
    lo_col = lax.broadcasted_iota(jnp.int32, (SSD_H, SSD_INNER), 0) * SSD_P
    expand = ((col >= lo_col) & (col < lo_col + SSD_P)).astype(BF16)
    w_exp = _expand_heads(jnp.exp(tot_row - cum_col) * dt_col, expand, 2)
    e_exp = _expand_heads(jnp.exp(cum_col), expand, 2)
    dec = _expand_heads(jnp.broadcast_to(jnp.exp(tot_row), (8, SSD_H)), expand, 3)[0:1, :]
    hpg = SSD_H // SSD_G
    xb = x.astype(BF16)
    bgs = [bm[:, g * SSD_N:(g + 1) * SSD_N].astype(BF16) for g in range(SSD_G)]
    cgs = [cm[:, g * SSD_N:(g + 1) * SSD_N].astype(BF16) for g in range(SSD_G)]
    gmat = [lax.dot_general(cgs[g], bgs[g], NT_DIMS, preferred_element_type=F32) for g in range(SSD_G)]
    lo = lax.broadcasted_iota(jnp.int32, (CHUNK, 2 * SSD_P), 1) < SSD_P
    zero = jnp.zeros((CHUNK, 2 * SSD_P), BF16)
    ys = []
    for pr in range(SSD_H // 2):
        g = (2 * pr) // hpg
        ms = []
        for h in (2 * pr, 2 * pr + 1):
            seg = cum_col[:, h:h + 1] - cum_row[h:h + 1, :]
            lm = jnp.exp(jnp.where(mask, seg, NEG))
            ms.append((gmat[g] * lm * dt_row[h:h + 1, :]).astype(BF16))
        xp = xb[:, pr * 2 * SSD_P:(pr + 1) * 2 * SSD_P]
        rhs = jnp.concatenate([jnp.where(lo, xp, zero), jnp.where(lo, zero, xp)], axis=0)
        ys.append(jnp.dot(jnp.concatenate(ms, axis=1), rhs, preferred_element_type=F32))
    y = jnp.concatenate(ys, axis=1)
    h_t = h_ref[...]
    hb = h_t.astype(BF16)
    xw = (x * w_exp).astype(BF16)
    wpg = SSD_INNER // SSD_G
    yo = [jnp.dot(cgs[g], hb[:, g * wpg:(g + 1) * wpg], preferred_element_type=F32) for g in range(SSD_G)]
    st = [lax.dot_general(bgs[g], xw[:, g * wpg:(g + 1) * wpg], TN_DIMS, preferred_element_type=F32)
          for g in range(SSD_G)]
    h_ref[...] = h_t * dec + jnp.concatenate(st, axis=1)
    return y + jnp.concatenate(yo, axis=1) * e_exp


def _ssd_kernel(xf_ref, xb_ref, dtf_ref, dtb_ref, dttf_ref, dttb_ref, prow_ref, pcol_ref, dtot_ref, *rest,
                context):
    hf_ref, hb_ref = rest[-2:]
    if context:
        yf_ref, yb_ref, hfin_ref = rest[-5:-2]
    else:
        h0_ref, yf_ref, yb_ref = rest[-5:-2]
    c = pl.program_id(1)

    @pl.when(c == 0)
    def _():
        if context:
            hf_ref[...] = jnp.zeros_like(hf_ref)
            hb_ref[...] = jnp.zeros_like(hb_ref)
        else:
            hf_ref[...] = h0_ref[0].T
            hb_ref[...] = h0_ref[1].T

    prow = prow_ref[...]
    pcol = pcol_ref[...]
    xf = xf_ref[...]
    dt_col = _softplus(dtf_ref[:, 0:SSD_H] + prow[0:1, :])
    dt_row = _softplus(dttf_ref[0:SSD_H, :] + pcol[:, 0:1])
    yf = _ssd_chunk(xf, dt_col, dt_row, -jnp.exp(prow[2:3, :]), -jnp.exp(pcol[:, 2:3]), hf_ref, False)
    yf_ref[...] = yf + dtot_ref[...] * xf[:, :SSD_INNER]
    dt_col = _softplus(dtb_ref[:, SSD_H:2 * SSD_H] + prow[1:2, :])
    dt_row = _softplus(dttb_ref[SSD_H:2 * SSD_H, :] + pcol[:, 1:2])
    yb_ref[...] = _ssd_chunk(xb_ref[...], dt_col, dt_row, -jnp.exp(prow[3:4, :]), -jnp.exp(pcol[:, 3:4]),
                             hb_ref, True)

    if context:
        @pl.when(c == pl.num_programs(1) - 1)
        def _():
            hfin_ref[0] = hf_ref[...].T
            hfin_ref[1] = hb_ref[...].T


def _ssd(xbc, u_dt, dt_t, prow, pcol, dtot, l, nseq, seq_len, row0, states, context):
    nc = seq_len // CHUNK
    cb0 = row0 // CHUNK
    rows = nseq * seq_len
    nd = 2 * SSD_H
    fwd = lambda b, c: b * nc + c
    bwd = lambda b, c: b * nc + (nc - 1 - c)
    state_spec = pl.BlockSpec((None, None, 2, SSD_INNER, SSD_N), lambda b, c: (b, l, 0, 0, 0))
    in_specs = [
        pl.BlockSpec((CHUNK, CONV_CH), lambda b, c: (fwd(b, c), 0)),
        pl.BlockSpec((CHUNK, CONV_CH), lambda b, c: (bwd(b, c), 0)),
        pl.BlockSpec((CHUNK, nd), lambda b, c: (cb0 + fwd(b, c), 0)),
        pl.BlockSpec((CHUNK, nd), lambda b, c: (cb0 + bwd(b, c), 0)),
        pl.BlockSpec((nd, CHUNK), lambda b, c: (0, cb0 + fwd(b, c))),
        pl.BlockSpec((nd, CHUNK), lambda b, c: (0, cb0 + bwd(b, c))),
        pl.BlockSpec((4, SSD_H), lambda b, c: (0, 0)),
        pl.BlockSpec((SSD_H, 4), lambda b, c: (0, 0)),
        pl.BlockSpec((1, SSD_INNER), lambda b, c: (0, 0)),
    ]
    args = [xbc, xbc, u_dt, u_dt, dt_t, dt_t, prow, pcol, dtot]
    out_specs = [
        pl.BlockSpec((CHUNK, SSD_INNER), lambda b, c: (fwd(b, c), 0)),
        pl.BlockSpec((CHUNK, SSD_INNER), lambda b, c: (bwd(b, c), 0)),
    ]
    out_shape = [jax.ShapeDtypeStruct((rows, SSD_INNER), F32), jax.ShapeDtypeStruct((rows, SSD_INNER), F32)]
    aliases = {}
    if context:
        out_specs.append(state_spec)
        out_shape.append(jax.ShapeDtypeStruct((nseq, DEPTH, 2, SSD_INNER, SSD_N), F32))
        if states is not None:
            in_specs.append(pl.BlockSpec(memory_space=pl.ANY))
            args.append(states)
            aliases = {len(args) - 1: 2}
    else:
        in_specs.append(state_spec)
        args.append(states)
    return pl.pallas_call(
        functools.partial(_ssd_kernel, context=context),
        grid=(nseq, nc),
        in_specs=in_specs,
        out_specs=out_specs,
        out_shape=out_shape,
        scratch_shapes=[pltpu.VMEM((SSD_N, SSD_INNER), F32), pltpu.VMEM((SSD_N, SSD_INNER), F32)],
        input_output_aliases=aliases,
        compiler_params=_cp(2),
        name="ssd",
    )(*args)


def _ssd_gate_kernel(yf_ref, yb_ref, z_ref, g_ref, *rest):
    o_ref = rest[-1]
    y = (yf_ref[...] + yb_ref[...]) * _silu(z_ref[...].astype(F32))
    o_ref[...] = _rms(y, g_ref[...]).astype(BF16)


def _ssd_gate(yf, yb, u, norm_w, l, row0, ob=None):
    tm = ROW_TILE
    rows = yf.shape[0]
    rb0 = row0 // tm
    row = pl.BlockSpec((tm, SSD_INNER), lambda i: (i, 0))
    in_specs = [
        row, row,
        pl.BlockSpec((pl.Element(tm), pl.Element(SSD_INNER)), lambda i: (tm * (rb0 + i), C_Z)),
        pl.BlockSpec((None, 1, SSD_INNER), lambda i: (l, 0, 0)),
    ]
    args = [yf, yb, u, norm_w.reshape(DEPTH, 1, SSD_INNER)]
    aliases = {}
    if ob is not None:
        in_specs.append(pl.BlockSpec(memory_space=pl.ANY))
        args.append(ob)
        aliases = {4: 0}
    return pl.pallas_call(
        _ssd_gate_kernel,
        grid=(rows // tm,),
        in_specs=in_specs,
        out_specs=pl.BlockSpec((tm, SSD_INNER), lambda i: (rb0 + i, 0)),
        out_shape=jax.ShapeDtypeStruct((T, SSD_INNER), BF16),
        input_output_aliases=aliases,
        compiler_params=_cp(1),
        name="ssd_gate",
    )(*args)


def kernel(x_prompt, x_sample, cache_na_k, cache_na_v, cache_gqa_k, cache_gqa_v, state_ssd, c, c_ctx, ada_w, ada_b, norm_mix_pre, norm_mix_post, norm_ffn_pre, norm_ffn_post, w_in, gate_b, na_rpb, gqa_q_norm, gqa_k_norm, ssd_conv_w, ssd_conv_b, ssd_dt_bias, ssd_a_log, ssd_d, ssd_norm_w, w_branch, w_out, ffn_w_up, ffn_w_down):
    cvec = jnp.concatenate([c_ctx[None, :], c, jnp.zeros((8 - 1 - B_LAT, D), F32)], axis=0)
    mods = _compute_mods(cvec, ada_w, ada_b).reshape(DEPTH, 8, 6, 1, D)
    wt_in = jnp.swapaxes(w_in, 1, 2)
    w_down = ffn_w_down.astype(BF16)
    t2 = _na_bias_tiles(na_rpb)
    cos_t, sin_t = _rope_tables()
    ck_a = cache_na_k.reshape(B_LAT, DEPTH, PAST, NA_H * HD).astype(BF16)
    cv_a = cache_na_v.reshape(B_LAT, DEPTH, PAST, NA_H * HD).astype(BF16)
    ck_c = cache_gqa_k.reshape(B_LAT, DEPTH, PAST, GQA_KV * HD).astype(BF16)
    cv_c = cache_gqa_v.reshape(B_LAT, DEPTH, PAST, GQA_KV * HD).astype(BF16)
    h0_lat = state_ssd.reshape(B_LAT, DEPTH, 2, SSD_INNER, SSD_N)
    prow = jnp.concatenate([ssd_dt_bias, ssd_a_log], axis=1)
    pcol = prow.transpose(0, 2, 1)
    dtot = jnp.repeat(ssd_d[:, 0] + ssd_d[:, 1], SSD_P, axis=-1)[:, None, :]

    x, h = _normmod(x_prompt.reshape(T_CTX, D), x_sample.reshape(T_LAT, D), norm_mix_pre, mods, 0, 0, 1)
    caches = None
    ssd_fin = None
    for l in range(DEPTH):
        u = _matmul_t(h, wt_in, l, N_MAIN, 1024, 1024, BF16, "in_proj")
        u_dt, dt_t = _dt_proj(h, wt_in, l)
        oa, oc, caches = _ctx_attn(u, gqa_q_norm, gqa_k_norm, l, caches)
        oc = _lat_gqa(u, ck_c, cv_c, gqa_q_norm, gqa_k_norm, cos_t, sin_t, oc, l)
        oa = _lat_na(u, ck_a, cv_a, t2, oa, l)
        xbc_c = _conv(u, ssd_conv_w, ssd_conv_b, l, B_CTX, L_CTX, 0)
        xbc_l = _conv(u, ssd_conv_w, ssd_conv_b, l, B_LAT, L_LAT, T_CTX)
        yf_c, yb_c, ssd_fin = _ssd(xbc_c, u_dt, dt_t, prow[l], pcol[l], dtot[l], l, B_CTX, L_CTX, 0, ssd_fin, True)
        yf_l, yb_l = _ssd(xbc_l, u_dt, dt_t, prow[l], pcol[l], dtot[l], l, B_LAT, L_LAT, T_CTX, h0_lat, False)
        ob = _ssd_gate(yf_c, yb_c, u, ssd_norm_w, l, 0)
        ob = _ssd_gate(yf_l, yb_l, u, ssd_norm_w, l, T_CTX, ob)
        mix = _merge(h, oa, ob, oc, wt_in, w_branch, gate_b, l)
        x, h2 = _proj_resid(mix, w_out, x, norm_mix_post, mods, l, 2, 512, (norm_ffn_pre, l, 3, 4), "out_proj")
        act = _matmul_swiglu(h2, ffn_w_up, l)
        nxt = (norm_mix_pre, l + 1, 0, 1) if l + 1 < DEPTH else None
        x, h = _proj_resid(act, w_down, x, norm_ffn_post, mods, l, 5, 256, nxt, "ffn_down")
    na_k, na_v, gqa_k, gqa_v = caches
    return (x.reshape(B_CTX, L_CTX, D), h.reshape(B_LAT, L_LAT, D),
            na_k.reshape(B_CTX, DEPTH, L_CTX, NA_H, HD), na_v.reshape(B_CTX, DEPTH, L_CTX, NA_H, HD),
            gqa_k.reshape(B_CTX, DEPTH, L_CTX, GQA_KV, HD), gqa_v.reshape(B_CTX, DEPTH, L_CTX, GQA_KV, HD),
            ssd_fin.reshape(B_CTX, DEPTH, 2, SSD_H, SSD_P, SSD_N))
```

```python
import functools
import math

import jax
import jax.numpy as jnp
from jax import lax
from jax.experimental import pallas as pl
from jax.experimental.pallas import tpu as pltpu

F32 = jnp.float32
BF16 = jnp.bfloat16
HI = lax.Precision.HIGHEST
EPS = 1e-6
NEG = -1e30

D = 2048
DEPTH = 4
B_CTX, L_CTX = 16, 256
B_LAT, L_LAT = 4, 2048
PAST = 512
GRID_W = 64
GRID_R = L_LAT // GRID_W
NA_H = 8
HD = 128
NA_ROWS, NA_COLS = 8, 16
GQA_H, GQA_KV = 8, 2
GQA_G = GQA_H // GQA_KV
ROPE_BASE = 10000.0
SSD_INNER = 1024
SSD_P = 64
SSD_H = SSD_INNER // SSD_P
SSD_N = 128
SSD_G = 2
CHUNK = 128
CONV_CH = SSD_INNER + 2 * SSD_G * SSD_N
BRANCH_W = 1024
FFN_H = 5632
T_CTX = B_CTX * L_CTX
T_LAT = B_LAT * L_LAT
T = T_CTX + T_LAT

C_QA, C_KA, C_VA = 0, NA_H * HD, 2 * NA_H * HD
C_QC = 3 * NA_H * HD
C_KC = C_QC + GQA_H * HD
C_VC = C_KC + GQA_KV * HD
C_Z = C_VC + GQA_KV * HD
C_XBC = C_Z + SSD_INNER
C_DT = C_XBC + CONV_CH
C_GL = C_DT + 2 * SSD_H
N_MAIN = C_DT

NT_DIMS = (((1,), (1,)), ((), ()))
TN_DIMS = (((0,), (0,)), ((), ()))

ROW_TILE = 256
SUBLANES = 8
VMEM_BIG = 56


def _cp(n_axes, vmem_mb=None):
    kw = dict(dimension_semantics=("arbitrary",) * n_axes)
    if vmem_mb is not None:
        kw["vmem_limit_bytes"] = vmem_mb * 1024 * 1024
    return pltpu.CompilerParams(**kw)


def _mod_row(i, tm):
    n_ctx = T_CTX // tm
    n_lat = L_LAT // tm
    return jnp.where(i < n_ctx, 0, 1 + (i - n_ctx) // n_lat)


def _rms(x, g):
    return x * lax.rsqrt(jnp.mean(x * x, axis=-1, keepdims=True) + EPS) * g


def _silu(x):
    return x * jax.nn.sigmoid(x)


def _mods_kernel(c_ref, w_ref, b_ref, o_ref):
    s = _silu(c_ref[...]).astype(BF16)
    o_ref[...] = jnp.dot(s, w_ref[...].astype(BF16), preferred_element_type=F32) + b_ref[...]


def _compute_mods(cvec, ada_w, ada_b):
    tn = 1024
    n = 6 * D
    return pl.pallas_call(
        _mods_kernel,
        grid=(DEPTH, n // tn),
        in_specs=[
            pl.BlockSpec((8, D), lambda l, j: (0, 0)),
            pl.BlockSpec((None, D, tn), lambda l, j: (l, 0, j)),
            pl.BlockSpec((None, 1, tn), lambda l, j: (l, 0, j)),
        ],
        out_specs=pl.BlockSpec((None, 8, tn), lambda l, j: (l, 0, j)),
        out_shape=jax.ShapeDtypeStruct((DEPTH, 8, n), F32),
        compiler_params=_cp(2),
        name="mods",
    )(cvec, ada_w, ada_b.reshape(DEPTH, 1, n))


def _mod_spec(l, k, tm):
    return pl.BlockSpec((None, None, None, 1, D), lambda i: (l, _mod_row(i, tm), k, 0, 0))


def _gain_spec(l):
    return pl.BlockSpec((None, 1, D), lambda i: (l, 0, 0))


def _normmod_kernel(xc_ref, xl_ref, g_ref, sh_ref, sc_ref, x_ref, h_ref):
    x = jnp.where(pl.program_id(0) < T_CTX // ROW_TILE, xc_ref[...], xl_ref[...])
    x_ref[...] = x
    h_ref[...] = (_rms(x, g_ref[...]) * (1.0 + sc_ref[...]) + sh_ref[...]).astype(BF16)


def _normmod(x_ctx, x_lat, gains, mods, l, k_sh, k_sc):
    tm = ROW_TILE
    n_ctx = T_CTX // tm
    row = pl.BlockSpec((tm, D), lambda i: (i, 0))
    return pl.pallas_call(
        _normmod_kernel,
        grid=(T // tm,),
        in_specs=[
            pl.BlockSpec((tm, D), lambda i: (jnp.minimum(i, n_ctx - 1), 0)),
            pl.BlockSpec((tm, D), lambda i: (jnp.maximum(i - n_ctx, 0), 0)),
            _gain_spec(l),
            _mod_spec(l, k_sh, tm),
            _mod_spec(l, k_sc, tm),
        ],
        out_specs=[row, row],
        out_shape=[jax.ShapeDtypeStruct((T, D), F32), jax.ShapeDtypeStruct((T, D), BF16)],
        compiler_params=_cp(1),
        name="normmod",
    )(x_ctx, x_lat, gains.reshape(DEPTH, 1, D), mods, mods)


DT_PAD = 128


def _in_proj_kernel(x_ref, wt_ref, wdt_ref, o_ref, odt_ref, odtt_ref, wb_ref):
    @pl.when(pl.program_id(1) == 0)
    def _():
        wb_ref[...] = wt_ref[...].astype(BF16).T

    x = x_ref[...]
    o_ref[...] = jnp.dot(x, wb_ref[...], preferred_element_type=F32).astype(o_ref.dtype)

    @pl.when(pl.program_id(0) == pl.num_programs(0) - 1)
    def _():
        dt = lax.dot_general(x, wdt_ref[...].astype(BF16), NT_DIMS, preferred_element_type=F32)
        odt_ref[...] = dt
        odtt_ref[...] = dt.T


def _in_proj(x, wt, l, tn=1024, tm=1024):
    m, k = x.shape
    nj = N_MAIN // tn
    last = lambda j, i: jnp.where(j == nj - 1, i, 0)
    return pl.pallas_call(
        _in_proj_kernel,
        grid=(nj, m // tm),
        in_specs=[
            pl.BlockSpec((tm, k), lambda j, i: (i, 0)),
            pl.BlockSpec((None, tn, k), lambda j, i: (l, j, 0)),
            pl.BlockSpec((None, DT_PAD, k), lambda j, i: (l, C_DT // DT_PAD, 0)),
        ],
        out_specs=[
            pl.BlockSpec((tm, tn), lambda j, i: (i, j)),
            pl.BlockSpec((tm, DT_PAD), lambda j, i: (last(j, i), 0)),
            pl.BlockSpec((DT_PAD, tm), lambda j, i: (0, last(j, i))),
        ],
        out_shape=[
            jax.ShapeDtypeStruct((m, N_MAIN), BF16),
            jax.ShapeDtypeStruct((m, DT_PAD), F32),
            jax.ShapeDtypeStruct((DT_PAD, m), F32),
        ],
        scratch_shapes=[pltpu.VMEM((k, tn), BF16)],
        compiler_params=_cp(2, 40),
        name="in_proj",
    )(x, wt, wt)


def _mm_swiglu_kernel(x_ref, wg_ref, wu_ref, o_ref, wgb_ref, wub_ref):
    @pl.when(pl.program_id(1) == 0)
    def _():
        wgb_ref[...] = wg_ref[...].astype(BF16)
        wub_ref[...] = wu_ref[...].astype(BF16)

    x = x_ref[...]
    g = jnp.dot(x, wgb_ref[...], preferred_element_type=F32)
    u = jnp.dot(x, wub_ref[...], preferred_element_type=F32)
    o_ref[...] = (_silu(g) * u).astype(o_ref.dtype)


def _matmul_swiglu(x, w, l, tn=512, tm=1024):
    m, k = x.shape
    nb = FFN_H // tn
    return pl.pallas_call(
        _mm_swiglu_kernel,
        grid=(nb, m // tm),
        in_specs=[
            pl.BlockSpec((tm, k), lambda j, i: (i, 0)),
            pl.BlockSpec((None, k, tn), lambda j, i: (l, 0, j)),
            pl.BlockSpec((None, k, tn), lambda j, i: (l, 0, nb + j)),
        ],
        out_specs=pl.BlockSpec((tm, tn), lambda j, i: (i, j)),
        out_shape=jax.ShapeDtypeStruct((m, FFN_H), BF16),
        scratch_shapes=[pltpu.VMEM((k, tn), BF16), pltpu.VMEM((k, tn), BF16)],
        compiler_params=_cp(2, 40),
        name="ffn_up",
    )(x, w, w)


def _merge_kernel(h_ref, oa_ref, ob_ref, oc_ref, wg0, wg1, wg2, wb0, wb1, wb2, gb0, gb1, gb2, o_ref, wgb_ref, wbb_ref):
    wg = (wg0, wg1, wg2)
    wb = (wb0, wb1, wb2)
    gb = (gb0, gb1, gb2)

    @pl.when(pl.program_id(1) == 0)
    def _():
        for g in range(3):
            wgb_ref[g] = wg[g][0].astype(BF16).T
            wbb_ref[g] = wb[g][...].astype(BF16)

    h = h_ref[...]
    acc = None
    for g, o_r in enumerate((oa_ref, ob_ref, oc_ref)):
        gl = jnp.dot(h, wgb_ref[g], preferred_element_type=F32) + gb[g][...]
        p = jnp.dot(o_r[...], wbb_ref[g], preferred_element_type=F32)
        t = jax.nn.sigmoid(gl) * p
        acc = t if acc is None else acc + t
    o_ref[...] = acc.astype(BF16)


def _merge(h, oa, ob, oc, wt_in, w_branch, gate_b, l, tn=512, tm=512):
    nb = D // tn
    xrow = pl.BlockSpec((tm, D), lambda j, i: (i, 0))
    orow = pl.BlockSpec((tm, BRANCH_W), lambda j, i: (i, 0))
    once = pl.Buffered(1)
    assert C_GL % SUBLANES == 0 and tn % SUBLANES == 0

    def wg_spec(g):
        row0 = (C_GL + g * D) // SUBLANES
        return pl.BlockSpec((pl.Element(1), pl.Element(tn), pl.Element(D)),
                            lambda j, i: (l, SUBLANES * (row0 + j * (tn // SUBLANES)), 0), pipeline_mode=once)

    def wb_spec(g):
        return pl.BlockSpec((None, None, BRANCH_W, tn), lambda j, i: (l, g, 0, j), pipeline_mode=once)

    def gb_spec(g):
        return pl.BlockSpec((None, 1, tn), lambda j, i: (l, 0, g * nb + j))

    gate_b3 = gate_b.reshape(DEPTH, 1, 3 * D)
    return pl.pallas_call(
        _merge_kernel,
        grid=(nb, T // tm),
        in_specs=[xrow, orow, orow, orow]
        + [wg_spec(g) for g in range(3)]
        + [wb_spec(g) for g in range(3)]
        + [gb_spec(g) for g in range(3)],
        out_specs=pl.BlockSpec((tm, tn), lambda j, i: (i, j)),
        out_shape=jax.ShapeDtypeStruct((T, D), BF16),
        scratch_shapes=[pltpu.VMEM((3, D, tn), BF16), pltpu.VMEM((3, BRANCH_W, tn), BF16)],
        compiler_params=_cp(2, 48),
        name="merge",
    )(h, oa, ob, oc, wt_in, wt_in, wt_in, w_branch, w_branch, w_branch, gate_b3, gate_b3, gate_b3)


def _proj_resid_kernel(a_ref, w_ref, x_ref, gpost_ref, gate_ref, *rest, with_h, cast_w):
    if cast_w:
        wb_ref = rest[-1]
        rest = rest[:-1]

        @pl.when(pl.program_id(0) == 0)
        def _():
            wb_ref[...] = w_ref[...].astype(BF16)

        w = wb_ref[...]
    else:
        w = w_ref[...]
    y = jnp.dot(a_ref[...], w, preferred_element_type=F32)
    xn = x_ref[...] + gate_ref[...] * _rms(y, gpost_ref[...])
    if with_h:
        gpre_ref, sh_ref, sc_ref, xo_ref, h_ref = rest
        xo_ref[...] = xn
        h_ref[...] = (_rms(xn, gpre_ref[...]) * (1.0 + sc_ref[...]) + sh_ref[...]).astype(BF16)
    else:
        oc_ref, ol_ref = rest
        n_ctx = T_CTX // xn.shape[0]

        @pl.when(pl.program_id(0) < n_ctx)
        def _():
            oc_ref[...] = xn

        @pl.when(pl.program_id(0) >= n_ctx)
        def _():
            ol_ref[...] = xn


def _proj_resid(a, w, x, post_gains, mods, l, k_gate, tm, nxt, name):
    k = a.shape[1]
    cast_w = w.dtype != BF16
    row = pl.BlockSpec((tm, D), lambda i: (i, 0))
    in_specs = [
        pl.BlockSpec((tm, k), lambda i: (i, 0)),
        pl.BlockSpec((None, k, D), lambda i: (l, 0, 0), pipeline_mode=pl.Buffered(1)),
        row, _gain_spec(l), _mod_spec(l, k_gate, tm),
    ]
    args = [a, w, x, post_gains.reshape(DEPTH, 1, D), mods]
    out_specs = [row]
    out_shape = [jax.ShapeDtypeStruct((T, D), F32)]
    if nxt is not None:
        gains, ln, k_sh, k_sc = nxt
        in_specs += [_gain_spec(ln), _mod_spec(ln, k_sh, tm), _mod_spec(ln, k_sc, tm)]
        args += [gains.reshape(DEPTH, 1, D), mods, mods]
        out_specs.append(row)
        out_shape.append(jax.ShapeDtypeStruct((T, D), BF16))
    else:
        n_ctx = T_CTX // tm
        out_specs = [pl.BlockSpec((tm, D), lambda i: (jnp.minimum(i, n_ctx - 1), 0)),
                     pl.BlockSpec((tm, D), lambda i: (jnp.maximum(i - n_ctx, 0), 0))]
        out_shape = [jax.ShapeDtypeStruct((T_CTX, D), F32), jax.ShapeDtypeStruct((T_LAT, D), F32)]
    out = pl.pallas_call(
        functools.partial(_proj_resid_kernel, with_h=nxt is not None, cast_w=cast_w),
        grid=(T // tm,),
        in_specs=in_specs,
        out_specs=out_specs,
        out_shape=out_shape,
        scratch_shapes=[pltpu.VMEM((k, D), BF16)] if cast_w else [],
        compiler_params=_cp(1, VMEM_BIG),
        name=name,
    )(*args)
    return out


ATT_SCALE = HD ** -0.5
ATT_SCALE_LOG2 = ATT_SCALE * math.log2(math.e)


def _softmax_pv(s, v):
    m = jnp.max(s, axis=-1, keepdims=True)
    e = jnp.exp(s - m)
    p = e * (1.0 / jnp.sum(e, axis=-1, keepdims=True))
    return jnp.dot(p.astype(BF16), v, preferred_element_type=F32)


def _softmax_pv2(s1, v1, s2, v2):
    m = jnp.maximum(jnp.max(s1, axis=-1, keepdims=True), jnp.max(s2, axis=-1, keepdims=True)) * ATT_SCALE_LOG2
    e1 = jnp.exp2(s1 * ATT_SCALE_LOG2 - m)
    e2 = jnp.exp2(s2 * ATT_SCALE_LOG2 - m)
    tot = jnp.sum(e1, axis=-1, keepdims=True) + jnp.sum(e2, axis=-1, keepdims=True)
    o = jnp.dot(e1.astype(BF16), v1, preferred_element_type=F32) + jnp.dot(e2.astype(BF16), v2,
                                                                           preferred_element_type=F32)
    return o * (1.0 / tot)


def _ctx_attn_kernel(a_ref, c_ref, qn_ref, kn_ref, *rest):
    oa_ref, oc_ref, nk_ref, nv_ref, kcn_ref, vcn_ref = rest[-6:]
    nk_ref[...] = a_ref[:, C_KA:C_KA + NA_H * HD].astype(F32)
    nv_ref[...] = a_ref[:, C_VA:C_VA + NA_H * HD].astype(F32)
    for h in range(NA_H):
        q = a_ref[:, C_QA + h * HD:C_QA + (h + 1) * HD]
        k = a_ref[:, C_KA + h * HD:C_KA + (h + 1) * HD]
        v = a_ref[:, C_VA + h * HD:C_VA + (h + 1) * HD]
        s = lax.dot_general(q, k, NT_DIMS, preferred_element_type=F32) * ATT_SCALE
        oa_ref[:, h * HD:(h + 1) * HD] = _softmax_pv(s, v).astype(BF16)
    kc0 = C_KC - C_QC
    vc0 = C_VC - C_QC
    vcn_ref[...] = c_ref[:, vc0:vc0 + GQA_KV * HD].astype(F32)
    for kv in range(GQA_KV):
        k = _rms(c_ref[:, kc0 + kv * HD:kc0 + (kv + 1) * HD].astype(F32), kn_ref[...])
        kcn_ref[:, kv * HD:(kv + 1) * HD] = k
        kb = k.astype(BF16)
        v = c_ref[:, vc0 + kv * HD:vc0 + (kv + 1) * HD]
        for g in range(GQA_G):
            h = kv * GQA_G + g
            q = _rms(c_ref[:, h * HD:(h + 1) * HD].astype(F32), qn_ref[...]).astype(BF16)
            s = lax.dot_general(q, kb, NT_DIMS, preferred_element_type=F32) * ATT_SCALE
            oc_ref[:, h * HD:(h + 1) * HD] = _softmax_pv(s, v).astype(BF16)


def _ctx_attn(u, qn, kn, l, caches):
    wa = 3 * NA_H * HD
    wc = C_Z - C_QC
    rows = lambda w: pl.BlockSpec((L_CTX, w), lambda b: (b, 0))
    lay = lambda w: pl.BlockSpec((None, None, L_CTX, w), lambda b: (b, l, 0, 0))
    cache_shape = lambda w: jax.ShapeDtypeStruct((B_CTX, DEPTH, L_CTX, w), F32)
    in_specs = [
        pl.BlockSpec((L_CTX, wa), lambda b: (b, 0)),
        pl.BlockSpec((L_CTX, wc), lambda b: (b, C_QC // wc)),
        pl.BlockSpec((None, 1, HD), lambda b: (l, 0, 0)),
        pl.BlockSpec((None, 1, HD), lambda b: (l, 0, 0)),
    ]
    args = [u, u, qn.reshape(DEPTH, 1, HD), kn.reshape(DEPTH, 1, HD)]
    aliases = {}
    if caches is not None:
        in_specs += [pl.BlockSpec(memory_space=pl.ANY)] * 4
        args += list(caches)
        aliases = {4 + n: 2 + n for n in range(4)}
    out = pl.pallas_call(
        _ctx_attn_kernel,
        grid=(B_CTX,),
        in_specs=in_specs,
        out_specs=[rows(BRANCH_W), rows(BRANCH_W), lay(NA_H * HD), lay(NA_H * HD), lay(GQA_KV * HD),
                   lay(GQA_KV * HD)],
        out_shape=[
            jax.ShapeDtypeStruct((T, BRANCH_W), BF16),
            jax.ShapeDtypeStruct((T, BRANCH_W), BF16),
            cache_shape(NA_H * HD), cache_shape(NA_H * HD), cache_shape(GQA_KV * HD), cache_shape(GQA_KV * HD),
        ],
        input_output_aliases=aliases,
        compiler_params=_cp(1),
        name="ctx_attn",
    )(*args)
    return out[0], out[1], tuple(out[2:])


def _rope(x, cos, sin):
    lane = lax.broadcasted_iota(jnp.int32, x.shape, 1)
    first = (lane & (HD // 2 - 1)) < (HD // 4)
    rot = jnp.where(first, -pltpu.roll(x, HD - HD // 4, 1), pltpu.roll(x, HD // 4, 1))
    return x * cos + rot * sin


def _lat_gqa_kernel(q_ref, k_ref, vl_ref, ck_ref, cv_ref, qn_ref, kn_ref, cos_ref, sin_ref, oc_in_ref, o_ref, kr_ref):
    del oc_in_ref
    i = pl.program_id(2)
    tq = q_ref.shape[0]

    @pl.when(i == 0)
    def _():
        k = _rms(k_ref[...].astype(F32), kn_ref[...])
        kr_ref[...] = _rope(k, cos_ref[...], sin_ref[...]).astype(BF16)

    start = pl.multiple_of(i * tq, tq)
    cos = cos_ref[pl.ds(start, tq), :]
    sin = sin_ref[pl.ds(start, tq), :]
    kl = kr_ref[...]
    vl = vl_ref[...]
    ck = ck_ref[...]
    cv = cv_ref[...]
    for g in range(GQA_G):
        q = _rope(_rms(q_ref[:, g * HD:(g + 1) * HD].astype(F32), qn_ref[...]), cos, sin).astype(BF16)
        s1 = lax.dot_general(q, kl, NT_DIMS, preferred_element_type=F32)
        s2 = lax.dot_general(q, ck, NT_DIMS, preferred_element_type=F32)
        o_ref[:, g * HD:(g + 1) * HD] = _softmax_pv2(s1, vl, s2, cv).astype(BF16)


def _lat_gqa(u, ck, cv, qn, kn, cos_t, sin_t, oc, l, tq=512):
    nq = L_LAT // tq
    r0 = T_CTX // tq
    s0 = T_CTX // L_LAT
    wq = GQA_G * HD
    full = lambda b, kv, i: (0, 0)
    return pl.pallas_call(
        _lat_gqa_kernel,
        grid=(B_LAT, GQA_KV, nq),
        in_specs=[
            pl.BlockSpec((tq, wq), lambda b, kv, i: (r0 + b * nq + i, C_QC // wq + kv)),
            pl.BlockSpec((L_LAT, HD), lambda b, kv, i: (s0 + b, C_KC // HD + kv)),
            pl.BlockSpec((L_LAT, HD), lambda b, kv, i: (s0 + b, C_VC // HD + kv)),
            pl.BlockSpec((None, None, PAST, HD), lambda b, kv, i: (b, l, 0, kv)),
            pl.BlockSpec((None, None, PAST, HD), lambda b, kv, i: (b, l, 0, kv)),
            pl.BlockSpec((None, 1, HD), lambda b, kv, i: (l, 0, 0)),
            pl.BlockSpec((None, 1, HD), lambda b, kv, i: (l, 0, 0)),
            pl.BlockSpec((L_LAT, HD), full),
            pl.BlockSpec((L_LAT, HD), full),
            pl.BlockSpec(memory_space=pl.ANY),
        ],
        out_specs=pl.BlockSpec((tq, wq), lambda b, kv, i: (r0 + b * nq + i, kv)),
        out_shape=jax.ShapeDtypeStruct((T, GQA_H * HD), BF16),
        scratch_shapes=[pltpu.VMEM((L_LAT, HD), BF16)],
        input_output_aliases={9: 0},
        compiler_params=_cp(3),
        name="lat_gqa",
    )(u, u, u, ck, cv, qn.reshape(DEPTH, 1, HD), kn.reshape(DEPTH, 1, HD), cos_t, sin_t, oc)


NA_QROWS = 4
NA_UROWS = 12
NA_TQ = NA_QROWS * GRID_W
NA_TK = NA_UROWS * GRID_W
NA_HP = 4


def _lat_na_kernel(q_ref, k_ref, v_ref, ck_ref, cv_ref, t2_ref, oa_in_ref, o_ref, bias_ref):
    del oa_in_ref
    j = pl.program_id(2)
    ustart = jnp.clip(NA_QROWS * j - NA_ROWS // 2, 0, GRID_R - NA_UROWS)
    left = lax.broadcasted_iota(jnp.int32, (GRID_W, 2 * GRID_W), 1) < GRID_W
    for a in range(NA_QROWS):
        r = NA_QROWS * j + a
        rs = jnp.clip(r - NA_ROWS // 2, 0, GRID_R - NA_ROWS)
        for m in range(NA_UROWS // 2):
            kr0 = ustart + 2 * m
            v0 = ((kr0 >= rs) & (kr0 < rs + NA_ROWS)).astype(jnp.int32)
            v1 = ((kr0 + 1 >= rs) & (kr0 + 1 < rs + NA_ROWS)).astype(jnp.int32)
            idx = jnp.clip(kr0 - r + NA_ROWS, 0, 2 * NA_ROWS - 1)
            ok = jnp.where(left, v0, v1) > 0
            for hh in range(NA_HP):
                bias_ref[hh, a * GRID_W:(a + 1) * GRID_W, m * 2 * GRID_W:(m + 1) * 2 * GRID_W] = jnp.where(
                    ok, t2_ref[hh, idx], NEG)
    start = pl.multiple_of(ustart * GRID_W, GRID_W)
    for hh in range(NA_HP):
        cs = slice(hh * HD, (hh + 1) * HD)
        q = q_ref[:, cs]
        kl = k_ref[pl.ds(start, NA_TK), cs]
        vl = v_ref[pl.ds(start, NA_TK), cs]
        s_loc = lax.dot_general(q, kl, NT_DIMS, preferred_element_type=F32) + bias_ref[hh]
        s_ctx = lax.dot_general(q, ck_ref[:, cs], NT_DIMS, preferred_element_type=F32)
        o_ref[:, cs] = _softmax_pv2(s_loc, vl, s_ctx, cv_ref[:, cs]).astype(BF16)


def _lat_na(u, ck, cv, t2, oa, l):
    nj = GRID_R // NA_QROWS
    r0 = T_CTX // NA_TQ
    s0 = T_CTX // L_LAT
    w = NA_HP * HD
    return pl.pallas_call(
        _lat_na_kernel,
        grid=(B_LAT, NA_H // NA_HP, nj),
        in_specs=[
            pl.BlockSpec((NA_TQ, w), lambda b, h, j: (r0 + b * nj + j, C_QA // w + h)),
            pl.BlockSpec((L_LAT, w), lambda b, h, j: (s0 + b, C_KA // w + h)),
            pl.BlockSpec((L_LAT, w), lambda b, h, j: (s0 + b, C_VA // w + h)),
            pl.BlockSpec((None, None, PAST, w), lambda b, h, j: (b, l, 0, h)),
            pl.BlockSpec((None, None, PAST, w), lambda b, h, j: (b, l, 0, h)),
            pl.BlockSpec((None, NA_HP, 2 * NA_ROWS, GRID_W, 2 * GRID_W), lambda b, h, j: (l, h, 0, 0, 0)),
            pl.BlockSpec(memory_space=pl.ANY),
        ],
        out_specs=pl.BlockSpec((NA_TQ, w), lambda b, h, j: (r0 + b * nj + j, h)),
        out_shape=jax.ShapeDtypeStruct((T, NA_H * HD), BF16),
        scratch_shapes=[pltpu.VMEM((NA_HP, NA_TQ, NA_TK), F32)],
        input_output_aliases={6: 0},
        compiler_params=_cp(3),
        name="lat_na",
    )(u, u, u, ck, cv, t2, oa)


def _na_bias_tiles(rpb):
    w = jnp.arange(GRID_W)
    cs = jnp.clip(w - NA_COLS // 2, 0, GRID_W - NA_COLS)
    cmask = (w[None, :] >= cs[:, None]) & (w[None, :] < cs[:, None] + NA_COLS)
    dc = jnp.clip(w[None, :] - w[:, None] + NA_COLS - 1, 0, 2 * NA_COLS - 2)
    tm = jnp.where(cmask, rpb[:, :, :, dc] / ATT_SCALE, NEG)
    pad = jnp.full(tm.shape[:2] + (1,) + tm.shape[3:], NEG, F32)
    lo = jnp.concatenate([pad, tm], axis=2)
    hi = jnp.concatenate([tm, pad], axis=2)
    return jnp.concatenate([lo, hi], axis=-1).astype(F32)


def _rope_tables():
    nf = HD // 4
    t = jnp.arange(L_LAT)
    inv = ROPE_BASE ** (-jnp.arange(nf, dtype=F32) / nf)
    ar = (t // GRID_W).astype(F32)[:, None] * inv[None, :]
    ac = (t % GRID_W).astype(F32)[:, None] * inv[None, :]
    cos_t = jnp.concatenate([jnp.cos(ar), jnp.cos(ar), jnp.cos(ac), jnp.cos(ac)], axis=-1)
    sin_t = jnp.concatenate([jnp.sin(ar), jnp.sin(ar), jnp.sin(ac), jnp.sin(ac)], axis=-1)
    return cos_t, sin_t


def _conv_kernel(x_ref, w_ref, b_ref, o_ref):
    x = x_ref[...].astype(F32)
    n = x.shape[0]
    row = lax.broadcasted_iota(jnp.int32, (SUBLANES, x.shape[1]), 0)
    xm = pltpu.roll(x, 1, 0)
    xm = jnp.concatenate([jnp.where(row == 0, 0.0, xm[:SUBLANES]), xm[SUBLANES:]], axis=0)
    xp = pltpu.roll(x, n - 1, 0)
    xp = jnp.concatenate([xp[:n - SUBLANES], jnp.where(row == SUBLANES - 1, 0.0, xp[n - SUBLANES:])], axis=0)
    y = w_ref[0:1, :] * xm + w_ref[1:2, :] * x + w_ref[2:3, :] * xp + b_ref[...]
    o_ref[...] = _silu(y)


def _conv(u, conv_w, conv_b, l, nseq, seq_len, row0):
    tc = 512
    nc = CONV_CH // tc
    rb0 = row0 // seq_len
    cb0 = C_XBC // tc
    return pl.pallas_call(
        _conv_kernel,
        grid=(nseq, nc),
        in_specs=[
            pl.BlockSpec((seq_len, tc), lambda b, j: (rb0 + b, cb0 + j)),
            pl.BlockSpec((None, 3, tc), lambda b, j: (l, 0, j)),
            pl.BlockSpec((None, 1, tc), lambda b, j: (l, 0, j)),
        ],
        out_specs=pl.BlockSpec((seq_len, tc), lambda b, j: (b, j)),
        out_shape=jax.ShapeDtypeStruct((nseq * seq_len, CONV_CH), F32),
        compiler_params=_cp(2),
        name="conv",
    )(u, conv_w, conv_b.reshape(DEPTH, 1, CONV_CH))


def _softplus(x):
    return jnp.maximum(x, 0.0) + jnp.log1p(jnp.exp(-jnp.abs(x)))


def _expand_heads(v, expand, terms):
    out = None
    rem = v
    for _ in range(terms):
        piece = rem.astype(BF16)
        rem = rem - piece.astype(F32)
        t = jnp.dot(piece, expand, preferred_element_type=F32)
        out = t if out is None else out + t
    return out


def _ssd_chunk(xbc, dt_col, dt_row, a_row, a_col, h_ref, reverse):
    x = xbc[:, :SSD_INNER]
    bm = xbc[:, SSD_INNER:SSD_INNER + SSD_G * SSD_N]
    cm = xbc[:, SSD_INNER + SSD_G * SSD_N:]
    li = lax.broadcasted_iota(jnp.int32, (CHUNK, CHUNK), 0)
    si = lax.broadcasted_iota(jnp.int32, (CHUNK, CHUNK), 1)
    mask = (si >= li) if reverse else (si <= li)
    tri = mask.astype(F32)
    tri_t = ((li >= si) if reverse else (li <= si)).astype(F32)
    cum_col = jnp.dot(tri, dt_col * a_row, precision=HI, preferred_element_type=F32)
    cum_row = jnp.dot(dt_row * a_col, tri_t, precision=HI, preferred_element_type=F32)
    tot_row = cum_col[0:1, :] if reverse else cum_col[CHUNK - 1:CHUNK, :]
    col = lax.broadcasted_iota(jnp.int32, (SSD_H, SSD_INNER), 1)
    lo_col = lax.broadcasted_iota(jnp.int32, (SSD_H, SSD_INNER), 0) * SSD_P
    expand = ((col >= lo_col) & (col < lo_col + SSD_P)).astype(BF16)
    w_exp = _expand_heads(jnp.exp(tot_row - cum_col) * dt_col, expand, 1)
    e_exp = _expand_heads(jnp.exp(cum_col), expand, 1)
    dec = _expand_heads(jnp.broadcast_to(jnp.exp(tot_row), (8, SSD_H)), expand, 3)[0:1, :]
    hpg = SSD_H // SSD_G
    xb = x.astype(BF16)
    bgs = [bm[:, g * SSD_N:(g + 1) * SSD_N].astype(BF16) for g in range(SSD_G)]
    cgs = [cm[:, g * SSD_N:(g + 1) * SSD_N].astype(BF16) for g in range(SSD_G)]
    gmat = [lax.dot_general(cgs[g], bgs[g], NT_DIMS, preferred_element_type=F32) for g in range(SSD_G)]
    lo = lax.broadcasted_iota(jnp.int32, (CHUNK, 2 * SSD_P), 1) < SSD_P
    zero = jnp.zeros((CHUNK, 2 * SSD_P), BF16)
    ys = []
    for pr in range(SSD_H // 2):
        g = (2 * pr) // hpg
        ms = []
        for h in (2 * pr, 2 * pr + 1):
            seg = cum_col[:, h:h + 1] - cum_row[h:h + 1, :]
            lm = jnp.exp(jnp.where(mask, seg, NEG))
            ms.append((gmat[g] * lm * dt_row[h:h + 1, :]).astype(BF16))
        xp = xb[:, pr * 2 * SSD_P:(pr + 1) * 2 * SSD_P]
        rhs = jnp.concatenate([jnp.where(lo, xp, zero), jnp.where(lo, zero, xp)], axis=0)
        ys.append(jnp.dot(jnp.concatenate(ms, axis=1), rhs, preferred_element_type=F32))
    y = jnp.concatenate(ys, axis=1)
    h_t = h_ref[...]
    hb = h_t.astype(BF16)
    xw = (x * w_exp).astype(BF16)
    wpg = SSD_INNER // SSD_G
    yo = [jnp.dot(cgs[g], hb[:, g * wpg:(g + 1) * wpg], preferred_element_type=F32) for g in range(SSD_G)]
    st = [lax.dot_general(bgs[g], xw[:, g * wpg:(g + 1) * wpg], TN_DIMS, preferred_element_type=F32)
          for g in range(SSD_G)]
    h_ref[...] = h_t * dec + jnp.concatenate(st, axis=1)
    return y + jnp.concatenate(yo, axis=1) * e_exp


def _ssd_kernel(xf_ref, xb_ref, dtf_ref, dtb_ref, dttf_ref, dttb_ref, prow_ref, pcol_ref, dtot_ref, *rest,
                context):
    hf_ref, hb_ref = rest[-2:]
    if context:
        yf_ref, yb_ref, hfin_ref = rest[-5:-2]
    else:
        h0_ref, yf_ref, yb_ref = rest[-5:-2]
    c = pl.program_id(1)

    @pl.when(c == 0)
    def _():
        if context:
            hf_ref[...] = jnp.zeros_like(hf_ref)
            hb_ref[...] = jnp.zeros_like(hb_ref)
        else:
            hf_ref[...] = h0_ref[0].T
            hb_ref[...] = h0_ref[1].T

    prow = prow_ref[...]
    pcol = pcol_ref[...]
    xf = xf_ref[...]
    dt_col = _softplus(dtf_ref[:, 0:SSD_H] + prow[0:1, :])
    dt_row = _softplus(dttf_ref[0:SSD_H, :] + pcol[:, 0:1])
    yf = _ssd_chunk(xf, dt_col, dt_row, -jnp.exp(prow[2:3, :]), -jnp.exp(pcol[:, 2:3]), hf_ref, False)
    yf_ref[...] = yf + dtot_ref[...] * xf[:, :SSD_INNER]
    dt_col = _softplus(dtb_ref[:, SSD_H:2 * SSD_H] + prow[1:2, :])
    dt_row = _softplus(dttb_ref[SSD_H:2 * SSD_H, :] + pcol[:, 1:2])
    yb_ref[...] = _ssd_chunk(xb_ref[...], dt_col, dt_row, -jnp.exp(prow[3:4, :]), -jnp.exp(pcol[:, 3:4]),
                             hb_ref, True)

    if context:
        @pl.when(c == pl.num_programs(1) - 1)
        def _():
            hfin_ref[0] = hf_ref[...].T
            hfin_ref[1] = hb_ref[...].T


def _ssd(xbc, u_dt, dt_t, prow, pcol, dtot, l, nseq, seq_len, row0, states, context):
    nc = seq_len // CHUNK
    cb0 = row0 // CHUNK
    rows = nseq * seq_len
    nd = 2 * SSD_H
    fwd = lambda b, c: b * nc + c
    bwd = lambda b, c: b * nc + (nc - 1 - c)
    state_spec = pl.BlockSpec((None, None, 2, SSD_INNER, SSD_N), lambda b, c: (b, l, 0, 0, 0))
    in_specs = [
        pl.BlockSpec((CHUNK, CONV_CH), lambda b, c: (fwd(b, c), 0)),
        pl.BlockSpec((CHUNK, CONV_CH), lambda b, c: (bwd(b, c), 0)),
        pl.BlockSpec((CHUNK, DT_PAD), lambda b, c: (cb0 + fwd(b, c), 0)),
        pl.BlockSpec((CHUNK, DT_PAD), lambda b, c: (cb0 + bwd(b, c), 0)),
        pl.BlockSpec((DT_PAD, CHUNK), lambda b, c: (0, cb0 + fwd(b, c))),
        pl.BlockSpec((DT_PAD, CHUNK), lambda b, c: (0, cb0 + bwd(b, c))),
        pl.BlockSpec((4, SSD_H), lambda b, c: (0, 0)),
        pl.BlockSpec((SSD_H, 4), lambda b, c: (0, 0)),
        pl.BlockSpec((1, SSD_INNER), lambda b, c: (0, 0)),
    ]
    args = [xbc, xbc, u_dt, u_dt, dt_t, dt_t, prow, pcol, dtot]
    out_specs = [
        pl.BlockSpec((CHUNK, SSD_INNER), lambda b, c: (fwd(b, c), 0)),
        pl.BlockSpec((CHUNK, SSD_INNER), lambda b, c: (bwd(b, c), 0)),
    ]
    out_shape = [jax.ShapeDtypeStruct((rows, SSD_INNER), F32), jax.ShapeDtypeStruct((rows, SSD_INNER), F32)]
    aliases = {}
    if context:
        out_specs.append(state_spec)
        out_shape.append(jax.ShapeDtypeStruct((nseq, DEPTH, 2, SSD_INNER, SSD_N), F32))
        if states is not None:
            in_specs.append(pl.BlockSpec(memory_space=pl.ANY))
            args.append(states)
            aliases = {len(args) - 1: 2}
    else:
        in_specs.append(state_spec)
        args.append(states)
    return pl.pallas_call(
        functools.partial(_ssd_kernel, context=context),
        grid=(nseq, nc),
        in_specs=in_specs,
        out_specs=out_specs,
        out_shape=out_shape,
        scratch_shapes=[pltpu.VMEM((SSD_N, SSD_INNER), F32), pltpu.VMEM((SSD_N, SSD_INNER), F32)],
        input_output_aliases=aliases,
        compiler_params=_cp(2),
        name="ssd",
    )(*args)


def _ssd_gate_kernel(yf_ref, yb_ref, z_ref, g_ref, *rest):
    o_ref = rest[-1]
    y = (yf_ref[...] + yb_ref[...]) * _silu(z_ref[...].astype(F32))
    o_ref[...] = _rms(y, g_ref[...]).astype(BF16)


def _ssd_gate(yf, yb, u, norm_w, l, row0, ob=None):
    tm = ROW_TILE
    rows = yf.shape[0]
    rb0 = row0 // tm
    row = pl.BlockSpec((tm, SSD_INNER), lambda i: (i, 0))
    in_specs = [
        row, row,
        pl.BlockSpec((pl.Element(tm), pl.Element(SSD_INNER)), lambda i: (tm * (rb0 + i), C_Z)),
        pl.BlockSpec((None, 1, SSD_INNER), lambda i: (l, 0, 0)),
    ]
    args = [yf, yb, u, norm_w.reshape(DEPTH, 1, SSD_INNER)]
    aliases = {}
    if ob is not None:
        in_specs.append(pl.BlockSpec(memory_space=pl.ANY))
        args.append(ob)
        aliases = {4: 0}
    return pl.pallas_call(
        _ssd_gate_kernel,
        grid=(rows // tm,),
        in_specs=in_specs,
        out_specs=pl.BlockSpec((tm, SSD_INNER), lambda i: (rb0 + i, 0)),
        out_shape=jax.ShapeDtypeStruct((T, SSD_INNER), BF16),
        input_output_aliases=aliases,
        compiler_params=_cp(1),
        name="ssd_gate",
    )(*args)


def kernel(x_prompt, x_sample, cache_na_k, cache_na_v, cache_gqa_k, cache_gqa_v, state_ssd, c, c_ctx, ada_w, ada_b, norm_mix_pre, norm_mix_post, norm_ffn_pre, norm_ffn_post, w_in, gate_b, na_rpb, gqa_q_norm, gqa_k_norm, ssd_conv_w, ssd_conv_b, ssd_dt_bias, ssd_a_log, ssd_d, ssd_norm_w, w_branch, w_out, ffn_w_up, ffn_w_down):
    cvec = jnp.concatenate([c_ctx[None, :], c, jnp.zeros((8 - 1 - B_LAT, D), F32)], axis=0)
    mods = _compute_mods(cvec, ada_w, ada_b).reshape(DEPTH, 8, 6, 1, D)
    wt_in = jnp.swapaxes(w_in, 1, 2)
    w_down = ffn_w_down.astype(BF16)
    t2 = _na_bias_tiles(na_rpb)
    cos_t, sin_t = _rope_tables()
    ck_a = cache_na_k.reshape(B_LAT, DEPTH, PAST, NA_H * HD).astype(BF16)
    cv_a = cache_na_v.reshape(B_LAT, DEPTH, PAST, NA_H * HD).astype(BF16)
    ck_c = cache_gqa_k.reshape(B_LAT, DEPTH, PAST, GQA_KV * HD).astype(BF16)
    cv_c = cache_gqa_v.reshape(B_LAT, DEPTH, PAST, GQA_KV * HD).astype(BF16)
    h0_lat = state_ssd.reshape(B_LAT, DEPTH, 2, SSD_INNER, SSD_N)
    prow = jnp.concatenate([ssd_dt_bias, ssd_a_log], axis=1)
    pcol = prow.transpose(0, 2, 1)
    dtot = jnp.repeat(ssd_d[:, 0] + ssd_d[:, 1], SSD_P, axis=-1)[:, None, :]

    x, h = _normmod(x_prompt.reshape(T_CTX, D), x_sample.reshape(T_LAT, D), norm_mix_pre, mods, 0, 0, 1)
    caches = None
    ssd_fin = None
    for l in range(DEPTH):
        u, u_dt, dt_t = _in_proj(h, wt_in, l)
        oa, oc, caches = _ctx_attn(u, gqa_q_norm, gqa_k_norm, l, caches)
        oc = _lat_gqa(u, ck_c, cv_c, gqa_q_norm, gqa_k_norm, cos_t, sin_t, oc, l)
        oa = _lat_na(u, ck_a, cv_a, t2, oa, l)
        xbc_c = _conv(u, ssd_conv_w, ssd_conv_b, l, B_CTX, L_CTX, 0)
        xbc_l = _conv(u, ssd_conv_w, ssd_conv_b, l, B_LAT, L_LAT, T_CTX)
        yf_c, yb_c, ssd_fin = _ssd(xbc_c, u_dt, dt_t, prow[l], pcol[l], dtot[l], l, B_CTX, L_CTX, 0, ssd_fin, True)
        yf_l, yb_l = _ssd(xbc_l, u_dt, dt_t, prow[l], pcol[l], dtot[l], l, B_LAT, L_LAT, T_CTX, h0_lat, False)
        ob = _ssd_gate(yf_c, yb_c, u, ssd_norm_w, l, 0)
        ob = _ssd_gate(yf_l, yb_l, u, ssd_norm_w, l, T_CTX, ob)
        mix = _merge(h, oa, ob, oc, wt_in, w_branch, gate_b, l)
        x, h2 = _proj_resid(mix, w_out, x, norm_mix_post, mods, l, 2, 512, (norm_ffn_pre, l, 3, 4), "out_proj")
        act = _matmul_swiglu(h2, ffn_w_up, l)
        nxt = (norm_mix_pre, l + 1, 0, 1) if l + 1 < DEPTH else None
        x, h = _proj_resid(act, w_down, x, norm_ffn_post, mods, l, 5, 256, nxt, "ffn_down")
    na_k, na_v, gqa_k, gqa_v = caches
    return (x.reshape(B_CTX, L_CTX, D), h.reshape(B_LAT, L_LAT, D),
            na_k.reshape(B_CTX, DEPTH, L_CTX, NA_H, HD), na_v.reshape(B_CTX, DEPTH, L_CTX, NA_H, HD),
            gqa_k.reshape(B_CTX, DEPTH, L_CTX, GQA_KV, HD), gqa_v.reshape(B_CTX, DEPTH, L_CTX, GQA_KV, HD),
            ssd_fin.reshape(B_CTX, DEPTH, 2, SSD_H, SSD_P, SSD_N))
```

```python
import functools
import math

import jax
import jax.numpy as jnp
from jax import lax
from jax.experimental import pallas as pl
from jax.experimental.pallas import tpu as pltpu

F32 = jnp.float32
BF16 = jnp.bfloat16
HI = lax.Precision.HIGHEST
EPS = 1e-6
NEG = -1e30

D = 2048
DEPTH = 4
B_CTX, L_CTX = 16, 256
B_LAT, L_LAT = 4, 2048
PAST = 512
GRID_W = 64
GRID_R = L_LAT // GRID_W
NA_H = 8
HD = 128
NA_ROWS, NA_COLS = 8, 16
GQA_H, GQA_KV = 8, 2
GQA_G = GQA_H // GQA_KV
ROPE_BASE = 10000.0
SSD_INNER = 1024
SSD_P = 64
SSD_H = SSD_INNER // SSD_P
SSD_N = 128
SSD_G = 2
CHUNK = 128
CONV_CH = SSD_INNER + 2 * SSD_G * SSD_N
BRANCH_W = 1024
FFN_H = 5632
T_CTX = B_CTX * L_CTX
T_LAT = B_LAT * L_LAT
T = T_CTX + T_LAT

C_QA, C_KA, C_VA = 0, NA_H * HD, 2 * NA_H * HD
C_QC = 3 * NA_H * HD
C_KC = C_QC + GQA_H * HD
C_VC = C_KC + GQA_KV * HD
C_Z = C_VC + GQA_KV * HD
C_XBC = C_Z + SSD_INNER
C_DT = C_XBC + CONV_CH
C_GL = C_DT + 2 * SSD_H
N_MAIN = C_DT

NT_DIMS = (((1,), (1,)), ((), ()))
TN_DIMS = (((0,), (0,)), ((), ()))

ROW_TILE = 256
SUBLANES = 8
VMEM_BIG = 56


def _cp(n_axes, vmem_mb=None):
    kw = dict(dimension_semantics=("arbitrary",) * n_axes)
    if vmem_mb is not None:
        kw["vmem_limit_bytes"] = vmem_mb * 1024 * 1024
    return pltpu.CompilerParams(**kw)


def _mod_row(i, tm):
    n_ctx = T_CTX // tm
    n_lat = L_LAT // tm
    return jnp.where(i < n_ctx, 0, 1 + (i - n_ctx) // n_lat)


def _rms(x, g):
    return x * lax.rsqrt(jnp.mean(x * x, axis=-1, keepdims=True) + EPS) * g


def _silu(x):
    return x * jax.nn.sigmoid(x)


def _mods_kernel(c_ref, w_ref, b_ref, o_ref):
    s = _silu(c_ref[...]).astype(BF16)
    o_ref[...] = jnp.dot(s, w_ref[...].astype(BF16), preferred_element_type=F32) + b_ref[...]


def _compute_mods(cvec, ada_w, ada_b):
    tn = 1024
    n = 6 * D
    return pl.pallas_call(
        _mods_kernel,
        grid=(DEPTH, n // tn),
        in_specs=[
            pl.BlockSpec((8, D), lambda l, j: (0, 0)),
            pl.BlockSpec((None, D, tn), lambda l, j: (l, 0, j)),
            pl.BlockSpec((None, 1, tn), lambda l, j: (l, 0, j)),
        ],
        out_specs=pl.BlockSpec((None, 8, tn), lambda l, j: (l, 0, j)),
        out_shape=jax.ShapeDtypeStruct((DEPTH, 8, n), F32),
        compiler_params=_cp(2),
        name="mods",
    )(cvec, ada_w, ada_b.reshape(DEPTH, 1, n))


def _mod_spec(l, k, tm):
    return pl.BlockSpec((None, None, None, 1, D), lambda i: (l, _mod_row(i, tm), k, 0, 0))


def _gain_spec(l):
    return pl.BlockSpec((None, 1, D), lambda i: (l, 0, 0))


def _normmod_kernel(xc_ref, xl_ref, g_ref, sh_ref, sc_ref, x_ref, h_ref):
    x = jnp.where(pl.program_id(0) < T_CTX // ROW_TILE, xc_ref[...], xl_ref[...])
    x_ref[...] = x
    h_ref[...] = (_rms(x, g_ref[...]) * (1.0 + sc_ref[...]) + sh_ref[...]).astype(BF16)


def _normmod(x_ctx, x_lat, gains, mods, l, k_sh, k_sc):
    tm = ROW_TILE
    n_ctx = T_CTX // tm
    row = pl.BlockSpec((tm, D), lambda i: (i, 0))
    return pl.pallas_call(
        _normmod_kernel,
        grid=(T // tm,),
        in_specs=[
            pl.BlockSpec((tm, D), lambda i: (jnp.minimum(i, n_ctx - 1), 0)),
            pl.BlockSpec((tm, D), lambda i: (jnp.maximum(i - n_ctx, 0), 0)),
            _gain_spec(l),
            _mod_spec(l, k_sh, tm),
            _mod_spec(l, k_sc, tm),
        ],
        out_specs=[row, row],
        out_shape=[jax.ShapeDtypeStruct((T, D), F32), jax.ShapeDtypeStruct((T, D), BF16)],
        compiler_params=_cp(1),
        name="normmod",
    )(x_ctx, x_lat, gains.reshape(DEPTH, 1, D), mods, mods)


DT_PAD = 128


def _in_proj_kernel(x_ref, wt_ref, wdt_ref, o_ref, odt_ref, odtt_ref, wb_ref):
    @pl.when(pl.program_id(1) == 0)
    def _():
        wb_ref[...] = wt_ref[...].astype(BF16).T

    x = x_ref[...]
    o_ref[...] = jnp.dot(x, wb_ref[...], preferred_element_type=F32).astype(o_ref.dtype)

    @pl.when(pl.program_id(0) == pl.num_programs(0) - 1)
    def _():
        dt = lax.dot_general(x, wdt_ref[...].astype(BF16), NT_DIMS, preferred_element_type=F32)
        odt_ref[...] = dt
        odtt_ref[...] = dt.T


def _in_proj(x, wt, l, tn=1024, tm=1024):
    m, k = x.shape
    nj = N_MAIN // tn
    last = lambda j, i: jnp.where(j == nj - 1, i, 0)
    return pl.pallas_call(
        _in_proj_kernel,
        grid=(nj, m // tm),
        in_specs=[
            pl.BlockSpec((tm, k), lambda j, i: (i, 0)),
            pl.BlockSpec((None, tn, k), lambda j, i: (l, j, 0)),
            pl.BlockSpec((None, DT_PAD, k), lambda j, i: (l, C_DT // DT_PAD, 0)),
        ],
        out_specs=[
            pl.BlockSpec((tm, tn), lambda j, i: (i, j)),
            pl.BlockSpec((tm, DT_PAD), lambda j, i: (last(j, i), 0)),
            pl.BlockSpec((DT_PAD, tm), lambda j, i: (0, last(j, i))),
        ],
        out_shape=[
            jax.ShapeDtypeStruct((m, N_MAIN), BF16),
            jax.ShapeDtypeStruct((m, DT_PAD), F32),
            jax.ShapeDtypeStruct((DT_PAD, m), F32),
        ],
        scratch_shapes=[pltpu.VMEM((k, tn), BF16)],
        compiler_params=_cp(2, 40),
        name="in_proj",
    )(x, wt, wt)


def _mm_swiglu_kernel(x_ref, wg_ref, wu_ref, o_ref, wgb_ref, wub_ref):
    @pl.when(pl.program_id(1) == 0)
    def _():
        wgb_ref[...] = wg_ref[...].astype(BF16)
        wub_ref[...] = wu_ref[...].astype(BF16)

    x = x_ref[...]
    g = jnp.dot(x, wgb_ref[...], preferred_element_type=F32)
    u = jnp.dot(x, wub_ref[...], preferred_element_type=F32)
    o_ref[...] = (_silu(g) * u).astype(o_ref.dtype)


def _matmul_swiglu(x, w, l, tn=512, tm=1024):
    m, k = x.shape
    nb = FFN_H // tn
    return pl.pallas_call(
        _mm_swiglu_kernel,
        grid=(nb, m // tm),
        in_specs=[
            pl.BlockSpec((tm, k), lambda j, i: (i, 0)),
            pl.BlockSpec((None, k, tn), lambda j, i: (l, 0, j)),
            pl.BlockSpec((None, k, tn), lambda j, i: (l, 0, nb + j)),
        ],
        out_specs=pl.BlockSpec((tm, tn), lambda j, i: (i, j)),
        out_shape=jax.ShapeDtypeStruct((m, FFN_H), BF16),
        scratch_shapes=[pltpu.VMEM((k, tn), BF16), pltpu.VMEM((k, tn), BF16)],
        compiler_params=_cp(2, 40),
        name="ffn_up",
    )(x, w, w)


def _merge_kernel(h_ref, oa_ref, ob_ref, oc_ref, wg0, wg1, wg2, wb0, wb1, wb2, gb0, gb1, gb2, o_ref, wgb_ref, wbb_ref):
    wg = (wg0, wg1, wg2)
    wb = (wb0, wb1, wb2)
    gb = (gb0, gb1, gb2)

    @pl.when(pl.program_id(1) == 0)
    def _():
        for g in range(3):
            wgb_ref[g] = wg[g][0].astype(BF16).T
            wbb_ref[g] = wb[g][...].astype(BF16)

    h = h_ref[...]
    acc = None
    for g, o_r in enumerate((oa_ref, ob_ref, oc_ref)):
        gl = jnp.dot(h, wgb_ref[g], preferred_element_type=F32) + gb[g][...]
        p = jnp.dot(o_r[...], wbb_ref[g], preferred_element_type=F32)
        t = jax.nn.sigmoid(gl) * p
        acc = t if acc is None else acc + t
    o_ref[...] = acc.astype(BF16)


def _merge(h, oa, ob, oc, wt_in, w_branch, gate_b, l, tn=512, tm=512):
    nb = D // tn
    xrow = pl.BlockSpec((tm, D), lambda j, i: (i, 0))
    orow = pl.BlockSpec((tm, BRANCH_W), lambda j, i: (i, 0))
    once = pl.Buffered(1)
    assert C_GL % SUBLANES == 0 and tn % SUBLANES == 0

    def wg_spec(g):
        row0 = (C_GL + g * D) // SUBLANES
        return pl.BlockSpec((pl.Element(1), pl.Element(tn), pl.Element(D)),
                            lambda j, i: (l, SUBLANES * (row0 + j * (tn // SUBLANES)), 0), pipeline_mode=once)

    def wb_spec(g):
        return pl.BlockSpec((None, None, BRANCH_W, tn), lambda j, i: (l, g, 0, j), pipeline_mode=once)

    def gb_spec(g):
        return pl.BlockSpec((None, 1, tn), lambda j, i: (l, 0, g * nb + j))

    gate_b3 = gate_b.reshape(DEPTH, 1, 3 * D)
    return pl.pallas_call(
        _merge_kernel,
        grid=(nb, T // tm),
        in_specs=[xrow, orow, orow, orow]
        + [wg_spec(g) for g in range(3)]
        + [wb_spec(g) for g in range(3)]
        + [gb_spec(g) for g in range(3)],
        out_specs=pl.BlockSpec((tm, tn), lambda j, i: (i, j)),
        out_shape=jax.ShapeDtypeStruct((T, D), BF16),
        scratch_shapes=[pltpu.VMEM((3, D, tn), BF16), pltpu.VMEM((3, BRANCH_W, tn), BF16)],
        compiler_params=_cp(2, 48),
        name="merge",
    )(h, oa, ob, oc, wt_in, wt_in, wt_in, w_branch, w_branch, w_branch, gate_b3, gate_b3, gate_b3)


def _proj_resid_kernel(a_ref, w_ref, x_ref, gpost_ref, gate_ref, *rest, with_h, cast_w):
    if cast_w:
        wb_ref = rest[-1]
        rest = rest[:-1]

        @pl.when(pl.program_id(0) == 0)
        def _():
            wb_ref[...] = w_ref[...].astype(BF16)

        w = wb_ref[...]
    else:
        w = w_ref[...]
    y = jnp.dot(a_ref[...], w, preferred_element_type=F32)
    xn = x_ref[...] + gate_ref[...] * _rms(y, gpost_ref[...])
    if with_h:
        gpre_ref, sh_ref, sc_ref, xo_ref, h_ref = rest
        xo_ref[...] = xn
        h_ref[...] = (_rms(xn, gpre_ref[...]) * (1.0 + sc_ref[...]) + sh_ref[...]).astype(BF16)
    else:
        oc_ref, ol_ref = rest
        n_ctx = T_CTX // xn.shape[0]

        @pl.when(pl.program_id(0) < n_ctx)
        def _():
            oc_ref[...] = xn

        @pl.when(pl.program_id(0) >= n_ctx)
        def _():
            ol_ref[...] = xn


def _proj_resid(a, w, x, post_gains, mods, l, k_gate, tm, nxt, name):
    k = a.shape[1]
    cast_w = w.dtype != BF16
    row = pl.BlockSpec((tm, D), lambda i: (i, 0))
    in_specs = [
        pl.BlockSpec((tm, k), lambda i: (i, 0)),
        pl.BlockSpec((None, k, D), lambda i: (l, 0, 0), pipeline_mode=pl.Buffered(1)),
        row, _gain_spec(l), _mod_spec(l, k_gate, tm),
    ]
    args = [a, w, x, post_gains.reshape(DEPTH, 1, D), mods]
    out_specs = [row]
    out_shape = [jax.ShapeDtypeStruct((T, D), F32)]
    if nxt is not None:
        gains, ln, k_sh, k_sc = nxt
        in_specs += [_gain_spec(ln), _mod_spec(ln, k_sh, tm), _mod_spec(ln, k_sc, tm)]
        args += [gains.reshape(DEPTH, 1, D), mods, mods]
        out_specs.append(row)
        out_shape.append(jax.ShapeDtypeStruct((T, D), BF16))
    else:
        n_ctx = T_CTX // tm
        out_specs = [pl.BlockSpec((tm, D), lambda i: (jnp.minimum(i, n_ctx - 1), 0)),
                     pl.BlockSpec((tm, D), lambda i: (jnp.maximum(i - n_ctx, 0), 0))]
        out_shape = [jax.ShapeDtypeStruct((T_CTX, D), F32), jax.ShapeDtypeStruct((T_LAT, D), F32)]
    out = pl.pallas_call(
        functools.partial(_proj_resid_kernel, with_h=nxt is not None, cast_w=cast_w),
        grid=(T // tm,),
        in_specs=in_specs,
        out_specs=out_specs,
        out_shape=out_shape,
        scratch_shapes=[pltpu.VMEM((k, D), BF16)] if cast_w else [],
        compiler_params=_cp(1, VMEM_BIG),
        name=name,
    )(*args)
    return out


ATT_SCALE = HD ** -0.5
ATT_SCALE_LOG2 = ATT_SCALE * math.log2(math.e)


def _softmax_pv(s, v):
    m = jnp.max(s, axis=-1, keepdims=True)
    e = jnp.exp(s - m)
    p = e * (1.0 / jnp.sum(e, axis=-1, keepdims=True))
    return jnp.dot(p.astype(BF16), v, preferred_element_type=F32)


def _softmax_pv2(s1, v1, s2, v2):
    m = jnp.maximum(jnp.max(s1, axis=-1, keepdims=True), jnp.max(s2, axis=-1, keepdims=True)) * ATT_SCALE_LOG2
    e1 = jnp.exp2(s1 * ATT_SCALE_LOG2 - m)
    e2 = jnp.exp2(s2 * ATT_SCALE_LOG2 - m)
    tot = jnp.sum(e1, axis=-1, keepdims=True) + jnp.sum(e2, axis=-1, keepdims=True)
    o = jnp.dot(e1.astype(BF16), v1, preferred_element_type=F32) + jnp.dot(e2.astype(BF16), v2,
                                                                           preferred_element_type=F32)
    return o * (1.0 / tot)


def _ctx_attn_kernel(a_ref, c_ref, qn_ref, kn_ref, *rest):
    oa_ref, oc_ref, nk_ref, nv_ref, kcn_ref, vcn_ref = rest[-6:]
    nk_ref[...] = a_ref[:, C_KA:C_KA + NA_H * HD].astype(F32)
    nv_ref[...] = a_ref[:, C_VA:C_VA + NA_H * HD].astype(F32)
    for h in range(NA_H):
        q = a_ref[:, C_QA + h * HD:C_QA + (h + 1) * HD]
        k = a_ref[:, C_KA + h * HD:C_KA + (h + 1) * HD]
        v = a_ref[:, C_VA + h * HD:C_VA + (h + 1) * HD]
        s = lax.dot_general(q, k, NT_DIMS, preferred_element_type=F32) * ATT_SCALE
        oa_ref[:, h * HD:(h + 1) * HD] = _softmax_pv(s, v).astype(BF16)
    kc0 = C_KC - C_QC
    vc0 = C_VC - C_QC
    vcn_ref[...] = c_ref[:, vc0:vc0 + GQA_KV * HD].astype(F32)
    for kv in range(GQA_KV):
        k = _rms(c_ref[:, kc0 + kv * HD:kc0 + (kv + 1) * HD].astype(F32), kn_ref[...])
        kcn_ref[:, kv * HD:(kv + 1) * HD] = k
        kb = k.astype(BF16)
        v = c_ref[:, vc0 + kv * HD:vc0 + (kv + 1) * HD]
        for g in range(GQA_G):
            h = kv * GQA_G + g
            q = _rms(c_ref[:, h * HD:(h + 1) * HD].astype(F32), qn_ref[...]).astype(BF16)
            s = lax.dot_general(q, kb, NT_DIMS, preferred_element_type=F32) * ATT_SCALE
            oc_ref[:, h * HD:(h + 1) * HD] = _softmax_pv(s, v).astype(BF16)


def _ctx_attn(u, qn, kn, l, caches):
    wa = 3 * NA_H * HD
    wc = C_Z - C_QC
    rows = lambda w: pl.BlockSpec((L_CTX, w), lambda b: (b, 0))
    lay = lambda w: pl.BlockSpec((None, None, L_CTX, w), lambda b: (b, l, 0, 0))
    cache_shape = lambda w: jax.ShapeDtypeStruct((B_CTX, DEPTH, L_CTX, w), F32)
    in_specs = [
        pl.BlockSpec((L_CTX, wa), lambda b: (b, 0)),
        pl.BlockSpec((L_CTX, wc), lambda b: (b, C_QC // wc)),
        pl.BlockSpec((None, 1, HD), lambda b: (l, 0, 0)),
        pl.BlockSpec((None, 1, HD), lambda b: (l, 0, 0)),
    ]
    args = [u, u, qn.reshape(DEPTH, 1, HD), kn.reshape(DEPTH, 1, HD)]
    aliases = {}
    if caches is not None:
        in_specs += [pl.BlockSpec(memory_space=pl.ANY)] * 4
        args += list(caches)
        aliases = {4 + n: 2 + n for n in range(4)}
    out = pl.pallas_call(
        _ctx_attn_kernel,
        grid=(B_CTX,),
        in_specs=in_specs,
        out_specs=[rows(BRANCH_W), rows(BRANCH_W), lay(NA_H * HD), lay(NA_H * HD), lay(GQA_KV * HD),
                   lay(GQA_KV * HD)],
        out_shape=[
            jax.ShapeDtypeStruct((T, BRANCH_W), BF16),
            jax.ShapeDtypeStruct((T, BRANCH_W), BF16),
            cache_shape(NA_H * HD), cache_shape(NA_H * HD), cache_shape(GQA_KV * HD), cache_shape(GQA_KV * HD),
        ],
        input_output_aliases=aliases,
        compiler_params=_cp(1),
        name="ctx_attn",
    )(*args)
    return out[0], out[1], tuple(out[2:])


def _rope(x, cos, sin):
    lane = lax.broadcasted_iota(jnp.int32, x.shape, 1)
    first = (lane & (HD // 2 - 1)) < (HD // 4)
    rot = jnp.where(first, -pltpu.roll(x, HD - HD // 4, 1), pltpu.roll(x, HD // 4, 1))
    return x * cos + rot * sin


def _lat_gqa_kernel(q_ref, k_ref, vl_ref, ck_ref, cv_ref, qn_ref, kn_ref, cos_ref, sin_ref, oc_in_ref, o_ref, kr_ref,
                    ve_ref):
    del oc_in_ref
    i = pl.program_id(2)
    tq = q_ref.shape[0]

    @pl.when(i == 0)
    def _():
        k = _rms(k_ref[...].astype(F32), kn_ref[...])
        kr_ref[...] = _rope(k, cos_ref[...], sin_ref[...]).astype(BF16)
        ve_ref[:, :HD] = vl_ref[...]
        ve_ref[:, HD:] = jnp.ones((L_LAT, HD), BF16)

    start = pl.multiple_of(i * tq, tq)
    cos = cos_ref[pl.ds(start, tq), :]
    sin = sin_ref[pl.ds(start, tq), :]
    kl = kr_ref[...]
    ve = ve_ref[...]
    ck = ck_ref[...]
    cve = jnp.concatenate([cv_ref[...], jnp.ones((PAST, HD), BF16)], axis=1)
    for g in range(GQA_G):
        q = _rope(_rms(q_ref[:, g * HD:(g + 1) * HD].astype(F32), qn_ref[...]), cos, sin)
        q = (q * ATT_SCALE_LOG2).astype(BF16)
        s1 = lax.dot_general(q, kl, NT_DIMS, preferred_element_type=F32)
        s2 = lax.dot_general(q, ck, NT_DIMS, preferred_element_type=F32)
        m = jnp.maximum(jnp.max(s1, axis=-1, keepdims=True), jnp.max(s2, axis=-1, keepdims=True))
        e1 = jnp.exp2(s1 - m).astype(BF16)
        e2 = jnp.exp2(s2 - m).astype(BF16)
        oe = jnp.dot(e1, ve, preferred_element_type=F32) + jnp.dot(e2, cve, preferred_element_type=F32)
        o_ref[:, g * HD:(g + 1) * HD] = (oe[:, :HD] * (1.0 / oe[:, HD:HD + 1])).astype(BF16)


def _lat_gqa(u, ck, cv, qn, kn, cos_t, sin_t, oc, l, tq=512):
    nq = L_LAT // tq
    r0 = T_CTX // tq
    s0 = T_CTX // L_LAT
    wq = GQA_G * HD
    full = lambda b, kv, i: (0, 0)
    return pl.pallas_call(
        _lat_gqa_kernel,
        grid=(B_LAT, GQA_KV, nq),
        in_specs=[
            pl.BlockSpec((tq, wq), lambda b, kv, i: (r0 + b * nq + i, C_QC // wq + kv)),
            pl.BlockSpec((L_LAT, HD), lambda b, kv, i: (s0 + b, C_KC // HD + kv)),
            pl.BlockSpec((L_LAT, HD), lambda b, kv, i: (s0 + b, C_VC // HD + kv)),
            pl.BlockSpec((None, None, PAST, HD), lambda b, kv, i: (b, l, 0, kv)),
            pl.BlockSpec((None, None, PAST, HD), lambda b, kv, i: (b, l, 0, kv)),
            pl.BlockSpec((None, 1, HD), lambda b, kv, i: (l, 0, 0)),
            pl.BlockSpec((None, 1, HD), lambda b, kv, i: (l, 0, 0)),
            pl.BlockSpec((L_LAT, HD), full),
            pl.BlockSpec((L_LAT, HD), full),
            pl.BlockSpec(memory_space=pl.ANY),
        ],
        out_specs=pl.BlockSpec((tq, wq), lambda b, kv, i: (r0 + b * nq + i, kv)),
        out_shape=jax.ShapeDtypeStruct((T, GQA_H * HD), BF16),
        scratch_shapes=[pltpu.VMEM((L_LAT, HD), BF16), pltpu.VMEM((L_LAT, 2 * HD), BF16)],
        input_output_aliases={9: 0},
        compiler_params=_cp(3),
        name="lat_gqa",
    )(u, u, u, ck, cv, qn.reshape(DEPTH, 1, HD), kn.reshape(DEPTH, 1, HD), cos_t, sin_t, oc)


NA_QROWS = 4
NA_UROWS = 12
NA_TQ = NA_QROWS * GRID_W
NA_TK = NA_UROWS * GRID_W
NA_HP = 4


def _lat_na_kernel(q_ref, k_ref, v_ref, ck_ref, cv_ref, t2_ref, oa_in_ref, o_ref, bias_ref):
    del oa_in_ref
    j = pl.program_id(2)
    ustart = jnp.clip(NA_QROWS * j - NA_ROWS // 2, 0, GRID_R - NA_UROWS)
    left = lax.broadcasted_iota(jnp.int32, (GRID_W, 2 * GRID_W), 1) < GRID_W
    for a in range(NA_QROWS):
        r = NA_QROWS * j + a
        rs = jnp.clip(r - NA_ROWS // 2, 0, GRID_R - NA_ROWS)
        for m in range(NA_UROWS // 2):
            kr0 = ustart + 2 * m
            v0 = ((kr0 >= rs) & (kr0 < rs + NA_ROWS)).astype(jnp.int32)
            v1 = ((kr0 + 1 >= rs) & (kr0 + 1 < rs + NA_ROWS)).astype(jnp.int32)
            idx = jnp.clip(kr0 - r + NA_ROWS, 0, 2 * NA_ROWS - 1)
            ok = jnp.where(left, v0, v1) > 0
            for hh in range(NA_HP):
                bias_ref[hh, a * GRID_W:(a + 1) * GRID_W, m * 2 * GRID_W:(m + 1) * 2 * GRID_W] = jnp.where(
                    ok, t2_ref[hh, idx], NEG)
    start = pl.multiple_of(ustart * GRID_W, GRID_W)
    for hh in range(NA_HP):
        cs = slice(hh * HD, (hh + 1) * HD)
        q = q_ref[:, cs]
        kl = k_ref[pl.ds(start, NA_TK), cs]
        vl = v_ref[pl.ds(start, NA_TK), cs]
        s_loc = lax.dot_general(q, kl, NT_DIMS, preferred_element_type=F32) + bias_ref[hh]
        s_ctx = lax.dot_general(q, ck_ref[:, cs], NT_DIMS, preferred_element_type=F32)
        o_ref[:, cs] = _softmax_pv2(s_loc, vl, s_ctx, cv_ref[:, cs]).astype(BF16)


def _lat_na(u, ck, cv, t2, oa, l):
    nj = GRID_R // NA_QROWS
    r0 = T_CTX // NA_TQ
    s0 = T_CTX // L_LAT
    w = NA_HP * HD
    return pl.pallas_call(
        _lat_na_kernel,
        grid=(B_LAT, NA_H // NA_HP, nj),
        in_specs=[
            pl.BlockSpec((NA_TQ, w), lambda b, h, j: (r0 + b * nj + j, C_QA // w + h)),
            pl.BlockSpec((L_LAT, w), lambda b, h, j: (s0 + b, C_KA // w + h)),
            pl.BlockSpec((L_LAT, w), lambda b, h, j: (s0 + b, C_VA // w + h)),
            pl.BlockSpec((None, None, PAST, w), lambda b, h, j: (b, l, 0, h)),
            pl.BlockSpec((None, None, PAST, w), lambda b, h, j: (b, l, 0, h)),
            pl.BlockSpec((None, NA_HP, 2 * NA_ROWS, GRID_W, 2 * GRID_W), lambda b, h, j: (l, h, 0, 0, 0)),
            pl.BlockSpec(memory_space=pl.ANY),
        ],
        out_specs=pl.BlockSpec((NA_TQ, w), lambda b, h, j: (r0 + b * nj + j, h)),
        out_shape=jax.ShapeDtypeStruct((T, NA_H * HD), BF16),
        scratch_shapes=[pltpu.VMEM((NA_HP, NA_TQ, NA_TK), F32)],
        input_output_aliases={6: 0},
        compiler_params=_cp(3),
        name="lat_na",
    )(u, u, u, ck, cv, t2, oa)


def _na_bias_tiles(rpb):
    w = jnp.arange(GRID_W)
    cs = jnp.clip(w - NA_COLS // 2, 0, GRID_W - NA_COLS)
    cmask = (w[None, :] >= cs[:, None]) & (w[None, :] < cs[:, None] + NA_COLS)
    dc = jnp.clip(w[None, :] - w[:, None] + NA_COLS - 1, 0, 2 * NA_COLS - 2)
    tm = jnp.where(cmask, rpb[:, :, :, dc] / ATT_SCALE, NEG)
    pad = jnp.full(tm.shape[:2] + (1,) + tm.shape[3:], NEG, F32)
    lo = jnp.concatenate([pad, tm], axis=2)
    hi = jnp.concatenate([tm, pad], axis=2)
    return jnp.concatenate([lo, hi], axis=-1).astype(F32)


def _rope_tables():
    nf = HD // 4
    t = jnp.arange(L_LAT)
    inv = ROPE_BASE ** (-jnp.arange(nf, dtype=F32) / nf)
    ar = (t // GRID_W).astype(F32)[:, None] * inv[None, :]
    ac = (t % GRID_W).astype(F32)[:, None] * inv[None, :]
    cos_t = jnp.concatenate([jnp.cos(ar), jnp.cos(ar), jnp.cos(ac), jnp.cos(ac)], axis=-1)
    sin_t = jnp.concatenate([jnp.sin(ar), jnp.sin(ar), jnp.sin(ac), jnp.sin(ac)], axis=-1)
    return cos_t, sin_t


def _conv_kernel(x_ref, w_ref, b_ref, o_ref):
    x = x_ref[...].astype(F32)
    n = x.shape[0]
    row = lax.broadcasted_iota(jnp.int32, (SUBLANES, x.shape[1]), 0)
    xm = pltpu.roll(x, 1, 0)
    xm = jnp.concatenate([jnp.where(row == 0, 0.0, xm[:SUBLANES]), xm[SUBLANES:]], axis=0)
    xp = pltpu.roll(x, n - 1, 0)
    xp = jnp.concatenate([xp[:n - SUBLANES], jnp.where(row == SUBLANES - 1, 0.0, xp[n - SUBLANES:])], axis=0)
    y = w_ref[0:1, :] * xm + w_ref[1:2, :] * x + w_ref[2:3, :] * xp + b_ref[...]
    o_ref[...] = _silu(y).astype(o_ref.dtype)


def _conv(u, conv_w, conv_b, l, nseq, seq_len, row0):
    tc = 512
    nc = CONV_CH // tc
    rb0 = row0 // seq_len
    cb0 = C_XBC // tc
    return pl.pallas_call(
        _conv_kernel,
        grid=(nseq, nc),
        in_specs=[
            pl.BlockSpec((seq_len, tc), lambda b, j: (rb0 + b, cb0 + j)),
            pl.BlockSpec((None, 3, tc), lambda b, j: (l, 0, j)),
            pl.BlockSpec((None, 1, tc), lambda b, j: (l, 0, j)),
        ],
        out_specs=pl.BlockSpec((seq_len, tc), lambda b, j: (b, j)),
        out_shape=jax.ShapeDtypeStruct((nseq * seq_len, CONV_CH), F32),
        compiler_params=_cp(2),
        name="conv",
    )(u, conv_w, conv_b.reshape(DEPTH, 1, CONV_CH))


def _softplus(x):
    return jnp.maximum(x, 0.0) + jnp.log1p(jnp.exp(-jnp.abs(x)))


def _expand_heads(v, expand, terms):
    out = None
    rem = v
    for _ in range(terms):
        piece = rem.astype(BF16)
        rem = rem - piece.astype(F32)
        t = jnp.dot(piece, expand, preferred_element_type=F32)
        out = t if out is None else out + t
    return out


def _ssd_chunk(xbc, dt_col, dt_row, a_row, a_col, h_ref, reverse):
    x = xbc[:, :SSD_INNER]
    bm = xbc[:, SSD_INNER:SSD_INNER + SSD_G * SSD_N]
    cm = xbc[:, SSD_INNER + SSD_G * SSD_N:]
    li = lax.broadcasted_iota(jnp.int32, (CHUNK, CHUNK), 0)
    si = lax.broadcasted_iota(jnp.int32, (CHUNK, CHUNK), 1)
    mask = (si >= li) if reverse else (si <= li)
    tri = mask.astype(F32)
    tri_t = ((li >= si) if reverse else (li <= si)).astype(F32)
    cum_col = jnp.dot(tri, dt_col * a_row, precision=HI, preferred_element_type=F32)
    cum_row = jnp.dot(dt_row * a_col, tri_t, precision=HI, preferred_element_type=F32)
    tot_row = cum_col[0:1, :] if reverse else cum_col[CHUNK - 1:CHUNK, :]
    col = lax.broadcasted_iota(jnp.int32, (SSD_H, SSD_INNER), 1)
    lo_col = lax.broadcasted_iota(jnp.int32, (SSD_H, SSD_INNER), 0) * SSD_P
    expand = ((col >= lo_col) & (col < lo_col + SSD_P)).astype(BF16)
    w_exp = _expand_heads(jnp.exp(tot_row - cum_col) * dt_col, expand, 1)
    e_exp = _expand_heads(jnp.exp(cum_col), expand, 1)
    dec = _expand_heads(jnp.broadcast_to(jnp.exp(tot_row), (8, SSD_H)), expand, 3)[0:1, :]
    hpg = SSD_H // SSD_G
    xb = x.astype(BF16)
    bgs = [bm[:, g * SSD_N:(g + 1) * SSD_N].astype(BF16) for g in range(SSD_G)]
    cgs = [cm[:, g * SSD_N:(g + 1) * SSD_N].astype(BF16) for g in range(SSD_G)]
    gmat = [lax.dot_general(cgs[g], bgs[g], NT_DIMS, preferred_element_type=F32) for g in range(SSD_G)]
    lo = lax.broadcasted_iota(jnp.int32, (CHUNK, 2 * SSD_P), 1) < SSD_P
    zero = jnp.zeros((CHUNK, 2 * SSD_P), BF16)
    ys = []
    for pr in range(SSD_H // 2):
        g = (2 * pr) // hpg
        ms = []
        for h in (2 * pr, 2 * pr + 1):
            seg = cum_col[:, h:h + 1] - cum_row[h:h + 1, :]
            lm = jnp.exp(jnp.where(mask, seg, NEG))
            ms.append((gmat[g] * lm * dt_row[h:h + 1, :]).astype(BF16))
        xp = xb[:, pr * 2 * SSD_P:(pr + 1) * 2 * SSD_P]
        rhs = jnp.concatenate([jnp.where(lo, xp, zero), jnp.where(lo, zero, xp)], axis=0)
        ys.append(jnp.dot(jnp.concatenate(ms, axis=1), rhs, preferred_element_type=F32))
    y = jnp.concatenate(ys, axis=1)
    h_t = h_ref[...]
    hb = h_t.astype(BF16)
    xw = (x * w_exp).astype(BF16)
    wpg = SSD_INNER // SSD_G
    yo = [jnp.dot(cgs[g], hb[:, g * wpg:(g + 1) * wpg], preferred_element_type=F32) for g in range(SSD_G)]
    st = [lax.dot_general(bgs[g], xw[:, g * wpg:(g + 1) * wpg], TN_DIMS, preferred_element_type=F32)
          for g in range(SSD_G)]
    h_ref[...] = h_t * dec + jnp.concatenate(st, axis=1)
    return y + jnp.concatenate(yo, axis=1) * e_exp


def _ssd_kernel(xf_ref, xb_ref, dtf_ref, dtb_ref, dttf_ref, dttb_ref, prow_ref, pcol_ref, dtot_ref, *rest,
                context):
    hf_ref, hb_ref = rest[-2:]
    if context:
        yf_ref, yb_ref, hfin_ref = rest[-5:-2]
    else:
        h0_ref, yf_ref, yb_ref = rest[-5:-2]
    c = pl.program_id(1)

    @pl.when(c == 0)
    def _():
        if context:
            hf_ref[...] = jnp.zeros_like(hf_ref)
            hb_ref[...] = jnp.zeros_like(hb_ref)
        else:
            hf_ref[...] = h0_ref[0].T
            hb_ref[...] = h0_ref[1].T

    prow = prow_ref[...]
    pcol = pcol_ref[...]
    xf = xf_ref[...]
    dt_col = _softplus(dtf_ref[:, 0:SSD_H] + prow[0:1, :])
    dt_row = _softplus(dttf_ref[0:SSD_H, :] + pcol[:, 0:1])
    yf = _ssd_chunk(xf, dt_col, dt_row, -jnp.exp(prow[2:3, :]), -jnp.exp(pcol[:, 2:3]), hf_ref, False)
    yf_ref[...] = (yf + dtot_ref[...] * xf[:, :SSD_INNER]).astype(yf_ref.dtype)
    dt_col = _softplus(dtb_ref[:, SSD_H:2 * SSD_H] + prow[1:2, :])
    dt_row = _softplus(dttb_ref[SSD_H:2 * SSD_H, :] + pcol[:, 1:2])
    yb_ref[...] = _ssd_chunk(xb_ref[...], dt_col, dt_row, -jnp.exp(prow[3:4, :]), -jnp.exp(pcol[:, 3:4]),
                             hb_ref, True).astype(yb_ref.dtype)

    if context:
        @pl.when(c == pl.num_programs(1) - 1)
        def _():
            hfin_ref[0] = hf_ref[...].T
            hfin_ref[1] = hb_ref[...].T


def _ssd(xbc, u_dt, dt_t, prow, pcol, dtot, l, nseq, seq_len, row0, states, context):
    nc = seq_len // CHUNK
    cb0 = row0 // CHUNK
    rows = nseq * seq_len
    nd = 2 * SSD_H
    fwd = lambda b, c: b * nc + c
    bwd = lambda b, c: b * nc + (nc - 1 - c)
    state_spec = pl.BlockSpec((None, None, 2, SSD_INNER, SSD_N), lambda b, c: (b, l, 0, 0, 0))
    in_specs = [
        pl.BlockSpec((CHUNK, CONV_CH), lambda b, c: (fwd(b, c), 0)),
        pl.BlockSpec((CHUNK, CONV_CH), lambda b, c: (bwd(b, c), 0)),
        pl.BlockSpec((CHUNK, DT_PAD), lambda b, c: (cb0 + fwd(b, c), 0)),
        pl.BlockSpec((CHUNK, DT_PAD), lambda b, c: (cb0 + bwd(b, c), 0)),
        pl.BlockSpec((DT_PAD, CHUNK), lambda b, c: (0, cb0 + fwd(b, c))),
        pl.BlockSpec((DT_PAD, CHUNK), lambda b, c: (0, cb0 + bwd(b, c))),
        pl.BlockSpec((4, SSD_H), lambda b, c: (0, 0)),
        pl.BlockSpec((SSD_H, 4), lambda b, c: (0, 0)),
        pl.BlockSpec((1, SSD_INNER), lambda b, c: (0, 0)),
    ]
    args = [xbc, xbc, u_dt, u_dt, dt_t, dt_t, prow, pcol, dtot]
    out_specs = [
        pl.BlockSpec((CHUNK, SSD_INNER), lambda b, c: (fwd(b, c), 0)),
        pl.BlockSpec((CHUNK, SSD_INNER), lambda b, c: (bwd(b, c), 0)),
    ]
    out_shape = [jax.ShapeDtypeStruct((rows, SSD_INNER), BF16), jax.ShapeDtypeStruct((rows, SSD_INNER), BF16)]
    aliases = {}
    if context:
        out_specs.append(state_spec)
        out_shape.append(jax.ShapeDtypeStruct((nseq, DEPTH, 2, SSD_INNER, SSD_N), F32))
        if states is not None:
            in_specs.append(pl.BlockSpec(memory_space=pl.ANY))
            args.append(states)
            aliases = {len(args) - 1: 2}
    else:
        in_specs.append(state_spec)
        args.append(states)
    return pl.pallas_call(
        functools.partial(_ssd_kernel, context=context),
        grid=(nseq, nc),
        in_specs=in_specs,
        out_specs=out_specs,
        out_shape=out_shape,
        scratch_shapes=[pltpu.VMEM((SSD_N, SSD_INNER), F32), pltpu.VMEM((SSD_N, SSD_INNER), F32)],
        input_output_aliases=aliases,
        compiler_params=_cp(2),
        name="ssd",
    )(*args)


def _ssd_gate_kernel(yf_ref, yb_ref, z_ref, g_ref, *rest):
    o_ref = rest[-1]
    y = (yf_ref[...].astype(F32) + yb_ref[...].astype(F32)) * _silu(z_ref[...].astype(F32))
    o_ref[...] = _rms(y, g_ref[...]).astype(BF16)


def _ssd_gate(yf, yb, u, norm_w, l, row0, ob=None):
    tm = ROW_TILE
    rows = yf.shape[0]
    rb0 = row0 // tm
    row = pl.BlockSpec((tm, SSD_INNER), lambda i: (i, 0))
    in_specs = [
        row, row,
        pl.BlockSpec((pl.Element(tm), pl.Element(SSD_INNER)), lambda i: (tm * (rb0 + i), C_Z)),
        pl.BlockSpec((None, 1, SSD_INNER), lambda i: (l, 0, 0)),
    ]
    args = [yf, yb, u, norm_w.reshape(DEPTH, 1, SSD_INNER)]
    aliases = {}
    if ob is not None:
        in_specs.append(pl.BlockSpec(memory_space=pl.ANY))
        args.append(ob)
        aliases = {4: 0}
    return pl.pallas_call(
        _ssd_gate_kernel,
        grid=(rows // tm,),
        in_specs=in_specs,
        out_specs=pl.BlockSpec((tm, SSD_INNER), lambda i: (rb0 + i, 0)),
        out_shape=jax.ShapeDtypeStruct((T, SSD_INNER), BF16),
        input_output_aliases=aliases,
        compiler_params=_cp(1),
        name="ssd_gate",
    )(*args)


def kernel(x_prompt, x_sample, cache_na_k, cache_na_v, cache_gqa_k, cache_gqa_v, state_ssd, c, c_ctx, ada_w, ada_b, norm_mix_pre, norm_mix_post, norm_ffn_pre, norm_ffn_post, w_in, gate_b, na_rpb, gqa_q_norm, gqa_k_norm, ssd_conv_w, ssd_conv_b, ssd_dt_bias, ssd_a_log, ssd_d, ssd_norm_w, w_branch, w_out, ffn_w_up, ffn_w_down):
    cvec = jnp.concatenate([c_ctx[None, :], c, jnp.zeros((8 - 1 - B_LAT, D), F32)], axis=0)
    mods = _compute_mods(cvec, ada_w, ada_b).reshape(DEPTH, 8, 6, 1, D)
    wt_in = jnp.swapaxes(w_in, 1, 2)
    w_down = ffn_w_down.astype(BF16)
    t2 = _na_bias_tiles(na_rpb)
    cos_t, sin_t = _rope_tables()
    ck_a = cache_na_k.reshape(B_LAT, DEPTH, PAST, NA_H * HD).astype(BF16)
    cv_a = cache_na_v.reshape(B_LAT, DEPTH, PAST, NA_H * HD).astype(BF16)
    ck_c = cache_gqa_k.reshape(B_LAT, DEPTH, PAST, GQA_KV * HD).astype(BF16)
    cv_c = cache_gqa_v.reshape(B_LAT, DEPTH, PAST, GQA_KV * HD).astype(BF16)
    h0_lat = state_ssd.reshape(B_LAT, DEPTH, 2, SSD_INNER, SSD_N)
    prow = jnp.concatenate([ssd_dt_bias, ssd_a_log], axis=1)
    pcol = prow.transpose(0, 2, 1)
    dtot = jnp.repeat(ssd_d[:, 0] + ssd_d[:, 1], SSD_P, axis=-1)[:, None, :]

    x, h = _normmod(x_prompt.reshape(T_CTX, D), x_sample.reshape(T_LAT, D), norm_mix_pre, mods, 0, 0, 1)
    caches = None
    ssd_fin = None
    for l in range(DEPTH):
        u, u_dt, dt_t = _in_proj(h, wt_in, l)
        oa, oc, caches = _ctx_attn(u, gqa_q_norm, gqa_k_norm, l, caches)
        oc = _lat_gqa(u, ck_c, cv_c, gqa_q_norm, gqa_k_norm, cos_t, sin_t, oc, l)
        oa = _lat_na(u, ck_a, cv_a, t2, oa, l)
        xbc_c = _conv(u, ssd_conv_w, ssd_conv_b, l, B_CTX, L_CTX, 0)
        xbc_l = _conv(u, ssd_conv_w, ssd_conv_b, l, B_LAT, L_LAT, T_CTX)
        yf_c, yb_c, ssd_fin = _ssd(xbc_c, u_dt, dt_t, prow[l], pcol[l], dtot[l], l, B_CTX, L_CTX, 0, ssd_fin, True)
        yf_l, yb_l = _ssd(xbc_l, u_dt, dt_t, prow[l], pcol[l], dtot[l], l, B_LAT, L_LAT, T_CTX, h0_lat, False)
        ob = _ssd_gate(yf_c, yb_c, u, ssd_norm_w, l, 0)
        ob = _ssd_gate(yf_l, yb_l, u, ssd_norm_w, l, T_CTX, ob)
        mix = _merge(h, oa, ob, oc, wt_in, w_branch, gate_b, l)
        x, h2 = _proj_resid(mix, w_out, x, norm_mix_post, mods, l, 2, 512, (norm_ffn_pre, l, 3, 4), "out_proj")
        act = _matmul_swiglu(h2, ffn_w_up, l)
        nxt = (norm_mix_pre, l + 1, 0, 1) if l + 1 < DEPTH else None
        x, h = _proj_resid(act, w_down, x, norm_ffn_post, mods, l, 5, 256, nxt, "ffn_down")
    na_k, na_v, gqa_k, gqa_v = caches
    return (x.reshape(B_CTX, L_CTX, D), h.reshape(B_LAT, L_LAT, D),
            na_k.reshape(B_CTX, DEPTH, L_CTX, NA_H, HD), na_v.reshape(B_CTX, DEPTH, L_CTX, NA_H, HD),
            gqa_k.reshape(B_CTX, DEPTH, L_CTX, GQA_KV, HD), gqa_v.reshape(B_CTX, DEPTH, L_CTX, GQA_KV, HD),
            ssd_fin.reshape(B_CTX, DEPTH, 2, SSD_H, SSD_P, SSD_N))
```

```python
import functools
import math

import jax
import jax.numpy as jnp
from jax import lax
from jax.experimental import pallas as pl
from jax.experimental.pallas import tpu as pltpu

F32 = jnp.float32
BF16 = jnp.bfloat16
HI = lax.Precision.HIGHEST
EPS = 1e-6
NEG = -1e30

D = 2048
DEPTH = 4
B_CTX, L_CTX = 16, 256
B_LAT, L_LAT = 4, 2048
PAST = 512
GRID_W = 64
GRID_R = L_LAT // GRID_W
NA_H = 8
HD = 128
NA_ROWS, NA_COLS = 8, 16
GQA_H, GQA_KV = 8, 2
GQA_G = GQA_H // GQA_KV
ROPE_BASE = 10000.0
SSD_INNER = 1024
SSD_P = 64
SSD_H = SSD_INNER // SSD_P
SSD_N = 128
SSD_G = 2
CHUNK = 128
CONV_CH = SSD_INNER + 2 * SSD_G * SSD_N
BRANCH_W = 1024
FFN_H = 5632
T_CTX = B_CTX * L_CTX
T_LAT = B_LAT * L_LAT
T = T_CTX + T_LAT

C_QA, C_KA, C_VA = 0, NA_H * HD, 2 * NA_H * HD
C_QC = 3 * NA_H * HD
C_KC = C_QC + GQA_H * HD
C_VC = C_KC + GQA_KV * HD
C_Z = C_VC + GQA_KV * HD
C_XBC = C_Z + SSD_INNER
C_DT = C_XBC + CONV_CH
C_GL = C_DT + 2 * SSD_H
N_MAIN = C_DT

NT_DIMS = (((1,), (1,)), ((), ()))
TN_DIMS = (((0,), (0,)), ((), ()))

ROW_TILE = 256
SUBLANES = 8
VMEM_BIG = 56


def _cp(n_axes, vmem_mb=None):
    kw = dict(dimension_semantics=("arbitrary",) * n_axes)
    if vmem_mb is not None:
        kw["vmem_limit_bytes"] = vmem_mb * 1024 * 1024
    return pltpu.CompilerParams(**kw)


def _mod_row(i, tm):
    n_ctx = T_CTX // tm
    n_lat = L_LAT // tm
    return jnp.where(i < n_ctx, 0, 1 + (i - n_ctx) // n_lat)


def _rms(x, g):
    return x * lax.rsqrt(jnp.mean(x * x, axis=-1, keepdims=True) + EPS) * g


def _silu(x):
    return x * jax.nn.sigmoid(x)


def _mods_kernel(c_ref, w_ref, b_ref, o_ref):
    s = _silu(c_ref[...]).astype(BF16)
    o_ref[...] = jnp.dot(s, w_ref[...].astype(BF16), preferred_element_type=F32) + b_ref[...]


def _compute_mods(cvec, ada_w, ada_b):
    tn = 1024
    n = 6 * D
    return pl.pallas_call(
        _mods_kernel,
        grid=(DEPTH, n // tn),
        in_specs=[
            pl.BlockSpec((8, D), lambda l, j: (0, 0)),
            pl.BlockSpec((None, D, tn), lambda l, j: (l, 0, j)),
            pl.BlockSpec((None, 1, tn), lambda l, j: (l, 0, j)),
        ],
        out_specs=pl.BlockSpec((None, 8, tn), lambda l, j: (l, 0, j)),
        out_shape=jax.ShapeDtypeStruct((DEPTH, 8, n), F32),
        compiler_params=_cp(2),
        name="mods",
    )(cvec, ada_w, ada_b.reshape(DEPTH, 1, n))


def _mod_spec(l, k, tm):
    return pl.BlockSpec((None, None, None, 1, D), lambda i: (l, _mod_row(i, tm), k, 0, 0))


def _gain_spec(l):
    return pl.BlockSpec((None, 1, D), lambda i: (l, 0, 0))


def _normmod_kernel(xc_ref, xl_ref, g_ref, sh_ref, sc_ref, x_ref, h_ref):
    x = jnp.where(pl.program_id(0) < T_CTX // ROW_TILE, xc_ref[...], xl_ref[...])
    x_ref[...] = x
    h_ref[...] = (_rms(x, g_ref[...]) * (1.0 + sc_ref[...]) + sh_ref[...]).astype(BF16)


def _normmod(x_ctx, x_lat, gains, mods, l, k_sh, k_sc):
    tm = ROW_TILE
    n_ctx = T_CTX // tm
    row = pl.BlockSpec((tm, D), lambda i: (i, 0))
    return pl.pallas_call(
        _normmod_kernel,
        grid=(T // tm,),
        in_specs=[
            pl.BlockSpec((tm, D), lambda i: (jnp.minimum(i, n_ctx - 1), 0)),
            pl.BlockSpec((tm, D), lambda i: (jnp.maximum(i - n_ctx, 0), 0)),
            _gain_spec(l),
            _mod_spec(l, k_sh, tm),
            _mod_spec(l, k_sc, tm),
        ],
        out_specs=[row, row],
        out_shape=[jax.ShapeDtypeStruct((T, D), F32), jax.ShapeDtypeStruct((T, D), BF16)],
        compiler_params=_cp(1),
        name="normmod",
    )(x_ctx, x_lat, gains.reshape(DEPTH, 1, D), mods, mods)


DT_PAD = 128


def _in_proj_kernel(x_ref, wt_ref, wdt_ref, o_ref, odt_ref, odtt_ref, wb_ref):
    @pl.when(pl.program_id(1) == 0)
    def _():
        wb_ref[...] = wt_ref[...].astype(BF16).T

    x = x_ref[...]
    o_ref[...] = jnp.dot(x, wb_ref[...], preferred_element_type=F32).astype(o_ref.dtype)

    @pl.when(pl.program_id(0) == pl.num_programs(0) - 1)
    def _():
        dt = lax.dot_general(x, wdt_ref[...].astype(BF16), NT_DIMS, preferred_element_type=F32)
        odt_ref[...] = dt
        odtt_ref[...] = dt.T


def _in_proj(x, wt, l, tn=1024, tm=1024):
    m, k = x.shape
    nj = N_MAIN // tn
    last = lambda j, i: jnp.where(j == nj - 1, i, 0)
    return pl.pallas_call(
        _in_proj_kernel,
        grid=(nj, m // tm),
        in_specs=[
            pl.BlockSpec((tm, k), lambda j, i: (i, 0)),
            pl.BlockSpec((None, tn, k), lambda j, i: (l, j, 0)),
            pl.BlockSpec((None, DT_PAD, k), lambda j, i: (l, C_DT // DT_PAD, 0)),
        ],
        out_specs=[
            pl.BlockSpec((tm, tn), lambda j, i: (i, j)),
            pl.BlockSpec((tm, DT_PAD), lambda j, i: (last(j, i), 0)),
            pl.BlockSpec((DT_PAD, tm), lambda j, i: (0, last(j, i))),
        ],
        out_shape=[
            jax.ShapeDtypeStruct((m, N_MAIN), BF16),
            jax.ShapeDtypeStruct((m, DT_PAD), F32),
            jax.ShapeDtypeStruct((DT_PAD, m), F32),
        ],
        scratch_shapes=[pltpu.VMEM((k, tn), BF16)],
        compiler_params=_cp(2, 40),
        name="in_proj",
    )(x, wt, wt)


def _mm_swiglu_kernel(x_ref, wg_ref, wu_ref, o_ref, wgb_ref, wub_ref):
    @pl.when(pl.program_id(1) == 0)
    def _():
        wgb_ref[...] = wg_ref[...].astype(BF16)
        wub_ref[...] = wu_ref[...].astype(BF16)

    x = x_ref[...]
    g = jnp.dot(x, wgb_ref[...], preferred_element_type=F32)
    u = jnp.dot(x, wub_ref[...], preferred_element_type=F32)
    o_ref[...] = (_silu(g) * u).astype(o_ref.dtype)


def _matmul_swiglu(x, w, l, tn=512, tm=1024):
    m, k = x.shape
    nb = FFN_H // tn
    return pl.pallas_call(
        _mm_swiglu_kernel,
        grid=(nb, m // tm),
        in_specs=[
            pl.BlockSpec((tm, k), lambda j, i: (i, 0)),
            pl.BlockSpec((None, k, tn), lambda j, i: (l, 0, j)),
            pl.BlockSpec((None, k, tn), lambda j, i: (l, 0, nb + j)),
        ],
        out_specs=pl.BlockSpec((tm, tn), lambda j, i: (i, j)),
        out_shape=jax.ShapeDtypeStruct((m, FFN_H), BF16),
        scratch_shapes=[pltpu.VMEM((k, tn), BF16), pltpu.VMEM((k, tn), BF16)],
        compiler_params=_cp(2, 40),
        name="ffn_up",
    )(x, w, w)


def _merge_kernel(h_ref, oa_ref, ob_ref, oc_ref, wg0, wg1, wg2, wb0, wb1, wb2, gb0, gb1, gb2, o_ref, wgb_ref, wbb_ref):
    wg = (wg0, wg1, wg2)
    wb = (wb0, wb1, wb2)
    gb = (gb0, gb1, gb2)

    @pl.when(pl.program_id(1) == 0)
    def _():
        for g in range(3):
            wgb_ref[g] = wg[g][0].astype(BF16).T
            wbb_ref[g] = wb[g][...].astype(BF16)

    h = h_ref[...]
    acc = None
    for g, o_r in enumerate((oa_ref, ob_ref, oc_ref)):
        gl = jnp.dot(h, wgb_ref[g], preferred_element_type=F32) + gb[g][...]
        p = jnp.dot(o_r[...], wbb_ref[g], preferred_element_type=F32)
        t = jax.nn.sigmoid(gl) * p
        acc = t if acc is None else acc + t
    o_ref[...] = acc.astype(BF16)


def _merge(h, oa, ob, oc, wt_in, w_branch, gate_b, l, tn=512, tm=512):
    nb = D // tn
    xrow = pl.BlockSpec((tm, D), lambda j, i: (i, 0))
    orow = pl.BlockSpec((tm, BRANCH_W), lambda j, i: (i, 0))
    once = pl.Buffered(1)
    assert C_GL % SUBLANES == 0 and tn % SUBLANES == 0

    def wg_spec(g):
        row0 = (C_GL + g * D) // SUBLANES
        return pl.BlockSpec((pl.Element(1), pl.Element(tn), pl.Element(D)),
                            lambda j, i: (l, SUBLANES * (row0 + j * (tn // SUBLANES)), 0), pipeline_mode=once)

    def wb_spec(g):
        return pl.BlockSpec((None, None, BRANCH_W, tn), lambda j, i: (l, g, 0, j), pipeline_mode=once)

    def gb_spec(g):
        return pl.BlockSpec((None, 1, tn), lambda j, i: (l, 0, g * nb + j))

    gate_b3 = gate_b.reshape(DEPTH, 1, 3 * D)
    return pl.pallas_call(
        _merge_kernel,
        grid=(nb, T // tm),
        in_specs=[xrow, orow, orow, orow]
        + [wg_spec(g) for g in range(3)]
        + [wb_spec(g) for g in range(3)]
        + [gb_spec(g) for g in range(3)],
        out_specs=pl.BlockSpec((tm, tn), lambda j, i: (i, j)),
        out_shape=jax.ShapeDtypeStruct((T, D), BF16),
        scratch_shapes=[pltpu.VMEM((3, D, tn), BF16), pltpu.VMEM((3, BRANCH_W, tn), BF16)],
        compiler_params=_cp(2, 48),
        name="merge",
    )(h, oa, ob, oc, wt_in, wt_in, wt_in, w_branch, w_branch, w_branch, gate_b3, gate_b3, gate_b3)


def _proj_resid_kernel(a_ref, w_ref, x_ref, gpost_ref, gate_ref, *rest, with_h, cast_w):
    if cast_w:
        wb_ref = rest[-1]
        rest = rest[:-1]

        @pl.when(pl.program_id(0) == 0)
        def _():
            wb_ref[...] = w_ref[...].astype(BF16)

        w = wb_ref[...]
    else:
        w = w_ref[...]
    y = jnp.dot(a_ref[...], w, preferred_element_type=F32)
    xn = x_ref[...] + gate_ref[...] * _rms(y, gpost_ref[...])
    if with_h:
        gpre_ref, sh_ref, sc_ref, xo_ref, h_ref = rest
        xo_ref[...] = xn
        h_ref[...] = (_rms(xn, gpre_ref[...]) * (1.0 + sc_ref[...]) + sh_ref[...]).astype(BF16)
    else:
        oc_ref, ol_ref = rest
        n_ctx = T_CTX // xn.shape[0]

        @pl.when(pl.program_id(0) < n_ctx)
        def _():
            oc_ref[...] = xn

        @pl.when(pl.program_id(0) >= n_ctx)
        def _():
            ol_ref[...] = xn


def _proj_resid(a, w, x, post_gains, mods, l, k_gate, tm, nxt, name):
    k = a.shape[1]
    cast_w = w.dtype != BF16
    row = pl.BlockSpec((tm, D), lambda i: (i, 0))
    in_specs = [
        pl.BlockSpec((tm, k), lambda i: (i, 0)),
        pl.BlockSpec((None, k, D), lambda i: (l, 0, 0), pipeline_mode=pl.Buffered(1)),
        row, _gain_spec(l), _mod_spec(l, k_gate, tm),
    ]
    args = [a, w, x, post_gains.reshape(DEPTH, 1, D), mods]
    out_specs = [row]
    out_shape = [jax.ShapeDtypeStruct((T, D), F32)]
    if nxt is not None:
        gains, ln, k_sh, k_sc = nxt
        in_specs += [_gain_spec(ln), _mod_spec(ln, k_sh, tm), _mod_spec(ln, k_sc, tm)]
        args += [gains.reshape(DEPTH, 1, D), mods, mods]
        out_specs.append(row)
        out_shape.append(jax.ShapeDtypeStruct((T, D), BF16))
    else:
        n_ctx = T_CTX // tm
        out_specs = [pl.BlockSpec((tm, D), lambda i: (jnp.minimum(i, n_ctx - 1), 0)),
                     pl.BlockSpec((tm, D), lambda i: (jnp.maximum(i - n_ctx, 0), 0))]
        out_shape = [jax.ShapeDtypeStruct((T_CTX, D), F32), jax.ShapeDtypeStruct((T_LAT, D), F32)]
    out = pl.pallas_call(
        functools.partial(_proj_resid_kernel, with_h=nxt is not None, cast_w=cast_w),
        grid=(T // tm,),
        in_specs=in_specs,
        out_specs=out_specs,
        out_shape=out_shape,
        scratch_shapes=[pltpu.VMEM((k, D), BF16)] if cast_w else [],
        compiler_params=_cp(1, VMEM_BIG),
        name=name,
    )(*args)
    return out


ATT_SCALE = HD ** -0.5
ATT_SCALE_LOG2 = ATT_SCALE * math.log2(math.e)


def _softmax_pv(s, v):
    m = jnp.max(s, axis=-1, keepdims=True) * ATT_SCALE_LOG2
    e = jnp.exp2(s * ATT_SCALE_LOG2 - m)
    o = jnp.dot(e.astype(BF16), v, preferred_element_type=F32)
    return o * (1.0 / jnp.sum(e, axis=-1, keepdims=True))


def _softmax_pv2(s1, v1, s2, v2):
    m = jnp.maximum(jnp.max(s1, axis=-1, keepdims=True), jnp.max(s2, axis=-1, keepdims=True)) * ATT_SCALE_LOG2
    e1 = jnp.exp2(s1 * ATT_SCALE_LOG2 - m)
    e2 = jnp.exp2(s2 * ATT_SCALE_LOG2 - m)
    tot = jnp.sum(e1, axis=-1, keepdims=True) + jnp.sum(e2, axis=-1, keepdims=True)
    o = jnp.dot(e1.astype(BF16), v1, preferred_element_type=F32) + jnp.dot(e2.astype(BF16), v2,
                                                                           preferred_element_type=F32)
    return o * (1.0 / tot)


def _ctx_attn_kernel(a_ref, c_ref, qn_ref, kn_ref, *rest):
    oa_ref, oc_ref, nk_ref, nv_ref, kcn_ref, vcn_ref = rest[-6:]
    nk_ref[...] = a_ref[:, C_KA:C_KA + NA_H * HD].astype(F32)
    nv_ref[...] = a_ref[:, C_VA:C_VA + NA_H * HD].astype(F32)
    for h in range(NA_H):
        q = a_ref[:, C_QA + h * HD:C_QA + (h + 1) * HD]
        k = a_ref[:, C_KA + h * HD:C_KA + (h + 1) * HD]
        v = a_ref[:, C_VA + h * HD:C_VA + (h + 1) * HD]
        s = lax.dot_general(q, k, NT_DIMS, preferred_element_type=F32)
        oa_ref[:, h * HD:(h + 1) * HD] = _softmax_pv(s, v).astype(BF16)
    kc0 = C_KC - C_QC
    vc0 = C_VC - C_QC
    vcn_ref[...] = c_ref[:, vc0:vc0 + GQA_KV * HD].astype(F32)
    for kv in range(GQA_KV):
        k = _rms(c_ref[:, kc0 + kv * HD:kc0 + (kv + 1) * HD].astype(F32), kn_ref[...])
        kcn_ref[:, kv * HD:(kv + 1) * HD] = k
        kb = k.astype(BF16)
        v = c_ref[:, vc0 + kv * HD:vc0 + (kv + 1) * HD]
        for g in range(GQA_G):
            h = kv * GQA_G + g
            q = _rms(c_ref[:, h * HD:(h + 1) * HD].astype(F32), qn_ref[...]).astype(BF16)
            s = lax.dot_general(q, kb, NT_DIMS, preferred_element_type=F32)
            oc_ref[:, h * HD:(h + 1) * HD] = _softmax_pv(s, v).astype(BF16)


def _ctx_attn(u, qn, kn, l, caches):
    wa = 3 * NA_H * HD
    wc = C_Z - C_QC
    rows = lambda w: pl.BlockSpec((L_CTX, w), lambda b: (b, 0))
    lay = lambda w: pl.BlockSpec((None, None, L_CTX, w), lambda b: (b, l, 0, 0))
    cache_shape = lambda w: jax.ShapeDtypeStruct((B_CTX, DEPTH, L_CTX, w), F32)
    in_specs = [
        pl.BlockSpec((L_CTX, wa), lambda b: (b, 0)),
        pl.BlockSpec((L_CTX, wc), lambda b: (b, C_QC // wc)),
        pl.BlockSpec((None, 1, HD), lambda b: (l, 0, 0)),
        pl.BlockSpec((None, 1, HD), lambda b: (l, 0, 0)),
    ]
    args = [u, u, qn.reshape(DEPTH, 1, HD), kn.reshape(DEPTH, 1, HD)]
    aliases = {}
    if caches is not None:
        in_specs += [pl.BlockSpec(memory_space=pl.ANY)] * 4
        args += list(caches)
        aliases = {4 + n: 2 + n for n in range(4)}
    out = pl.pallas_call(
        _ctx_attn_kernel,
        grid=(B_CTX,),
        in_specs=in_specs,
        out_specs=[rows(BRANCH_W), rows(BRANCH_W), lay(NA_H * HD), lay(NA_H * HD), lay(GQA_KV * HD),
                   lay(GQA_KV * HD)],
        out_shape=[
            jax.ShapeDtypeStruct((T, BRANCH_W), BF16),
            jax.ShapeDtypeStruct((T, BRANCH_W), BF16),
            cache_shape(NA_H * HD), cache_shape(NA_H * HD), cache_shape(GQA_KV * HD), cache_shape(GQA_KV * HD),
        ],
        input_output_aliases=aliases,
        compiler_params=_cp(1),
        name="ctx_attn",
    )(*args)
    return out[0], out[1], tuple(out[2:])


def _rope(x, cos, sin):
    lane = lax.broadcasted_iota(jnp.int32, x.shape, 1)
    first = (lane & (HD // 2 - 1)) < (HD // 4)
    rot = jnp.where(first, -pltpu.roll(x, HD - HD // 4, 1), pltpu.roll(x, HD // 4, 1))
    return x * cos + rot * sin


def _lat_gqa_kernel(q_ref, k_ref, vl_ref, ck_ref, cv_ref, qn_ref, kn_ref, cos_ref, sin_ref, oc_in_ref, o_ref, kr_ref,
                    ve_ref):
    del oc_in_ref
    i = pl.program_id(2)
    tq = q_ref.shape[0]

    @pl.when(i == 0)
    def _():
        k = _rms(k_ref[...].astype(F32), kn_ref[...])
        kr_ref[...] = _rope(k, cos_ref[...], sin_ref[...]).astype(BF16)
        ve_ref[:, :HD] = vl_ref[...]
        ve_ref[:, HD:] = jnp.ones((L_LAT, HD), BF16)

    start = pl.multiple_of(i * tq, tq)
    cos = cos_ref[pl.ds(start, tq), :]
    sin = sin_ref[pl.ds(start, tq), :]
    kl = kr_ref[...]
    ve = ve_ref[...]
    ck = ck_ref[...]
    cve = jnp.concatenate([cv_ref[...], jnp.ones((PAST, HD), BF16)], axis=1)
    for g in range(GQA_G):
        q = _rope(_rms(q_ref[:, g * HD:(g + 1) * HD].astype(F32), qn_ref[...]), cos, sin)
        q = (q * ATT_SCALE_LOG2).astype(BF16)
        s1 = lax.dot_general(q, kl, NT_DIMS, preferred_element_type=F32)
        s2 = lax.dot_general(q, ck, NT_DIMS, preferred_element_type=F32)
        m = jnp.maximum(jnp.max(s1, axis=-1, keepdims=True), jnp.max(s2, axis=-1, keepdims=True))
        e1 = jnp.exp2(s1 - m).astype(BF16)
        e2 = jnp.exp2(s2 - m).astype(BF16)
        oe = jnp.dot(e1, ve, preferred_element_type=F32) + jnp.dot(e2, cve, preferred_element_type=F32)
        o_ref[:, g * HD:(g + 1) * HD] = (oe[:, :HD] * (1.0 / oe[:, HD:HD + 1])).astype(BF16)


def _lat_gqa(u, ck, cv, qn, kn, cos_t, sin_t, oc, l, tq=512):
    nq = L_LAT // tq
    r0 = T_CTX // tq
    s0 = T_CTX // L_LAT
    wq = GQA_G * HD
    full = lambda b, kv, i: (0, 0)
    return pl.pallas_call(
        _lat_gqa_kernel,
        grid=(B_LAT, GQA_KV, nq),
        in_specs=[
            pl.BlockSpec((tq, wq), lambda b, kv, i: (r0 + b * nq + i, C_QC // wq + kv)),
            pl.BlockSpec((L_LAT, HD), lambda b, kv, i: (s0 + b, C_KC // HD + kv)),
            pl.BlockSpec((L_LAT, HD), lambda b, kv, i: (s0 + b, C_VC // HD + kv)),
            pl.BlockSpec((None, None, PAST, HD), lambda b, kv, i: (b, l, 0, kv)),
            pl.BlockSpec((None, None, PAST, HD), lambda b, kv, i: (b, l, 0, kv)),
            pl.BlockSpec((None, 1, HD), lambda b, kv, i: (l, 0, 0)),
            pl.BlockSpec((None, 1, HD), lambda b, kv, i: (l, 0, 0)),
            pl.BlockSpec((L_LAT, HD), full),
            pl.BlockSpec((L_LAT, HD), full),
            pl.BlockSpec(memory_space=pl.ANY),
        ],
        out_specs=pl.BlockSpec((tq, wq), lambda b, kv, i: (r0 + b * nq + i, kv)),
        out_shape=jax.ShapeDtypeStruct((T, GQA_H * HD), BF16),
        scratch_shapes=[pltpu.VMEM((L_LAT, HD), BF16), pltpu.VMEM((L_LAT, 2 * HD), BF16)],
        input_output_aliases={9: 0},
        compiler_params=_cp(3),
        name="lat_gqa",
    )(u, u, u, ck, cv, qn.reshape(DEPTH, 1, HD), kn.reshape(DEPTH, 1, HD), cos_t, sin_t, oc)


NA_QROWS = 4
NA_UROWS = 12
NA_TQ = NA_QROWS * GRID_W
NA_TK = NA_UROWS * GRID_W
NA_HP = 8


def _lat_na_kernel(q_ref, k_ref, v_ref, ck_ref, cv_ref, t2_ref, oa_in_ref, o_ref, bias_ref):
    del oa_in_ref
    j = pl.program_id(2)
    ustart = jnp.clip(NA_QROWS * j - NA_ROWS // 2, 0, GRID_R - NA_UROWS)
    left = lax.broadcasted_iota(jnp.int32, (GRID_W, 2 * GRID_W), 1) < GRID_W
    for a in range(NA_QROWS):
        r = NA_QROWS * j + a
        rs = jnp.clip(r - NA_ROWS // 2, 0, GRID_R - NA_ROWS)
        for m in range(NA_UROWS // 2):
            kr0 = ustart + 2 * m
            v0 = ((kr0 >= rs) & (kr0 < rs + NA_ROWS)).astype(jnp.int32)
            v1 = ((kr0 + 1 >= rs) & (kr0 + 1 < rs + NA_ROWS)).astype(jnp.int32)
            idx = jnp.clip(kr0 - r + NA_ROWS, 0, 2 * NA_ROWS - 1)
            ok = jnp.where(left, v0, v1) > 0
            for hh in range(NA_HP):
                bias_ref[hh, a * GRID_W:(a + 1) * GRID_W, m * 2 * GRID_W:(m + 1) * 2 * GRID_W] = jnp.where(
                    ok, t2_ref[hh, idx], NEG)
    start = pl.multiple_of(ustart * GRID_W, GRID_W)
    for hh in range(NA_HP):
        cs = slice(hh * HD, (hh + 1) * HD)
        q = q_ref[:, cs]
        kl = k_ref[pl.ds(start, NA_TK), cs]
        vl = v_ref[pl.ds(start, NA_TK), cs]
        s_loc = lax.dot_general(q, kl, NT_DIMS, preferred_element_type=F32) + bias_ref[hh]
        s_ctx = lax.dot_general(q, ck_ref[:, cs], NT_DIMS, preferred_element_type=F32)
        o_ref[:, cs] = _softmax_pv2(s_loc, vl, s_ctx, cv_ref[:, cs]).astype(BF16)


def _lat_na(u, ck, cv, t2, oa, l):
    nj = GRID_R // NA_QROWS
    r0 = T_CTX // NA_TQ
    s0 = T_CTX // L_LAT
    w = NA_HP * HD
    return pl.pallas_call(
        _lat_na_kernel,
        grid=(B_LAT, NA_H // NA_HP, nj),
        in_specs=[
            pl.BlockSpec((NA_TQ, w), lambda b, h, j: (r0 + b * nj + j, C_QA // w + h)),
            pl.BlockSpec((L_LAT, w), lambda b, h, j: (s0 + b, C_KA // w + h)),
            pl.BlockSpec((L_LAT, w), lambda b, h, j: (s0 + b, C_VA // w + h)),
            pl.BlockSpec((None, None, PAST, w), lambda b, h, j: (b, l, 0, h)),
            pl.BlockSpec((None, None, PAST, w), lambda b, h, j: (b, l, 0, h)),
            pl.BlockSpec((None, NA_HP, 2 * NA_ROWS, GRID_W, 2 * GRID_W), lambda b, h, j: (l, h, 0, 0, 0)),
            pl.BlockSpec(memory_space=pl.ANY),
        ],
        out_specs=pl.BlockSpec((NA_TQ, w), lambda b, h, j: (r0 + b * nj + j, h)),
        out_shape=jax.ShapeDtypeStruct((T, NA_H * HD), BF16),
        scratch_shapes=[pltpu.VMEM((NA_HP, NA_TQ, NA_TK), F32)],
        input_output_aliases={6: 0},
        compiler_params=_cp(3),
        name="lat_na",
    )(u, u, u, ck, cv, t2, oa)


def _na_bias_tiles(rpb):
    w = jnp.arange(GRID_W)
    cs = jnp.clip(w - NA_COLS // 2, 0, GRID_W - NA_COLS)
    cmask = (w[None, :] >= cs[:, None]) & (w[None, :] < cs[:, None] + NA_COLS)
    dc = jnp.clip(w[None, :] - w[:, None] + NA_COLS - 1, 0, 2 * NA_COLS - 2)
    tm = jnp.where(cmask, rpb[:, :, :, dc] / ATT_SCALE, NEG)
    pad = jnp.full(tm.shape[:2] + (1,) + tm.shape[3:], NEG, F32)
    lo = jnp.concatenate([pad, tm], axis=2)
    hi = jnp.concatenate([tm, pad], axis=2)
    return jnp.concatenate([lo, hi], axis=-1).astype(F32)


def _rope_tables():
    nf = HD // 4
    t = jnp.arange(L_LAT)
    inv = ROPE_BASE ** (-jnp.arange(nf, dtype=F32) / nf)
    ar = (t // GRID_W).astype(F32)[:, None] * inv[None, :]
    ac = (t % GRID_W).astype(F32)[:, None] * inv[None, :]
    cos_t = jnp.concatenate([jnp.cos(ar), jnp.cos(ar), jnp.cos(ac), jnp.cos(ac)], axis=-1)
    sin_t = jnp.concatenate([jnp.sin(ar), jnp.sin(ar), jnp.sin(ac), jnp.sin(ac)], axis=-1)
    return cos_t, sin_t


def _conv_kernel(x_ref, w_ref, b_ref, o_ref):
    x = x_ref[...].astype(F32)
    n = x.shape[0]
    row = lax.broadcasted_iota(jnp.int32, (SUBLANES, x.shape[1]), 0)
    xm = pltpu.roll(x, 1, 0)
    xm = jnp.concatenate([jnp.where(row == 0, 0.0, xm[:SUBLANES]), xm[SUBLANES:]], axis=0)
    xp = pltpu.roll(x, n - 1, 0)
    xp = jnp.concatenate([xp[:n - SUBLANES], jnp.where(row == SUBLANES - 1, 0.0, xp[n - SUBLANES:])], axis=0)
    y = w_ref[0:1, :] * xm + w_ref[1:2, :] * x + w_ref[2:3, :] * xp + b_ref[...]
    hy = 0.5 * y
    o_ref[...] = (hy * (1.0 + jnp.tanh(hy))).astype(o_ref.dtype)


def _conv(u, conv_w, conv_b, l, nseq, seq_len, row0):
    tc = 512
    nc = CONV_CH // tc
    rb0 = row0 // seq_len
    cb0 = C_XBC // tc
    return pl.pallas_call(
        _conv_kernel,
        grid=(nseq, nc),
        in_specs=[
            pl.BlockSpec((seq_len, tc), lambda b, j: (rb0 + b, cb0 + j)),
            pl.BlockSpec((None, 3, tc), lambda b, j: (l, 0, j)),
            pl.BlockSpec((None, 1, tc), lambda b, j: (l, 0, j)),
        ],
        out_specs=pl.BlockSpec((seq_len, tc), lambda b, j: (b, j)),
        out_shape=jax.ShapeDtypeStruct((nseq * seq_len, CONV_CH), F32),
        compiler_params=_cp(2),
        name="conv",
    )(u, conv_w, conv_b.reshape(DEPTH, 1, CONV_CH))


def _softplus(x):
    return jnp.maximum(x, 0.0) + jnp.log1p(jnp.exp(-jnp.abs(x)))


def _expand_heads(v, expand, terms):
    out = None
    rem = v
    for _ in range(terms):
        piece = rem.astype(BF16)
        rem = rem - piece.astype(F32)
        t = jnp.dot(piece, expand, preferred_element_type=F32)
        out = t if out is None else out + t
    return out


def _ssd_chunk(xbc, dt_col, dt_row, a_row, a_col, h_ref, reverse):
    x = xbc[:, :SSD_INNER]
    bm = xbc[:, SSD_INNER:SSD_INNER + SSD_G * SSD_N]
    cm = xbc[:, SSD_INNER + SSD_G * SSD_N:]
    li = lax.broadcasted_iota(jnp.int32, (CHUNK, CHUNK), 0)
    si = lax.broadcasted_iota(jnp.int32, (CHUNK, CHUNK), 1)
    mask = (si >= li) if reverse else (si <= li)
    tri = mask.astype(F32)
    tri_t = ((li >= si) if reverse else (li <= si)).astype(F32)
    cum_col = jnp.dot(tri, dt_col * a_row, precision=HI, preferred_element_type=F32)
    cum_row = jnp.dot(dt_row * a_col, tri_t, precision=HI, preferred_element_type=F32)
    tot_row = cum_col[0:1, :] if reverse else cum_col[CHUNK - 1:CHUNK, :]
    col = lax.broadcasted_iota(jnp.int32, (SSD_H, SSD_INNER), 1)
    lo_col = lax.broadcasted_iota(jnp.int32, (SSD_H, SSD_INNER), 0) * SSD_P
    expand = ((col >= lo_col) & (col < lo_col + SSD_P)).astype(BF16)
    w_exp = _expand_heads(jnp.exp(tot_row - cum_col) * dt_col, expand, 1)
    e_exp = _expand_heads(jnp.exp(cum_col), expand, 1)
    dec = _expand_heads(jnp.broadcast_to(jnp.exp(tot_row), (8, SSD_H)), expand, 3)[0:1, :]
    hpg = SSD_H // SSD_G
    xb = x.astype(BF16)
    bgs = [bm[:, g * SSD_N:(g + 1) * SSD_N].astype(BF16) for g in range(SSD_G)]
    cgs = [cm[:, g * SSD_N:(g + 1) * SSD_N].astype(BF16) for g in range(SSD_G)]
    gmat = [lax.dot_general(cgs[g], bgs[g], NT_DIMS, preferred_element_type=F32) for g in range(SSD_G)]
    lo = lax.broadcasted_iota(jnp.int32, (CHUNK, 2 * SSD_P), 1) < SSD_P
    zero = jnp.zeros((CHUNK, 2 * SSD_P), BF16)
    ys = []
    for pr in range(SSD_H // 2):
        g = (2 * pr) // hpg
        ms = []
        for h in (2 * pr, 2 * pr + 1):
            seg = cum_col[:, h:h + 1] - cum_row[h:h + 1, :]
            lm = jnp.exp(jnp.where(mask, seg, NEG))
            ms.append((gmat[g] * lm * dt_row[h:h + 1, :]).astype(BF16))
        xp = xb[:, pr * 2 * SSD_P:(pr + 1) * 2 * SSD_P]
        rhs = jnp.concatenate([jnp.where(lo, xp, zero), jnp.where(lo, zero, xp)], axis=0)
        ys.append(jnp.dot(jnp.concatenate(ms, axis=1), rhs, preferred_element_type=F32))
    y = jnp.concatenate(ys, axis=1)
    h_t = h_ref[...]
    hb = h_t.astype(BF16)
    xw = (x * w_exp).astype(BF16)
    wpg = SSD_INNER // SSD_G
    yo = [jnp.dot(cgs[g], hb[:, g * wpg:(g + 1) * wpg], preferred_element_type=F32) for g in range(SSD_G)]
    st = [lax.dot_general(bgs[g], xw[:, g * wpg:(g + 1) * wpg], TN_DIMS, preferred_element_type=F32)
          for g in range(SSD_G)]
    h_ref[...] = h_t * dec + jnp.concatenate(st, axis=1)
    return y + jnp.concatenate(yo, axis=1) * e_exp


def _ssd_kernel(xf_ref, xb_ref, dtf_ref, dtb_ref, dttf_ref, dttb_ref, prow_ref, pcol_ref, dtot_ref, *rest,
                context):
    hf_ref, hb_ref = rest[-2:]
    if context:
        yf_ref, yb_ref, hfin_ref = rest[-5:-2]
    else:
        h0_ref, yf_ref, yb_ref = rest[-5:-2]
    c = pl.program_id(1)

    @pl.when(c == 0)
    def _():
        if context:
            hf_ref[...] = jnp.zeros_like(hf_ref)
            hb_ref[...] = jnp.zeros_like(hb_ref)
        else:
            hf_ref[...] = h0_ref[0].T
            hb_ref[...] = h0_ref[1].T

    prow = prow_ref[...]
    pcol = pcol_ref[...]
    xf = xf_ref[...]
    dt_col = _softplus(dtf_ref[:, 0:SSD_H] + prow[0:1, :])
    dt_row = _softplus(dttf_ref[0:SSD_H, :] + pcol[:, 0:1])
    yf = _ssd_chunk(xf, dt_col, dt_row, -jnp.exp(prow[2:3, :]), -jnp.exp(pcol[:, 2:3]), hf_ref, False)
    yf_ref[...] = (yf + dtot_ref[...] * xf[:, :SSD_INNER]).astype(yf_ref.dtype)
    dt_col = _softplus(dtb_ref[:, SSD_H:2 * SSD_H] + prow[1:2, :])
    dt_row = _softplus(dttb_ref[SSD_H:2 * SSD_H, :] + pcol[:, 1:2])
    yb_ref[...] = _ssd_chunk(xb_ref[...], dt_col, dt_row, -jnp.exp(prow[3:4, :]), -jnp.exp(pcol[:, 3:4]),
                             hb_ref, True).astype(yb_ref.dtype)

    if context:
        @pl.when(c == pl.num_programs(1) - 1)
        def _():
            hfin_ref[0] = hf_ref[...].T
            hfin_ref[1] = hb_ref[...].T


def _ssd(xbc, u_dt, dt_t, prow, pcol, dtot, l, nseq, seq_len, row0, states, context):
    nc = seq_len // CHUNK
    cb0 = row0 // CHUNK
    rows = nseq * seq_len
    nd = 2 * SSD_H
    fwd = lambda b, c: b * nc + c
    bwd = lambda b, c: b * nc + (nc - 1 - c)
    state_spec = pl.BlockSpec((None, None, 2, SSD_INNER, SSD_N), lambda b, c: (b, l, 0, 0, 0))
    in_specs = [
        pl.BlockSpec((CHUNK, CONV_CH), lambda b, c: (fwd(b, c), 0)),
        pl.BlockSpec((CHUNK, CONV_CH), lambda b, c: (bwd(b, c), 0)),
        pl.BlockSpec((CHUNK, DT_PAD), lambda b, c: (cb0 + fwd(b, c), 0)),
        pl.BlockSpec((CHUNK, DT_PAD), lambda b, c: (cb0 + bwd(b, c), 0)),
        pl.BlockSpec((DT_PAD, CHUNK), lambda b, c: (0, cb0 + fwd(b, c))),
        pl.BlockSpec((DT_PAD, CHUNK), lambda b, c: (0, cb0 + bwd(b, c))),
        pl.BlockSpec((4, SSD_H), lambda b, c: (0, 0)),
        pl.BlockSpec((SSD_H, 4), lambda b, c: (0, 0)),
        pl.BlockSpec((1, SSD_INNER), lambda b, c: (0, 0)),
    ]
    args = [xbc, xbc, u_dt, u_dt, dt_t, dt_t, prow, pcol, dtot]
    out_specs = [
        pl.BlockSpec((CHUNK, SSD_INNER), lambda b, c: (fwd(b, c), 0)),
        pl.BlockSpec((CHUNK, SSD_INNER), lambda b, c: (bwd(b, c), 0)),
    ]
    out_shape = [jax.ShapeDtypeStruct((rows, SSD_INNER), BF16), jax.ShapeDtypeStruct((rows, SSD_INNER), BF16)]
    aliases = {}
    if context:
        out_specs.append(state_spec)
        out_shape.append(jax.ShapeDtypeStruct((nseq, DEPTH, 2, SSD_INNER, SSD_N), F32))
        if states is not None:
            in_specs.append(pl.BlockSpec(memory_space=pl.ANY))
            args.append(states)
            aliases = {len(args) - 1: 2}
    else:
        in_specs.append(state_spec)
        args.append(states)
    return pl.pallas_call(
        functools.partial(_ssd_kernel, context=context),
        grid=(nseq, nc),
        in_specs=in_specs,
        out_specs=out_specs,
        out_shape=out_shape,
        scratch_shapes=[pltpu.VMEM((SSD_N, SSD_INNER), F32), pltpu.VMEM((SSD_N, SSD_INNER), F32)],
        input_output_aliases=aliases,
        compiler_params=_cp(2),
        name="ssd",
    )(*args)


def _ssd_gate_kernel(yf_ref, yb_ref, z_ref, g_ref, *rest):
    o_ref = rest[-1]
    y = (yf_ref[...].astype(F32) + yb_ref[...].astype(F32)) * _silu(z_ref[...].astype(F32))
    o_ref[...] = _rms(y, g_ref[...]).astype(BF16)


def _ssd_gate(yf, yb, u, norm_w, l, row0, ob=None):
    tm = ROW_TILE
    rows = yf.shape[0]
    rb0 = row0 // tm
    row = pl.BlockSpec((tm, SSD_INNER), lambda i: (i, 0))
    in_specs = [
        row, row,
        pl.BlockSpec((pl.Element(tm), pl.Element(SSD_INNER)), lambda i: (tm * (rb0 + i), C_Z)),
        pl.BlockSpec((None, 1, SSD_INNER), lambda i: (l, 0, 0)),
    ]
    args = [yf, yb, u, norm_w.reshape(DEPTH, 1, SSD_INNER)]
    aliases = {}
    if ob is not None:
        in_specs.append(pl.BlockSpec(memory_space=pl.ANY))
        args.append(ob)
        aliases = {4: 0}
    return pl.pallas_call(
        _ssd_gate_kernel,
        grid=(rows // tm,),
        in_specs=in_specs,
        out_specs=pl.BlockSpec((tm, SSD_INNER), lambda i: (rb0 + i, 0)),
        out_shape=jax.ShapeDtypeStruct((T, SSD_INNER), BF16),
        input_output_aliases=aliases,
        compiler_params=_cp(1),
        name="ssd_gate",
    )(*args)


def kernel(x_prompt, x_sample, cache_na_k, cache_na_v, cache_gqa_k, cache_gqa_v, state_ssd, c, c_ctx, ada_w, ada_b, norm_mix_pre, norm_mix_post, norm_ffn_pre, norm_ffn_post, w_in, gate_b, na_rpb, gqa_q_norm, gqa_k_norm, ssd_conv_w, ssd_conv_b, ssd_dt_bias, ssd_a_log, ssd_d, ssd_norm_w, w_branch, w_out, ffn_w_up, ffn_w_down):
    cvec = jnp.concatenate([c_ctx[None, :], c, jnp.zeros((8 - 1 - B_LAT, D), F32)], axis=0)
    mods = _compute_mods(cvec, ada_w, ada_b).reshape(DEPTH, 8, 6, 1, D)
    wt_in = jnp.swapaxes(w_in, 1, 2)
    w_down = ffn_w_down.astype(BF16)
    t2 = _na_bias_tiles(na_rpb)
    cos_t, sin_t = _rope_tables()
    ck_a = cache_na_k.reshape(B_LAT, DEPTH, PAST, NA_H * HD).astype(BF16)
    cv_a = cache_na_v.reshape(B_LAT, DEPTH, PAST, NA_H * HD).astype(BF16)
    ck_c = cache_gqa_k.reshape(B_LAT, DEPTH, PAST, GQA_KV * HD).astype(BF16)
    cv_c = cache_gqa_v.reshape(B_LAT, DEPTH, PAST, GQA_KV * HD).astype(BF16)
    h0_lat = state_ssd.reshape(B_LAT, DEPTH, 2, SSD_INNER, SSD_N)
    prow = jnp.concatenate([ssd_dt_bias, ssd_a_log], axis=1)
    pcol = prow.transpose(0, 2, 1)
    dtot = jnp.repeat(ssd_d[:, 0] + ssd_d[:, 1], SSD_P, axis=-1)[:, None, :]

    x, h = _normmod(x_prompt.reshape(T_CTX, D), x_sample.reshape(T_LAT, D), norm_mix_pre, mods, 0, 0, 1)
    caches = None
    ssd_fin = None
    for l in range(DEPTH):
        u, u_dt, dt_t = _in_proj(h, wt_in, l)
        oa, oc, caches = _ctx_attn(u, gqa_q_norm, gqa_k_norm, l, caches)
        oc = _lat_gqa(u, ck_c, cv_c, gqa_q_norm, gqa_k_norm, cos_t, sin_t, oc, l)
        oa = _lat_na(u, ck_a, cv_a, t2, oa, l)
        xbc_c = _conv(u, ssd_conv_w, ssd_conv_b, l, B_CTX, L_CTX, 0)
        xbc_l = _conv(u, ssd_conv_w, ssd_conv_b, l, B_LAT, L_LAT, T_CTX)
        yf_c, yb_c, ssd_fin = _ssd(xbc_c, u_dt, dt_t, prow[l], pcol[l], dtot[l], l, B_CTX, L_CTX, 0, ssd_fin, True)
        yf_l, yb_l = _ssd(xbc_l, u_dt, dt_t, prow[l], pcol[l], dtot[l], l, B_LAT, L_LAT, T_CTX, h0_lat, False)
        ob = _ssd_gate(yf_c, yb_c, u, ssd_norm_w, l, 0)
        ob = _ssd_gate(yf_l, yb_l, u, ssd_norm_w, l, T_CTX, ob)
        mix = _merge(h, oa, ob, oc, wt_in, w_branch, gate_b, l)
        x, h2 = _proj_resid(mix, w_out, x, norm_mix_post, mods, l, 2, 512, (norm_ffn_pre, l, 3, 4), "out_proj")
        act = _matmul_swiglu(h2, ffn_w_up, l)
        nxt = (norm_mix_pre, l + 1, 0, 1) if l + 1 < DEPTH else None
        x, h = _proj_resid(act, w_down, x, norm_ffn_post, mods, l, 5, 256, nxt, "ffn_down")
    na_k, na_v, gqa_k, gqa_v = caches
    return (x.reshape(B_CTX, L_CTX, D), h.reshape(B_LAT, L_LAT, D),
            na_k.reshape(B_CTX, DEPTH, L_CTX, NA_H, HD), na_v.reshape(B_CTX, DEPTH, L_CTX, NA_H, HD),
            gqa_k.reshape(B_CTX, DEPTH, L_CTX, GQA_KV, HD), gqa_v.reshape(B_CTX, DEPTH, L_CTX, GQA_KV, HD),
            ssd_fin.reshape(B_CTX, DEPTH, 2, SSD_H, SSD_P, SSD_N))
```

```python
import functools
import math

import jax
import jax.numpy as jnp
from jax import lax
from jax.experimental import pallas as pl
from jax.experimental.pallas import tpu as pltpu

F32 = jnp.float32
BF16 = jnp.bfloat16
HI = lax.Precision.HIGHEST
EPS = 1e-6
NEG = -1e30

D = 2048
DEPTH = 4
B_CTX, L_CTX = 16, 256
B_LAT, L_LAT = 4, 2048
PAST = 512
GRID_W = 64
GRID_R = L_LAT // GRID_W
NA_H = 8
HD = 128
NA_ROWS, NA_COLS = 8, 16
GQA_H, GQA_KV = 8, 2
GQA_G = GQA_H // GQA_KV
ROPE_BASE = 10000.0
SSD_INNER = 1024
SSD_P = 64
SSD_H = SSD_INNER // SSD_P
SSD_N = 128
SSD_G = 2
CHUNK = 128
CONV_CH = SSD_INNER + 2 * SSD_G * SSD_N
BRANCH_W = 1024
FFN_H = 5632
T_CTX = B_CTX * L_CTX
T_LAT = B_LAT * L_LAT
T = T_CTX + T_LAT

C_QA, C_KA, C_VA = 0, NA_H * HD, 2 * NA_H * HD
C_QC = 3 * NA_H * HD
C_KC = C_QC + GQA_H * HD
C_VC = C_KC + GQA_KV * HD
C_Z = C_VC + GQA_KV * HD
C_XBC = C_Z + SSD_INNER
C_DT = C_XBC + CONV_CH
C_GL = C_DT + 2 * SSD_H
N_MAIN = C_DT

NT_DIMS = (((1,), (1,)), ((), ()))
TN_DIMS = (((0,), (0,)), ((), ()))

ROW_TILE = 256
SUBLANES = 8
VMEM_BIG = 56


def _cp(n_axes, vmem_mb=None):
    kw = dict(dimension_semantics=("arbitrary",) * n_axes)
    if vmem_mb is not None:
        kw["vmem_limit_bytes"] = vmem_mb * 1024 * 1024
    return pltpu.CompilerParams(**kw)


def _mod_row(i, tm):
    n_ctx = T_CTX // tm
    n_lat = L_LAT // tm
    return jnp.where(i < n_ctx, 0, 1 + (i - n_ctx) // n_lat)


def _rms(x, g):
    return x * lax.rsqrt(jnp.mean(x * x, axis=-1, keepdims=True) + EPS) * g


def _silu(x):
    return x * jax.nn.sigmoid(x)


def _mods_kernel(c_ref, w_ref, b_ref, o_ref):
    s = _silu(c_ref[...]).astype(BF16)
    o_ref[...] = jnp.dot(s, w_ref[...].astype(BF16), preferred_element_type=F32) + b_ref[...]


def _compute_mods(cvec, ada_w, ada_b):
    tn = 1024
    n = 6 * D
    return pl.pallas_call(
        _mods_kernel,
        grid=(DEPTH, n // tn),
        in_specs=[
            pl.BlockSpec((8, D), lambda l, j: (0, 0)),
            pl.BlockSpec((None, D, tn), lambda l, j: (l, 0, j)),
            pl.BlockSpec((None, 1, tn), lambda l, j: (l, 0, j)),
        ],
        out_specs=pl.BlockSpec((None, 8, tn), lambda l, j: (l, 0, j)),
        out_shape=jax.ShapeDtypeStruct((DEPTH, 8, n), F32),
        compiler_params=_cp(2),
        name="mods",
    )(cvec, ada_w, ada_b.reshape(DEPTH, 1, n))


def _mod_spec(l, k, tm):
    return pl.BlockSpec((None, None, None, 1, D), lambda i: (l, _mod_row(i, tm), k, 0, 0))


def _gain_spec(l):
    return pl.BlockSpec((None, 1, D), lambda i: (l, 0, 0))


def _normmod_kernel(xc_ref, xl_ref, g_ref, sh_ref, sc_ref, x_ref, h_ref):
    x = jnp.where(pl.program_id(0) < T_CTX // ROW_TILE, xc_ref[...], xl_ref[...])
    x_ref[...] = x
    h_ref[...] = (_rms(x, g_ref[...]) * (1.0 + sc_ref[...]) + sh_ref[...]).astype(BF16)


def _normmod(x_ctx, x_lat, gains, mods, l, k_sh, k_sc):
    tm = ROW_TILE
    n_ctx = T_CTX // tm
    row = pl.BlockSpec((tm, D), lambda i: (i, 0))
    return pl.pallas_call(
        _normmod_kernel,
        grid=(T // tm,),
        in_specs=[
            pl.BlockSpec((tm, D), lambda i: (jnp.minimum(i, n_ctx - 1), 0)),
            pl.BlockSpec((tm, D), lambda i: (jnp.maximum(i - n_ctx, 0), 0)),
            _gain_spec(l),
            _mod_spec(l, k_sh, tm),
            _mod_spec(l, k_sc, tm),
        ],
        out_specs=[row, row],
        out_shape=[jax.ShapeDtypeStruct((T, D), F32), jax.ShapeDtypeStruct((T, D), BF16)],
        compiler_params=_cp(1),
        name="normmod",
    )(x_ctx, x_lat, gains.reshape(DEPTH, 1, D), mods, mods)


DT_PAD = 128


def _in_proj_kernel(x_ref, wt_ref, wdt_ref, o_ref, odt_ref, odtt_ref, wb_ref):
    @pl.when(pl.program_id(1) == 0)
    def _():
        wb_ref[...] = wt_ref[...].astype(BF16).T

    x = x_ref[...]
    o_ref[...] = jnp.dot(x, wb_ref[...], preferred_element_type=F32).astype(o_ref.dtype)

    @pl.when(pl.program_id(0) == pl.num_programs(0) - 1)
    def _():
        dt = lax.dot_general(x, wdt_ref[...].astype(BF16), NT_DIMS, preferred_element_type=F32)
        odt_ref[...] = dt
        odtt_ref[...] = dt.T


def _in_proj(x, wt, l, tn=1024, tm=1024):
    m, k = x.shape
    nj = N_MAIN // tn
    last = lambda j, i: jnp.where(j == nj - 1, i, 0)
    return pl.pallas_call(
        _in_proj_kernel,
        grid=(nj, m // tm),
        in_specs=[
            pl.BlockSpec((tm, k), lambda j, i: (i, 0)),
            pl.BlockSpec((None, tn, k), lambda j, i: (l, j, 0)),
            pl.BlockSpec((None, DT_PAD, k), lambda j, i: (l, C_DT // DT_PAD, 0)),
        ],
        out_specs=[
            pl.BlockSpec((tm, tn), lambda j, i: (i, j)),
            pl.BlockSpec((tm, DT_PAD), lambda j, i: (last(j, i), 0)),
            pl.BlockSpec((DT_PAD, tm), lambda j, i: (0, last(j, i))),
        ],
        out_shape=[
            jax.ShapeDtypeStruct((m, N_MAIN), BF16),
            jax.ShapeDtypeStruct((m, DT_PAD), F32),
            jax.ShapeDtypeStruct((DT_PAD, m), F32),
        ],
        scratch_shapes=[pltpu.VMEM((k, tn), BF16)],
        compiler_params=_cp(2, 40),
        name="in_proj",
    )(x, wt, wt)


WD_COLS = 256


def _mm_swiglu_kernel(x_ref, wg_ref, wu_ref, wd_ref, o_ref, owd_ref, wgb_ref, wub_ref):
    @pl.when(pl.program_id(1) == 0)
    def _():
        wgb_ref[...] = wg_ref[...].astype(BF16)
        wub_ref[...] = wu_ref[...].astype(BF16)

    x = x_ref[...]
    g = jnp.dot(x, wgb_ref[...], preferred_element_type=F32)
    u = jnp.dot(x, wub_ref[...], preferred_element_type=F32)
    o_ref[...] = (_silu(g) * u).astype(o_ref.dtype)

    @pl.when(pl.program_id(1) < D // WD_COLS)
    def _():
        owd_ref[...] = wd_ref[...].astype(BF16)


def _matmul_swiglu(x, w, w_down, l, tn=512, tm=1024):
    m, k = x.shape
    nb = FFN_H // tn
    ncb = D // WD_COLS
    assert m // tm >= ncb
    wd_idx = lambda j, i: (j, jnp.minimum(i, ncb - 1))
    return pl.pallas_call(
        _mm_swiglu_kernel,
        grid=(nb, m // tm),
        in_specs=[
            pl.BlockSpec((tm, k), lambda j, i: (i, 0)),
            pl.BlockSpec((None, k, tn), lambda j, i: (l, 0, j)),
            pl.BlockSpec((None, k, tn), lambda j, i: (l, 0, nb + j)),
            pl.BlockSpec((None, tn, WD_COLS), lambda j, i: (l,) + wd_idx(j, i)),
        ],
        out_specs=[
            pl.BlockSpec((tm, tn), lambda j, i: (i, j)),
            pl.BlockSpec((None, tn, WD_COLS), lambda j, i: (0,) + wd_idx(j, i)),
        ],
        out_shape=[jax.ShapeDtypeStruct((m, FFN_H), BF16), jax.ShapeDtypeStruct((1, FFN_H, D), BF16)],
        scratch_shapes=[pltpu.VMEM((k, tn), BF16), pltpu.VMEM((k, tn), BF16)],
        compiler_params=_cp(2, 40),
        name="ffn_up",
    )(x, w, w, w_down)


def _merge_kernel(h_ref, oa_ref, ob_ref, oc_ref, wg0, wg1, wg2, wb0, wb1, wb2, gb0, gb1, gb2, o_ref, wgb_ref, wbb_ref):
    wg = (wg0, wg1, wg2)
    wb = (wb0, wb1, wb2)
    gb = (gb0, gb1, gb2)

    @pl.when(pl.program_id(1) == 0)
    def _():
        for g in range(3):
            wgb_ref[g] = wg[g][0].astype(BF16).T
            wbb_ref[g] = wb[g][...].astype(BF16)

    h = h_ref[...]
    acc = None
    for g, o_r in enumerate((oa_ref, ob_ref, oc_ref)):
        gl = jnp.dot(h, wgb_ref[g], preferred_element_type=F32) + gb[g][...]
        p = jnp.dot(o_r[...], wbb_ref[g], preferred_element_type=F32)
        t = jax.nn.sigmoid(gl) * p
        acc = t if acc is None else acc + t
    o_ref[...] = acc.astype(BF16)


def _merge(h, oa, ob, oc, wt_in, w_branch, gate_b, l, tn=512, tm=512):
    nb = D // tn
    xrow = pl.BlockSpec((tm, D), lambda j, i: (i, 0))
    orow = pl.BlockSpec((tm, BRANCH_W), lambda j, i: (i, 0))
    once = pl.Buffered(1)
    assert C_GL % SUBLANES == 0 and tn % SUBLANES == 0

    def wg_spec(g):
        row0 = (C_GL + g * D) // SUBLANES
        return pl.BlockSpec((pl.Element(1), pl.Element(tn), pl.Element(D)),
                            lambda j, i: (l, SUBLANES * (row0 + j * (tn // SUBLANES)), 0), pipeline_mode=once)

    def wb_spec(g):
        return pl.BlockSpec((None, None, BRANCH_W, tn), lambda j, i: (l, g, 0, j), pipeline_mode=once)

    def gb_spec(g):
        return pl.BlockSpec((None, 1, tn), lambda j, i: (l, 0, g * nb + j))

    gate_b3 = gate_b.reshape(DEPTH, 1, 3 * D)
    return pl.pallas_call(
        _merge_kernel,
        grid=(nb, T // tm),
        in_specs=[xrow, orow, orow, orow]
        + [wg_spec(g) for g in range(3)]
        + [wb_spec(g) for g in range(3)]
        + [gb_spec(g) for g in range(3)],
        out_specs=pl.BlockSpec((tm, tn), lambda j, i: (i, j)),
        out_shape=jax.ShapeDtypeStruct((T, D), BF16),
        scratch_shapes=[pltpu.VMEM((3, D, tn), BF16), pltpu.VMEM((3, BRANCH_W, tn), BF16)],
        compiler_params=_cp(2, 48),
        name="merge",
    )(h, oa, ob, oc, wt_in, wt_in, wt_in, w_branch, w_branch, w_branch, gate_b3, gate_b3, gate_b3)


def _proj_resid_kernel(a_ref, w_ref, x_ref, gpost_ref, gate_ref, *rest, with_h, cast_w):
    if cast_w:
        wb_ref = rest[-1]
        rest = rest[:-1]

        @pl.when(pl.program_id(0) == 0)
        def _():
            wb_ref[...] = w_ref[...].astype(BF16)

        w = wb_ref[...]
    else:
        w = w_ref[...]
    y = jnp.dot(a_ref[...], w, preferred_element_type=F32)
    xn = x_ref[...] + gate_ref[...] * _rms(y, gpost_ref[...])
    if with_h:
        gpre_ref, sh_ref, sc_ref, xo_ref, h_ref = rest
        xo_ref[...] = xn
        h_ref[...] = (_rms(xn, gpre_ref[...]) * (1.0 + sc_ref[...]) + sh_ref[...]).astype(BF16)
    else:
        oc_ref, ol_ref = rest
        n_ctx = T_CTX // xn.shape[0]

        @pl.when(pl.program_id(0) < n_ctx)
        def _():
            oc_ref[...] = xn

        @pl.when(pl.program_id(0) >= n_ctx)
        def _():
            ol_ref[...] = xn


def _proj_resid(a, w, wl, x, post_gains, mods, l, k_gate, tm, nxt, name):
    k = a.shape[1]
    cast_w = w.dtype != BF16
    row = pl.BlockSpec((tm, D), lambda i: (i, 0))
    in_specs = [
        pl.BlockSpec((tm, k), lambda i: (i, 0)),
        pl.BlockSpec((None, k, D), lambda i: (wl, 0, 0), pipeline_mode=pl.Buffered(1)),
        row, _gain_spec(l), _mod_spec(l, k_gate, tm),
    ]
    args = [a, w, x, post_gains.reshape(DEPTH, 1, D), mods]
    out_specs = [row]
    out_shape = [jax.ShapeDtypeStruct((T, D), F32)]
    if nxt is not None:
        gains, ln, k_sh, k_sc = nxt
        in_specs += [_gain_spec(ln), _mod_spec(ln, k_sh, tm), _mod_spec(ln, k_sc, tm)]
        args += [gains.reshape(DEPTH, 1, D), mods, mods]
        out_specs.append(row)
        out_shape.append(jax.ShapeDtypeStruct((T, D), BF16))
    else:
        n_ctx = T_CTX // tm
        out_specs = [pl.BlockSpec((tm, D), lambda i: (jnp.minimum(i, n_ctx - 1), 0)),
                     pl.BlockSpec((tm, D), lambda i: (jnp.maximum(i - n_ctx, 0), 0))]
        out_shape = [jax.ShapeDtypeStruct((T_CTX, D), F32), jax.ShapeDtypeStruct((T_LAT, D), F32)]
    out = pl.pallas_call(
        functools.partial(_proj_resid_kernel, with_h=nxt is not None, cast_w=cast_w),
        grid=(T // tm,),
        in_specs=in_specs,
        out_specs=out_specs,
        out_shape=out_shape,
        scratch_shapes=[pltpu.VMEM((k, D), BF16)] if cast_w else [],
        compiler_params=_cp(1, VMEM_BIG),
        name=name,
    )(*args)
    return out


ATT_SCALE = HD ** -0.5
ATT_SCALE_LOG2 = ATT_SCALE * math.log2(math.e)


def _softmax_pv(s, v):
    m = jnp.max(s, axis=-1, keepdims=True) * ATT_SCALE_LOG2
    e = jnp.exp2(s * ATT_SCALE_LOG2 - m)
    o = jnp.dot(e.astype(BF16), v, preferred_element_type=F32)
    return o * (1.0 / jnp.sum(e, axis=-1, keepdims=True))


def _softmax_pv2(s1, v1, s2, v2):
    m = jnp.maximum(jnp.max(s1, axis=-1, keepdims=True), jnp.max(s2, axis=-1, keepdims=True)) * ATT_SCALE_LOG2
    e1 = jnp.exp2(s1 * ATT_SCALE_LOG2 - m)
    e2 = jnp.exp2(s2 * ATT_SCALE_LOG2 - m)
    tot = jnp.sum(e1, axis=-1, keepdims=True) + jnp.sum(e2, axis=-1, keepdims=True)
    o = jnp.dot(e1.astype(BF16), v1, preferred_element_type=F32) + jnp.dot(e2.astype(BF16), v2,
                                                                           preferred_element_type=F32)
    return o * (1.0 / tot)


def _ctx_attn_kernel(a_ref, c_ref, qn_ref, kn_ref, *rest):
    oa_ref, oc_ref, nk_ref, nv_ref, kcn_ref, vcn_ref = rest[-6:]
    nk_ref[...] = a_ref[:, C_KA:C_KA + NA_H * HD].astype(F32)
    nv_ref[...] = a_ref[:, C_VA:C_VA + NA_H * HD].astype(F32)
    for h in range(NA_H):
        q = a_ref[:, C_QA + h * HD:C_QA + (h + 1) * HD]
        k = a_ref[:, C_KA + h * HD:C_KA + (h + 1) * HD]
        v = a_ref[:, C_VA + h * HD:C_VA + (h + 1) * HD]
        s = lax.dot_general(q, k, NT_DIMS, preferred_element_type=F32)
        oa_ref[:, h * HD:(h + 1) * HD] = _softmax_pv(s, v).astype(BF16)
    kc0 = C_KC - C_QC
    vc0 = C_VC - C_QC
    vcn_ref[...] = c_ref[:, vc0:vc0 + GQA_KV * HD].astype(F32)
    for kv in range(GQA_KV):
        k = _rms(c_ref[:, kc0 + kv * HD:kc0 + (kv + 1) * HD].astype(F32), kn_ref[...])
        kcn_ref[:, kv * HD:(kv + 1) * HD] = k
        kb = k.astype(BF16)
        v = c_ref[:, vc0 + kv * HD:vc0 + (kv + 1) * HD]
        for g in range(GQA_G):
            h = kv * GQA_G + g
            q = _rms(c_ref[:, h * HD:(h + 1) * HD].astype(F32), qn_ref[...]).astype(BF16)
            s = lax.dot_general(q, kb, NT_DIMS, preferred_element_type=F32)
            oc_ref[:, h * HD:(h + 1) * HD] = _softmax_pv(s, v).astype(BF16)


def _ctx_attn(u, qn, kn, l, caches):
    wa = 3 * NA_H * HD
    wc = C_Z - C_QC
    rows = lambda w: pl.BlockSpec((L_CTX, w), lambda b: (b, 0))
    lay = lambda w: pl.BlockSpec((None, None, L_CTX, w), lambda b: (b, l, 0, 0))
    cache_shape = lambda w: jax.ShapeDtypeStruct((B_CTX, DEPTH, L_CTX, w), F32)
    in_specs = [
        pl.BlockSpec((L_CTX, wa), lambda b: (b, 0)),
        pl.BlockSpec((L_CTX, wc), lambda b: (b, C_QC // wc)),
        pl.BlockSpec((None, 1, HD), lambda b: (l, 0, 0)),
        pl.BlockSpec((None, 1, HD), lambda b: (l, 0, 0)),
    ]
    args = [u, u, qn.reshape(DEPTH, 1, HD), kn.reshape(DEPTH, 1, HD)]
    aliases = {}
    if caches is not None:
        in_specs += [pl.BlockSpec(memory_space=pl.ANY)] * 4
        args += list(caches)
        aliases = {4 + n: 2 + n for n in range(4)}
    out = pl.pallas_call(
        _ctx_attn_kernel,
        grid=(B_CTX,),
        in_specs=in_specs,
        out_specs=[rows(BRANCH_W), rows(BRANCH_W), lay(NA_H * HD), lay(NA_H * HD), lay(GQA_KV * HD),
                   lay(GQA_KV * HD)],
        out_shape=[
            jax.ShapeDtypeStruct((T, BRANCH_W), BF16),
            jax.ShapeDtypeStruct((T, BRANCH_W), BF16),
            cache_shape(NA_H * HD), cache_shape(NA_H * HD), cache_shape(GQA_KV * HD), cache_shape(GQA_KV * HD),
        ],
        input_output_aliases=aliases,
        compiler_params=_cp(1),
        name="ctx_attn",
    )(*args)
    return out[0], out[1], tuple(out[2:])


def _rope(x, cos, sin):
    lane = lax.broadcasted_iota(jnp.int32, x.shape, 1)
    first = (lane & (HD // 2 - 1)) < (HD // 4)
    rot = jnp.where(first, -pltpu.roll(x, HD - HD // 4, 1), pltpu.roll(x, HD // 4, 1))
    return x * cos + rot * sin


def _lat_gqa_kernel(q_ref, k_ref, vl_ref, ck_ref, cv_ref, qn_ref, kn_ref, cos_ref, sin_ref, oc_in_ref, o_ref, kr_ref,
                    ve_ref):
    del oc_in_ref
    i = pl.program_id(2)
    tq = q_ref.shape[0]

    @pl.when(i == 0)
    def _():
        k = _rms(k_ref[...].astype(F32), kn_ref[...])
        kr_ref[...] = _rope(k, cos_ref[...], sin_ref[...]).astype(BF16)
        ve_ref[:, :HD] = vl_ref[...]
        ve_ref[:, HD:] = jnp.ones((L_LAT, HD), BF16)

    start = pl.multiple_of(i * tq, tq)
    cos = cos_ref[pl.ds(start, tq), :]
    sin = sin_ref[pl.ds(start, tq), :]
    kl = kr_ref[...]
    ve = ve_ref[...]
    ck = ck_ref[...]
    cve = jnp.concatenate([cv_ref[...], jnp.ones((PAST, HD), BF16)], axis=1)
    for g in range(GQA_G):
        q = _rope(_rms(q_ref[:, g * HD:(g + 1) * HD].astype(F32), qn_ref[...]), cos, sin)
        q = (q * ATT_SCALE_LOG2).astype(BF16)
        s1 = lax.dot_general(q, kl, NT_DIMS, preferred_element_type=F32)
        s2 = lax.dot_general(q, ck, NT_DIMS, preferred_element_type=F32)
        m = jnp.maximum(jnp.max(s1, axis=-1, keepdims=True), jnp.max(s2, axis=-1, keepdims=True))
        e1 = jnp.exp2(s1 - m).astype(BF16)
        e2 = jnp.exp2(s2 - m).astype(BF16)
        oe = jnp.dot(e1, ve, preferred_element_type=F32) + jnp.dot(e2, cve, preferred_element_type=F32)
        o_ref[:, g * HD:(g + 1) * HD] = (oe[:, :HD] * (1.0 / oe[:, HD:HD + 1])).astype(BF16)


def _lat_gqa(u, ck, cv, qn, kn, cos_t, sin_t, oc, l, tq=512):
    nq = L_LAT // tq
    r0 = T_CTX // tq
    s0 = T_CTX // L_LAT
    wq = GQA_G * HD
    full = lambda b, kv, i: (0, 0)
    return pl.pallas_call(
        _lat_gqa_kernel,
        grid=(B_LAT, GQA_KV, nq),
        in_specs=[
            pl.BlockSpec((tq, wq), lambda b, kv, i: (r0 + b * nq + i, C_QC // wq + kv)),
            pl.BlockSpec((L_LAT, HD), lambda b, kv, i: (s0 + b, C_KC // HD + kv)),
            pl.BlockSpec((L_LAT, HD), lambda b, kv, i: (s0 + b, C_VC // HD + kv)),
            pl.BlockSpec((None, None, PAST, HD), lambda b, kv, i: (b, l, 0, kv)),
            pl.BlockSpec((None, None, PAST, HD), lambda b, kv, i: (b, l, 0, kv)),
            pl.BlockSpec((None, 1, HD), lambda b, kv, i: (l, 0, 0)),
            pl.BlockSpec((None, 1, HD), lambda b, kv, i: (l, 0, 0)),
            pl.BlockSpec((L_LAT, HD), full),
            pl.BlockSpec((L_LAT, HD), full),
            pl.BlockSpec(memory_space=pl.ANY),
        ],
        out_specs=pl.BlockSpec((tq, wq), lambda b, kv, i: (r0 + b * nq + i, kv)),
        out_shape=jax.ShapeDtypeStruct((T, GQA_H * HD), BF16),
        scratch_shapes=[pltpu.VMEM((L_LAT, HD), BF16), pltpu.VMEM((L_LAT, 2 * HD), BF16)],
        input_output_aliases={9: 0},
        compiler_params=_cp(3),
        name="lat_gqa",
    )(u, u, u, ck, cv, qn.reshape(DEPTH, 1, HD), kn.reshape(DEPTH, 1, HD), cos_t, sin_t, oc)


NA_QROWS = 4
NA_UROWS = 12
NA_TQ = NA_QROWS * GRID_W
NA_TK = NA_UROWS * GRID_W
NA_HP = 8


def _lat_na_kernel(q_ref, k_ref, v_ref, ck_ref, cv_ref, t2_ref, oa_in_ref, o_ref, bias_ref):
    del oa_in_ref
    j = pl.program_id(2)
    ustart = jnp.clip(NA_QROWS * j - NA_ROWS // 2, 0, GRID_R - NA_UROWS)
    left = lax.broadcasted_iota(jnp.int32, (GRID_W, 2 * GRID_W), 1) < GRID_W
    for a in range(NA_QROWS):
        r = NA_QROWS * j + a
        rs = jnp.clip(r - NA_ROWS // 2, 0, GRID_R - NA_ROWS)
        for m in range(NA_UROWS // 2):
            kr0 = ustart + 2 * m
            v0 = ((kr0 >= rs) & (kr0 < rs + NA_ROWS)).astype(jnp.int32)
            v1 = ((kr0 + 1 >= rs) & (kr0 + 1 < rs + NA_ROWS)).astype(jnp.int32)
            idx = jnp.clip(kr0 - r + NA_ROWS, 0, 2 * NA_ROWS - 1)
            ok = jnp.where(left, v0, v1) > 0
            for hh in range(NA_HP):
                bias_ref[hh, a * GRID_W:(a + 1) * GRID_W, m * 2 * GRID_W:(m + 1) * 2 * GRID_W] = jnp.where(
                    ok, t2_ref[hh, idx], NEG)
    start = pl.multiple_of(ustart * GRID_W, GRID_W)
    for hh in range(NA_HP):
        cs = slice(hh * HD, (hh + 1) * HD)
        q = q_ref[:, cs]
        kl = k_ref[pl.ds(start, NA_TK), cs]
        vl = v_ref[pl.ds(start, NA_TK), cs]
        s_loc = lax.dot_general(q, kl, NT_DIMS, preferred_element_type=F32) + bias_ref[hh]
        s_ctx = lax.dot_general(q, ck_ref[:, cs], NT_DIMS, preferred_element_type=F32)
        o_ref[:, cs] = _softmax_pv2(s_loc, vl, s_ctx, cv_ref[:, cs]).astype(BF16)


def _lat_na(u, ck, cv, t2, oa, l):
    nj = GRID_R // NA_QROWS
    r0 = T_CTX // NA_TQ
    s0 = T_CTX // L_LAT
    w = NA_HP * HD
    return pl.pallas_call(
        _lat_na_kernel,
        grid=(B_LAT, NA_H // NA_HP, nj),
        in_specs=[
            pl.BlockSpec((NA_TQ, w), lambda b, h, j: (r0 + b * nj + j, C_QA // w + h)),
            pl.BlockSpec((L_LAT, w), lambda b, h, j: (s0 + b, C_KA // w + h)),
            pl.BlockSpec((L_LAT, w), lambda b, h, j: (s0 + b, C_VA // w + h)),
            pl.BlockSpec((None, None, PAST, w), lambda b, h, j: (b, l, 0, h)),
            pl.BlockSpec((None, None, PAST, w), lambda b, h, j: (b, l, 0, h)),
            pl.BlockSpec((None, NA_HP, 2 * NA_ROWS, GRID_W, 2 * GRID_W), lambda b, h, j: (l, h, 0, 0, 0)),
            pl.BlockSpec(memory_space=pl.ANY),
        ],
        out_specs=pl.BlockSpec((NA_TQ, w), lambda b, h, j: (r0 + b * nj + j, h)),
        out_shape=jax.ShapeDtypeStruct((T, NA_H * HD), BF16),
        scratch_shapes=[pltpu.VMEM((NA_HP, NA_TQ, NA_TK), F32)],
        input_output_aliases={6: 0},
        compiler_params=_cp(3),
        name="lat_na",
    )(u, u, u, ck, cv, t2, oa)


def _na_bias_tiles(rpb):
    w = jnp.arange(GRID_W)
    cs = jnp.clip(w - NA_COLS // 2, 0, GRID_W - NA_COLS)
    cmask = (w[None, :] >= cs[:, None]) & (w[None, :] < cs[:, None] + NA_COLS)
    dc = jnp.clip(w[None, :] - w[:, None] + NA_COLS - 1, 0, 2 * NA_COLS - 2)
    tm = jnp.where(cmask, rpb[:, :, :, dc] / ATT_SCALE, NEG)
    pad = jnp.full(tm.shape[:2] + (1,) + tm.shape[3:], NEG, F32)
    lo = jnp.concatenate([pad, tm], axis=2)
    hi = jnp.concatenate([tm, pad], axis=2)
    return jnp.concatenate([lo, hi], axis=-1).astype(F32)


def _rope_tables():
    nf = HD // 4
    t = jnp.arange(L_LAT)
    inv = ROPE_BASE ** (-jnp.arange(nf, dtype=F32) / nf)
    ar = (t // GRID_W).astype(F32)[:, None] * inv[None, :]
    ac = (t % GRID_W).astype(F32)[:, None] * inv[None, :]
    cos_t = jnp.concatenate([jnp.cos(ar), jnp.cos(ar), jnp.cos(ac), jnp.cos(ac)], axis=-1)
    sin_t = jnp.concatenate([jnp.sin(ar), jnp.sin(ar), jnp.sin(ac), jnp.sin(ac)], axis=-1)
    return cos_t, sin_t


def _conv_kernel(x_ref, w_ref, b_ref, o_ref):
    x = x_ref[...].astype(F32)
    n = x.shape[0]
    row = lax.broadcasted_iota(jnp.int32, (SUBLANES, x.shape[1]), 0)
    xm = pltpu.roll(x, 1, 0)
    xm = jnp.concatenate([jnp.where(row == 0, 0.0, xm[:SUBLANES]), xm[SUBLANES:]], axis=0)
    xp = pltpu.roll(x, n - 1, 0)
    xp = jnp.concatenate([xp[:n - SUBLANES], jnp.where(row == SUBLANES - 1, 0.0, xp[n - SUBLANES:])], axis=0)
    y = w_ref[0:1, :] * xm + w_ref[1:2, :] * x + w_ref[2:3, :] * xp + b_ref[...]
    hy = 0.5 * y
    o_ref[...] = (hy * (1.0 + jnp.tanh(hy))).astype(o_ref.dtype)


def _conv(u, conv_w, conv_b, l, nseq, seq_len, row0):
    tc = 512
    nc = CONV_CH // tc
    rb0 = row0 // seq_len
    cb0 = C_XBC // tc
    return pl.pallas_call(
        _conv_kernel,
        grid=(nseq, nc),
        in_specs=[
            pl.BlockSpec((seq_len, tc), lambda b, j: (rb0 + b, cb0 + j)),
            pl.BlockSpec((None, 3, tc), lambda b, j: (l, 0, j)),
            pl.BlockSpec((None, 1, tc), lambda b, j: (l, 0, j)),
        ],
        out_specs=pl.BlockSpec((seq_len, tc), lambda b, j: (b, j)),
        out_shape=jax.ShapeDtypeStruct((nseq * seq_len, CONV_CH), F32),
        compiler_params=_cp(2),
        name="conv",
    )(u, conv_w, conv_b.reshape(DEPTH, 1, CONV_CH))


def _softplus(x):
    return jnp.maximum(x, 0.0) + jnp.log1p(jnp.exp(-jnp.abs(x)))


def _expand_heads(v, expand, terms):
    out = None
    rem = v
    for _ in range(terms):
        piece = rem.astype(BF16)
        rem = rem - piece.astype(F32)
        t = jnp.dot(piece, expand, preferred_element_type=F32)
        out = t if out is None else out + t
    return out


def _ssd_chunk(xbc, dt_col, dt_row, a_row, a_col, h_ref, reverse):
    x = xbc[:, :SSD_INNER]
    bm = xbc[:, SSD_INNER:SSD_INNER + SSD_G * SSD_N]
    cm = xbc[:, SSD_INNER + SSD_G * SSD_N:]
    li = lax.broadcasted_iota(jnp.int32, (CHUNK, CHUNK), 0)
    si = lax.broadcasted_iota(jnp.int32, (CHUNK, CHUNK), 1)
    mask = (si >= li) if reverse else (si <= li)
    tri = mask.astype(F32)
    tri_t = ((li >= si) if reverse else (li <= si)).astype(F32)
    cum_col = jnp.dot(tri, dt_col * a_row, precision=HI, preferred_element_type=F32)
    cum_row = jnp.dot(dt_row * a_col, tri_t, precision=HI, preferred_element_type=F32)
    tot_row = cum_col[0:1, :] if reverse else cum_col[CHUNK - 1:CHUNK, :]
    col = lax.broadcasted_iota(jnp.int32, (SSD_H, SSD_INNER), 1)
    lo_col = lax.broadcasted_iota(jnp.int32, (SSD_H, SSD_INNER), 0) * SSD_P
    expand = ((col >= lo_col) & (col < lo_col + SSD_P)).astype(BF16)
    w_exp = _expand_heads(jnp.exp(tot_row - cum_col) * dt_col, expand, 1)
    e_exp = _expand_heads(jnp.exp(cum_col), expand, 1)
    dec = _expand_heads(jnp.broadcast_to(jnp.exp(tot_row), (8, SSD_H)), expand, 3)[0:1, :]
    hpg = SSD_H // SSD_G
    xb = x.astype(BF16)
    bgs = [bm[:, g * SSD_N:(g + 1) * SSD_N].astype(BF16) for g in range(SSD_G)]
    cgs = [cm[:, g * SSD_N:(g + 1) * SSD_N].astype(BF16) for g in range(SSD_G)]
    gmat = [lax.dot_general(cgs[g], bgs[g], NT_DIMS, preferred_element_type=F32) for g in range(SSD_G)]
    lo = lax.broadcasted_iota(jnp.int32, (CHUNK, 2 * SSD_P), 1) < SSD_P
    zero = jnp.zeros((CHUNK, 2 * SSD_P), BF16)
    ys = []
    for pr in range(SSD_H // 2):
        g = (2 * pr) // hpg
        ms = []
        for h in (2 * pr, 2 * pr + 1):
            seg = cum_col[:, h:h + 1] - cum_row[h:h + 1, :]
            lm = jnp.exp(jnp.where(mask, seg, NEG))
            ms.append((gmat[g] * lm * dt_row[h:h + 1, :]).astype(BF16))
        xp = xb[:, pr * 2 * SSD_P:(pr + 1) * 2 * SSD_P]
        rhs = jnp.concatenate([jnp.where(lo, xp, zero), jnp.where(lo, zero, xp)], axis=0)
        ys.append(jnp.dot(jnp.concatenate(ms, axis=1), rhs, preferred_element_type=F32))
    y = jnp.concatenate(ys, axis=1)
    h_t = h_ref[...]
    hb = h_t.astype(BF16)
    xw = (x * w_exp).astype(BF16)
    wpg = SSD_INNER // SSD_G
    yo = [jnp.dot(cgs[g], hb[:, g * wpg:(g + 1) * wpg], preferred_element_type=F32) for g in range(SSD_G)]
    st = [lax.dot_general(bgs[g], xw[:, g * wpg:(g + 1) * wpg], TN_DIMS, preferred_element_type=F32)
          for g in range(SSD_G)]
    h_ref[...] = h_t * dec + jnp.concatenate(st, axis=1)
    return y + jnp.concatenate(yo, axis=1) * e_exp


SSD_STEP_CHUNKS = 2


def _ssd_kernel(xf_ref, xb_ref, dtf_ref, dtb_ref, dttf_ref, dttb_ref, prow_ref, pcol_ref, dtot_ref, *rest,
                context):
    hf_ref, hb_ref = rest[-2:]
    if context:
        yf_ref, yb_ref, hfin_ref = rest[-5:-2]
    else:
        h0_ref, yf_ref, yb_ref = rest[-5:-2]
    c = pl.program_id(1)

    @pl.when(c == 0)
    def _():
        if context:
            hf_ref[...] = jnp.zeros_like(hf_ref)
            hb_ref[...] = jnp.zeros_like(hb_ref)
        else:
            hf_ref[...] = h0_ref[0].T
            hb_ref[...] = h0_ref[1].T

    prow = prow_ref[...]
    pcol = pcol_ref[...]
    for s in range(SSD_STEP_CHUNKS):
        r = slice(s * CHUNK, (s + 1) * CHUNK)
        xf = xf_ref[r, :]
        dt_col = _softplus(dtf_ref[r, 0:SSD_H] + prow[0:1, :])
        dt_row = _softplus(dttf_ref[0:SSD_H, r] + pcol[:, 0:1])
        yf = _ssd_chunk(xf, dt_col, dt_row, -jnp.exp(prow[2:3, :]), -jnp.exp(pcol[:, 2:3]), hf_ref, False)
        yf_ref[r, :] = (yf + dtot_ref[...] * xf[:, :SSD_INNER]).astype(yf_ref.dtype)
    for s in reversed(range(SSD_STEP_CHUNKS)):
        r = slice(s * CHUNK, (s + 1) * CHUNK)
        dt_col = _softplus(dtb_ref[r, SSD_H:2 * SSD_H] + prow[1:2, :])
        dt_row = _softplus(dttb_ref[SSD_H:2 * SSD_H, r] + pcol[:, 1:2])
        yb_ref[r, :] = _ssd_chunk(xb_ref[r, :], dt_col, dt_row, -jnp.exp(prow[3:4, :]), -jnp.exp(pcol[:, 3:4]),
                                  hb_ref, True).astype(yb_ref.dtype)

    if context:
        @pl.when(c == pl.num_programs(1) - 1)
        def _():
            hfin_ref[0] = hf_ref[...].T
            hfin_ref[1] = hb_ref[...].T


def _ssd(xbc, u_dt, dt_t, prow, pcol, dtot, l, nseq, seq_len, row0, states, context):
    blk = SSD_STEP_CHUNKS * CHUNK
    nc = seq_len // blk
    cb0 = row0 // blk
    rows = nseq * seq_len
    fwd = lambda b, c: b * nc + c
    bwd = lambda b, c: b * nc + (nc - 1 - c)
    state_spec = pl.BlockSpec((None, None, 2, SSD_INNER, SSD_N), lambda b, c: (b, l, 0, 0, 0))
    in_specs = [
        pl.BlockSpec((blk, CONV_CH), lambda b, c: (fwd(b, c), 0)),
        pl.BlockSpec((blk, CONV_CH), lambda b, c: (bwd(b, c), 0)),
        pl.BlockSpec((blk, DT_PAD), lambda b, c: (cb0 + fwd(b, c), 0)),
        pl.BlockSpec((blk, DT_PAD), lambda b, c: (cb0 + bwd(b, c), 0)),
        pl.BlockSpec((DT_PAD, blk), lambda b, c: (0, cb0 + fwd(b, c))),
        pl.BlockSpec((DT_PAD, blk), lambda b, c: (0, cb0 + bwd(b, c))),
        pl.BlockSpec((4, SSD_H), lambda b, c: (0, 0)),
        pl.BlockSpec((SSD_H, 4), lambda b, c: (0, 0)),
        pl.BlockSpec((1, SSD_INNER), lambda b, c: (0, 0)),
    ]
    args = [xbc, xbc, u_dt, u_dt, dt_t, dt_t, prow, pcol, dtot]
    out_specs = [
        pl.BlockSpec((blk, SSD_INNER), lambda b, c: (fwd(b, c), 0)),
        pl.BlockSpec((blk, SSD_INNER), lambda b, c: (bwd(b, c), 0)),
    ]
    out_shape = [jax.ShapeDtypeStruct((rows, SSD_INNER), BF16), jax.ShapeDtypeStruct((rows, SSD_INNER), BF16)]
    aliases = {}
    if context:
        out_specs.append(state_spec)
        out_shape.append(jax.ShapeDtypeStruct((nseq, DEPTH, 2, SSD_INNER, SSD_N), F32))
        if states is not None:
            in_specs.append(pl.BlockSpec(memory_space=pl.ANY))
            args.append(states)
            aliases = {len(args) - 1: 2}
    else:
        in_specs.append(state_spec)
        args.append(states)
    return pl.pallas_call(
        functools.partial(_ssd_kernel, context=context),
        grid=(nseq, nc),
        in_specs=in_specs,
        out_specs=out_specs,
        out_shape=out_shape,
        scratch_shapes=[pltpu.VMEM((SSD_N, SSD_INNER), F32), pltpu.VMEM((SSD_N, SSD_INNER), F32)],
        input_output_aliases=aliases,
        compiler_params=_cp(2),
        name="ssd",
    )(*args)


def _ssd_gate_kernel(yf_ref, yb_ref, z_ref, g_ref, *rest):
    o_ref = rest[-1]
    y = (yf_ref[...].astype(F32) + yb_ref[...].astype(F32)) * _silu(z_ref[...].astype(F32))
    o_ref[...] = _rms(y, g_ref[...]).astype(BF16)


def _ssd_gate(yf, yb, u, norm_w, l, row0, ob=None):
    tm = ROW_TILE
    rows = yf.shape[0]
    rb0 = row0 // tm
    row = pl.BlockSpec((tm, SSD_INNER), lambda i: (i, 0))
    in_specs = [
        row, row,
        pl.BlockSpec((pl.Element(tm), pl.Element(SSD_INNER)), lambda i: (tm * (rb0 + i), C_Z)),
        pl.BlockSpec((None, 1, SSD_INNER), lambda i: (l, 0, 0)),
    ]
    args = [yf, yb, u, norm_w.reshape(DEPTH, 1, SSD_INNER)]
    aliases = {}
    if ob is not None:
        in_specs.append(pl.BlockSpec(memory_space=pl.ANY))
        args.append(ob)
        aliases = {4: 0}
    return pl.pallas_call(
        _ssd_gate_kernel,
        grid=(rows // tm,),
        in_specs=in_specs,
        out_specs=pl.BlockSpec((tm, SSD_INNER), lambda i: (rb0 + i, 0)),
        out_shape=jax.ShapeDtypeStruct((T, SSD_INNER), BF16),
        input_output_aliases=aliases,
        compiler_params=_cp(1),
        name="ssd_gate",
    )(*args)


def kernel(x_prompt, x_sample, cache_na_k, cache_na_v, cache_gqa_k, cache_gqa_v, state_ssd, c, c_ctx, ada_w, ada_b, norm_mix_pre, norm_mix_post, norm_ffn_pre, norm_ffn_post, w_in, gate_b, na_rpb, gqa_q_norm, gqa_k_norm, ssd_conv_w, ssd_conv_b, ssd_dt_bias, ssd_a_log, ssd_d, ssd_norm_w, w_branch, w_out, ffn_w_up, ffn_w_down):
    cvec = jnp.concatenate([c_ctx[None, :], c, jnp.zeros((8 - 1 - B_LAT, D), F32)], axis=0)
    mods = _compute_mods(cvec, ada_w, ada_b).reshape(DEPTH, 8, 6, 1, D)
    wt_in = jnp.swapaxes(w_in, 1, 2)
    t2 = _na_bias_tiles(na_rpb)
    cos_t, sin_t = _rope_tables()
    ck_a = cache_na_k.reshape(B_LAT, DEPTH, PAST, NA_H * HD).astype(BF16)
    cv_a = cache_na_v.reshape(B_LAT, DEPTH, PAST, NA_H * HD).astype(BF16)
    ck_c = cache_gqa_k.reshape(B_LAT, DEPTH, PAST, GQA_KV * HD).astype(BF16)
    cv_c = cache_gqa_v.reshape(B_LAT, DEPTH, PAST, GQA_KV * HD).astype(BF16)
    h0_lat = state_ssd.reshape(B_LAT, DEPTH, 2, SSD_INNER, SSD_N)
    prow = jnp.concatenate([ssd_dt_bias, ssd_a_log], axis=1)
    pcol = prow.transpose(0, 2, 1)
    dtot = jnp.repeat(ssd_d[:, 0] + ssd_d[:, 1], SSD_P, axis=-1)[:, None, :]

    x, h = _normmod(x_prompt.reshape(T_CTX, D), x_sample.reshape(T_LAT, D), norm_mix_pre, mods, 0, 0, 1)
    caches = None
    ssd_fin = None
    for l in range(DEPTH):
        u, u_dt, dt_t = _in_proj(h, wt_in, l)
        oa, oc, caches = _ctx_attn(u, gqa_q_norm, gqa_k_norm, l, caches)
        oc = _lat_gqa(u, ck_c, cv_c, gqa_q_norm, gqa_k_norm, cos_t, sin_t, oc, l)
        oa = _lat_na(u, ck_a, cv_a, t2, oa, l)
        xbc_c = _conv(u, ssd_conv_w, ssd_conv_b, l, B_CTX, L_CTX, 0)
        xbc_l = _conv(u, ssd_conv_w, ssd_conv_b, l, B_LAT, L_LAT, T_CTX)
        yf_c, yb_c, ssd_fin = _ssd(xbc_c, u_dt, dt_t, prow[l], pcol[l], dtot[l], l, B_CTX, L_CTX, 0, ssd_fin, True)
        yf_l, yb_l = _ssd(xbc_l, u_dt, dt_t, prow[l], pcol[l], dtot[l], l, B_LAT, L_LAT, T_CTX, h0_lat, False)
        ob = _ssd_gate(yf_c, yb_c, u, ssd_norm_w, l, 0)
        ob = _ssd_gate(yf_l, yb_l, u, ssd_norm_w, l, T_CTX, ob)
        mix = _merge(h, oa, ob, oc, wt_in, w_branch, gate_b, l)
        x, h2 = _proj_resid(mix, w_out, l, x, norm_mix_post, mods, l, 2, 512, (norm_ffn_pre, l, 3, 4), "out_proj")
        act, w_down = _matmul_swiglu(h2, ffn_w_up, ffn_w_down, l)
        nxt = (norm_mix_pre, l + 1, 0, 1) if l + 1 < DEPTH else None
        x, h = _proj_resid(act, w_down, 0, x, norm_ffn_post, mods, l, 5, 256, nxt, "ffn_down")
    na_k, na_v, gqa_k, gqa_v = caches
    return (x.reshape(B_CTX, L_CTX, D), h.reshape(B_LAT, L_LAT, D),
            na_k.reshape(B_CTX, DEPTH, L_CTX, NA_H, HD), na_v.reshape(B_CTX, DEPTH, L_CTX, NA_H, HD),
            gqa_k.reshape(B_CTX, DEPTH, L_CTX, GQA_KV, HD), gqa_v.reshape(B_CTX, DEPTH, L_CTX, GQA_KV, HD),
            ssd_fin.reshape(B_CTX, DEPTH, 2, SSD_H, SSD_P, SSD_N))
```

```python
import functools
import math

import jax
import jax.numpy as jnp
from jax import lax
from jax.experimental import pallas as pl
from jax.experimental.pallas import tpu as pltpu

F32 = jnp.float32
BF16 = jnp.bfloat16
HI = lax.Precision.HIGHEST
EPS = 1e-6
NEG = -1e30

D = 2048
DEPTH = 4
B_CTX, L_CTX = 16, 256
B_LAT, L_LAT = 4, 2048
PAST = 512
GRID_W = 64
GRID_R = L_LAT // GRID_W
NA_H = 8
HD = 128
NA_ROWS, NA_COLS = 8, 16
GQA_H, GQA_KV = 8, 2
GQA_G = GQA_H // GQA_KV
ROPE_BASE = 10000.0
SSD_INNER = 1024
SSD_P = 64
SSD_H = SSD_INNER // SSD_P
SSD_N = 128
SSD_G = 2
CHUNK = 128
CONV_CH = SSD_INNER + 2 * SSD_G * SSD_N
BRANCH_W = 1024
FFN_H = 5632
T_CTX = B_CTX * L_CTX
T_LAT = B_LAT * L_LAT
T = T_CTX + T_LAT

C_QA, C_KA, C_VA = 0, NA_H * HD, 2 * NA_H * HD
C_QC = 3 * NA_H * HD
C_KC = C_QC + GQA_H * HD
C_VC = C_KC + GQA_KV * HD
C_Z = C_VC + GQA_KV * HD
C_XBC = C_Z + SSD_INNER
C_DT = C_XBC + CONV_CH
C_GL = C_DT + 2 * SSD_H
N_MAIN = C_DT

NT_DIMS = (((1,), (1,)), ((), ()))
TN_DIMS = (((0,), (0,)), ((), ()))

ROW_TILE = 256
SUBLANES = 8
VMEM_BIG = 56


def _cp(n_axes, vmem_mb=None):
    kw = dict(dimension_semantics=("arbitrary",) * n_axes)
    if vmem_mb is not None:
        kw["vmem_limit_bytes"] = vmem_mb * 1024 * 1024
    return pltpu.CompilerParams(**kw)


def _mod_row(i, tm):
    n_ctx = T_CTX // tm
    n_lat = L_LAT // tm
    return jnp.where(i < n_ctx, 0, 1 + (i - n_ctx) // n_lat)


def _rms(x, g):
    return x * lax.rsqrt(jnp.mean(x * x, axis=-1, keepdims=True) + EPS) * g


def _silu(x):
    return x * jax.nn.sigmoid(x)


def _mods_kernel(c_ref, w_ref, b_ref, o_ref):
    s = _silu(c_ref[...]).astype(BF16)
    o_ref[...] = jnp.dot(s, w_ref[...].astype(BF16), preferred_element_type=F32) + b_ref[...]


def _compute_mods(cvec, ada_w, ada_b):
    tn = 1024
    n = 6 * D
    return pl.pallas_call(
        _mods_kernel,
        grid=(DEPTH, n // tn),
        in_specs=[
            pl.BlockSpec((8, D), lambda l, j: (0, 0)),
            pl.BlockSpec((None, D, tn), lambda l, j: (l, 0, j)),
            pl.BlockSpec((None, 1, tn), lambda l, j: (l, 0, j)),
        ],
        out_specs=pl.BlockSpec((None, 8, tn), lambda l, j: (l, 0, j)),
        out_shape=jax.ShapeDtypeStruct((DEPTH, 8, n), F32),
        compiler_params=_cp(2),
        name="mods",
    )(cvec, ada_w, ada_b.reshape(DEPTH, 1, n))


def _mod_spec(l, k, tm):
    return pl.BlockSpec((None, None, None, 1, D), lambda i: (l, _mod_row(i, tm), k, 0, 0))


def _gain_spec(l):
    return pl.BlockSpec((None, 1, D), lambda i: (l, 0, 0))


def _normmod_kernel(xc_ref, xl_ref, g_ref, sh_ref, sc_ref, x_ref, h_ref):
    x = jnp.where(pl.program_id(0) < T_CTX // ROW_TILE, xc_ref[...], xl_ref[...])
    x_ref[...] = x
    h_ref[...] = (_rms(x, g_ref[...]) * (1.0 + sc_ref[...]) + sh_ref[...]).astype(BF16)


def _normmod(x_ctx, x_lat, gains, mods, l, k_sh, k_sc):
    tm = ROW_TILE
    n_ctx = T_CTX // tm
    row = pl.BlockSpec((tm, D), lambda i: (i, 0))
    return pl.pallas_call(
        _normmod_kernel,
        grid=(T // tm,),
        in_specs=[
            pl.BlockSpec((tm, D), lambda i: (jnp.minimum(i, n_ctx - 1), 0)),
            pl.BlockSpec((tm, D), lambda i: (jnp.maximum(i - n_ctx, 0), 0)),
            _gain_spec(l),
            _mod_spec(l, k_sh, tm),
            _mod_spec(l, k_sc, tm),
        ],
        out_specs=[row, row],
        out_shape=[jax.ShapeDtypeStruct((T, D), F32), jax.ShapeDtypeStruct((T, D), BF16)],
        compiler_params=_cp(1),
        name="normmod",
    )(x_ctx, x_lat, gains.reshape(DEPTH, 1, D), mods, mods)


DT_PAD = 128


def _in_proj_kernel(x_ref, wt_ref, wdt_ref, o_ref, odt_ref, odtt_ref, wb_ref):
    @pl.when(pl.program_id(1) == 0)
    def _():
        wb_ref[...] = wt_ref[...].astype(BF16).T

    x = x_ref[...]
    o_ref[...] = jnp.dot(x, wb_ref[...], preferred_element_type=F32).astype(o_ref.dtype)

    @pl.when(pl.program_id(0) == pl.num_programs(0) - 1)
    def _():
        dt = lax.dot_general(x, wdt_ref[...].astype(BF16), NT_DIMS, preferred_element_type=F32)
        odt_ref[...] = dt
        odtt_ref[...] = dt.T


def _in_proj(x, wt, l, tn=1024, tm=1024):
    m, k = x.shape
    nj = N_MAIN // tn
    last = lambda j, i: jnp.where(j == nj - 1, i, 0)
    return pl.pallas_call(
        _in_proj_kernel,
        grid=(nj, m // tm),
        in_specs=[
            pl.BlockSpec((tm, k), lambda j, i: (i, 0)),
            pl.BlockSpec((None, tn, k), lambda j, i: (l, j, 0)),
            pl.BlockSpec((None, DT_PAD, k), lambda j, i: (l, C_DT // DT_PAD, 0)),
        ],
        out_specs=[
            pl.BlockSpec((tm, tn), lambda j, i: (i, j)),
            pl.BlockSpec((tm, DT_PAD), lambda j, i: (last(j, i), 0)),
            pl.BlockSpec((DT_PAD, tm), lambda j, i: (0, last(j, i))),
        ],
        out_shape=[
            jax.ShapeDtypeStruct((m, N_MAIN), BF16),
            jax.ShapeDtypeStruct((m, DT_PAD), F32),
            jax.ShapeDtypeStruct((DT_PAD, m), F32),
        ],
        scratch_shapes=[pltpu.VMEM((k, tn), BF16)],
        compiler_params=_cp(2, 40),
        name="in_proj",
    )(x, wt, wt)


WD_COLS = 256


def _mm_swiglu_kernel(x_ref, wg_ref, wu_ref, wd_ref, o_ref, owd_ref, wgb_ref, wub_ref):
    @pl.when(pl.program_id(1) == 0)
    def _():
        wgb_ref[...] = wg_ref[...].astype(BF16)
        wub_ref[...] = wu_ref[...].astype(BF16)

    x = x_ref[...]
    g = jnp.dot(x, wgb_ref[...], preferred_element_type=F32)
    u = jnp.dot(x, wub_ref[...], preferred_element_type=F32)
    o_ref[...] = (_silu(g) * u).astype(o_ref.dtype)

    @pl.when(pl.program_id(1) < D // WD_COLS)
    def _():
        owd_ref[...] = wd_ref[...].astype(BF16)


def _matmul_swiglu(x, w, w_down, l, tn=512, tm=1024):
    m, k = x.shape
    nb = FFN_H // tn
    ncb = D // WD_COLS
    assert m // tm >= ncb
    wd_idx = lambda j, i: (j, jnp.minimum(i, ncb - 1))
    return pl.pallas_call(
        _mm_swiglu_kernel,
        grid=(nb, m // tm),
        in_specs=[
            pl.BlockSpec((tm, k), lambda j, i: (i, 0)),
            pl.BlockSpec((None, k, tn), lambda j, i: (l, 0, j)),
            pl.BlockSpec((None, k, tn), lambda j, i: (l, 0, nb + j)),
            pl.BlockSpec((None, tn, WD_COLS), lambda j, i: (l,) + wd_idx(j, i)),
        ],
        out_specs=[
            pl.BlockSpec((tm, tn), lambda j, i: (i, j)),
            pl.BlockSpec((None, tn, WD_COLS), lambda j, i: (0,) + wd_idx(j, i)),
        ],
        out_shape=[jax.ShapeDtypeStruct((m, FFN_H), BF16), jax.ShapeDtypeStruct((1, FFN_H, D), BF16)],
        scratch_shapes=[pltpu.VMEM((k, tn), BF16), pltpu.VMEM((k, tn), BF16)],
        compiler_params=_cp(2, 40),
        name="ffn_up",
    )(x, w, w, w_down)


def _merge_kernel(h_ref, oa_ref, ob_ref, oc_ref, wg0, wg1, wg2, wb0, wb1, wb2, gb0, gb1, gb2, o_ref, wgb_ref, wbb_ref):
    wg = (wg0, wg1, wg2)
    wb = (wb0, wb1, wb2)
    gb = (gb0, gb1, gb2)

    @pl.when(pl.program_id(1) == 0)
    def _():
        for g in range(3):
            wgb_ref[g] = wg[g][0].astype(BF16).T
            wbb_ref[g] = wb[g][...].astype(BF16)

    h = h_ref[...]
    acc = None
    for g, o_r in enumerate((oa_ref, ob_ref, oc_ref)):
        gl = jnp.dot(h, wgb_ref[g], preferred_element_type=F32) + gb[g][...]
        p = jnp.dot(o_r[...], wbb_ref[g], preferred_element_type=F32)
        t = jax.nn.sigmoid(gl) * p
        acc = t if acc is None else acc + t
    o_ref[...] = acc.astype(BF16)


def _merge(h, oa, ob, oc, wt_in, w_branch, gate_b, l, tn=512, tm=512):
    nb = D // tn
    xrow = pl.BlockSpec((tm, D), lambda j, i: (i, 0))
    orow = pl.BlockSpec((tm, BRANCH_W), lambda j, i: (i, 0))
    once = pl.Buffered(1)
    assert C_GL % SUBLANES == 0 and tn % SUBLANES == 0

    def wg_spec(g):
        row0 = (C_GL + g * D) // SUBLANES
        return pl.BlockSpec((pl.Element(1), pl.Element(tn), pl.Element(D)),
                            lambda j, i: (l, SUBLANES * (row0 + j * (tn // SUBLANES)), 0), pipeline_mode=once)

    def wb_spec(g):
        return pl.BlockSpec((None, None, BRANCH_W, tn), lambda j, i: (l, g, 0, j), pipeline_mode=once)

    def gb_spec(g):
        return pl.BlockSpec((None, 1, tn), lambda j, i: (l, 0, g * nb + j))

    gate_b3 = gate_b.reshape(DEPTH, 1, 3 * D)
    return pl.pallas_call(
        _merge_kernel,
        grid=(nb, T // tm),
        in_specs=[xrow, orow, orow, orow]
        + [wg_spec(g) for g in range(3)]
        + [wb_spec(g) for g in range(3)]
        + [gb_spec(g) for g in range(3)],
        out_specs=pl.BlockSpec((tm, tn), lambda j, i: (i, j)),
        out_shape=jax.ShapeDtypeStruct((T, D), BF16),
        scratch_shapes=[pltpu.VMEM((3, D, tn), BF16), pltpu.VMEM((3, BRANCH_W, tn), BF16)],
        compiler_params=_cp(2, 48),
        name="merge",
    )(h, oa, ob, oc, wt_in, wt_in, wt_in, w_branch, w_branch, w_branch, gate_b3, gate_b3, gate_b3)


def _proj_resid_kernel(a_ref, w_ref, x_ref, gpost_ref, gate_ref, *rest, with_h, cast_w):
    if cast_w:
        wb_ref = rest[-1]
        rest = rest[:-1]

        @pl.when(pl.program_id(0) == 0)
        def _():
            wb_ref[...] = w_ref[...].astype(BF16)

        w = wb_ref[...]
    else:
        w = w_ref[...]
    y = jnp.dot(a_ref[...], w, preferred_element_type=F32)
    xn = x_ref[...] + gate_ref[...] * _rms(y, gpost_ref[...])
    if with_h:
        gpre_ref, sh_ref, sc_ref, xo_ref, h_ref = rest
        xo_ref[...] = xn
        h_ref[...] = (_rms(xn, gpre_ref[...]) * (1.0 + sc_ref[...]) + sh_ref[...]).astype(BF16)
    else:
        oc_ref, ol_ref = rest
        n_ctx = T_CTX // xn.shape[0]

        @pl.when(pl.program_id(0) < n_ctx)
        def _():
            oc_ref[...] = xn

        @pl.when(pl.program_id(0) >= n_ctx)
        def _():
            ol_ref[...] = xn


def _proj_resid(a, w, wl, x, post_gains, mods, l, k_gate, tm, nxt, name):
    k = a.shape[1]
    cast_w = w.dtype != BF16
    row = pl.BlockSpec((tm, D), lambda i: (i, 0))
    in_specs = [
        pl.BlockSpec((tm, k), lambda i: (i, 0)),
        pl.BlockSpec((None, k, D), lambda i: (wl, 0, 0), pipeline_mode=pl.Buffered(1)),
        row, _gain_spec(l), _mod_spec(l, k_gate, tm),
    ]
    args = [a, w, x, post_gains.reshape(DEPTH, 1, D), mods]
    out_specs = [row]
    out_shape = [jax.ShapeDtypeStruct((T, D), F32)]
    if nxt is not None:
        gains, ln, k_sh, k_sc = nxt
        in_specs += [_gain_spec(ln), _mod_spec(ln, k_sh, tm), _mod_spec(ln, k_sc, tm)]
        args += [gains.reshape(DEPTH, 1, D), mods, mods]
        out_specs.append(row)
        out_shape.append(jax.ShapeDtypeStruct((T, D), BF16))
    else:
        n_ctx = T_CTX // tm
        out_specs = [pl.BlockSpec((tm, D), lambda i: (jnp.minimum(i, n_ctx - 1), 0)),
                     pl.BlockSpec((tm, D), lambda i: (jnp.maximum(i - n_ctx, 0), 0))]
        out_shape = [jax.ShapeDtypeStruct((T_CTX, D), F32), jax.ShapeDtypeStruct((T_LAT, D), F32)]
    out = pl.pallas_call(
        functools.partial(_proj_resid_kernel, with_h=nxt is not None, cast_w=cast_w),
        grid=(T // tm,),
        in_specs=in_specs,
        out_specs=out_specs,
        out_shape=out_shape,
        scratch_shapes=[pltpu.VMEM((k, D), BF16)] if cast_w else [],
        compiler_params=_cp(1, VMEM_BIG),
        name=name,
    )(*args)
    return out


ATT_SCALE = HD ** -0.5
ATT_SCALE_LOG2 = ATT_SCALE * math.log2(math.e)


def _softmax_pv(s, v):
    m = jnp.max(s, axis=-1, keepdims=True) * ATT_SCALE_LOG2
    e = jnp.exp2(s * ATT_SCALE_LOG2 - m)
    o = jnp.dot(e.astype(BF16), v, preferred_element_type=F32)
    return o * (1.0 / jnp.sum(e, axis=-1, keepdims=True))


def _softmax_pv2(s1, v1, s2, v2):
    m = jnp.maximum(jnp.max(s1, axis=-1, keepdims=True), jnp.max(s2, axis=-1, keepdims=True)) * ATT_SCALE_LOG2
    e1 = jnp.exp2(s1 * ATT_SCALE_LOG2 - m)
    e2 = jnp.exp2(s2 * ATT_SCALE_LOG2 - m)
    tot = jnp.sum(e1, axis=-1, keepdims=True) + jnp.sum(e2, axis=-1, keepdims=True)
    o = jnp.dot(e1.astype(BF16), v1, preferred_element_type=F32) + jnp.dot(e2.astype(BF16), v2,
                                                                           preferred_element_type=F32)
    return o * (1.0 / tot)


def _ctx_attn_kernel(a_ref, c_ref, qn_ref, kn_ref, *rest):
    oa_ref, oc_ref, nk_ref, nv_ref, kcn_ref, vcn_ref = rest[-6:]
    nk_ref[...] = a_ref[:, C_KA:C_KA + NA_H * HD].astype(F32)
    nv_ref[...] = a_ref[:, C_VA:C_VA + NA_H * HD].astype(F32)
    for h in range(NA_H):
        q = a_ref[:, C_QA + h * HD:C_QA + (h + 1) * HD]
        k = a_ref[:, C_KA + h * HD:C_KA + (h + 1) * HD]
        v = a_ref[:, C_VA + h * HD:C_VA + (h + 1) * HD]
        s = lax.dot_general(q, k, NT_DIMS, preferred_element_type=F32)
        oa_ref[:, h * HD:(h + 1) * HD] = _softmax_pv(s, v).astype(BF16)
    kc0 = C_KC - C_QC
    vc0 = C_VC - C_QC
    vcn_ref[...] = c_ref[:, vc0:vc0 + GQA_KV * HD].astype(F32)
    for kv in range(GQA_KV):
        k = _rms(c_ref[:, kc0 + kv * HD:kc0 + (kv + 1) * HD].astype(F32), kn_ref[...])
        kcn_ref[:, kv * HD:(kv + 1) * HD] = k
        kb = k.astype(BF16)
        v = c_ref[:, vc0 + kv * HD:vc0 + (kv + 1) * HD]
        for g in range(GQA_G):
            h = kv * GQA_G + g
            q = _rms(c_ref[:, h * HD:(h + 1) * HD].astype(F32), qn_ref[...]).astype(BF16)
            s = lax.dot_general(q, kb, NT_DIMS, preferred_element_type=F32)
            oc_ref[:, h * HD:(h + 1) * HD] = _softmax_pv(s, v).astype(BF16)


def _ctx_attn(u, qn, kn, l, caches):
    wa = 3 * NA_H * HD
    wc = C_Z - C_QC
    rows = lambda w: pl.BlockSpec((L_CTX, w), lambda b: (b, 0))
    lay = lambda w: pl.BlockSpec((None, None, L_CTX, w), lambda b: (b, l, 0, 0))
    cache_shape = lambda w: jax.ShapeDtypeStruct((B_CTX, DEPTH, L_CTX, w), F32)
    in_specs = [
        pl.BlockSpec((L_CTX, wa), lambda b: (b, 0)),
        pl.BlockSpec((L_CTX, wc), lambda b: (b, C_QC // wc)),
        pl.BlockSpec((None, 1, HD), lambda b: (l, 0, 0)),
        pl.BlockSpec((None, 1, HD), lambda b: (l, 0, 0)),
    ]
    args = [u, u, qn.reshape(DEPTH, 1, HD), kn.reshape(DEPTH, 1, HD)]
    aliases = {}
    if caches is not None:
        in_specs += [pl.BlockSpec(memory_space=pl.ANY)] * 4
        args += list(caches)
        aliases = {4 + n: 2 + n for n in range(4)}
    out = pl.pallas_call(
        _ctx_attn_kernel,
        grid=(B_CTX,),
        in_specs=in_specs,
        out_specs=[rows(BRANCH_W), rows(BRANCH_W), lay(NA_H * HD), lay(NA_H * HD), lay(GQA_KV * HD),
                   lay(GQA_KV * HD)],
        out_shape=[
            jax.ShapeDtypeStruct((T, BRANCH_W), BF16),
            jax.ShapeDtypeStruct((T, BRANCH_W), BF16),
            cache_shape(NA_H * HD), cache_shape(NA_H * HD), cache_shape(GQA_KV * HD), cache_shape(GQA_KV * HD),
        ],
        input_output_aliases=aliases,
        compiler_params=_cp(1),
        name="ctx_attn",
    )(*args)
    return out[0], out[1], tuple(out[2:])


def _rope(x, cos, sin):
    lane = lax.broadcasted_iota(jnp.int32, x.shape, 1)
    first = (lane & (HD // 2 - 1)) < (HD // 4)
    rot = jnp.where(first, -pltpu.roll(x, HD - HD // 4, 1), pltpu.roll(x, HD // 4, 1))
    return x * cos + rot * sin


def _lat_gqa_kernel(q_ref, k_ref, vl_ref, ck_ref, cv_ref, qn_ref, kn_ref, cos_ref, sin_ref, oc_in_ref, o_ref, kr_ref,
                    ve_ref):
    del oc_in_ref
    i = pl.program_id(2)
    tq = q_ref.shape[0]

    @pl.when(i == 0)
    def _():
        k = _rms(k_ref[...].astype(F32), kn_ref[...])
        kr_ref[...] = _rope(k, cos_ref[...], sin_ref[...]).astype(BF16)
        ve_ref[:, :HD] = vl_ref[...]
        ve_ref[:, HD:] = jnp.ones((L_LAT, HD), BF16)

    start = pl.multiple_of(i * tq, tq)
    cos = cos_ref[pl.ds(start, tq), :]
    sin = sin_ref[pl.ds(start, tq), :]
    kl = kr_ref[...]
    ve = ve_ref[...]
    ck = ck_ref[...]
    cve = jnp.concatenate([cv_ref[...], jnp.ones((PAST, HD), BF16)], axis=1)
    for g in range(GQA_G):
        q = _rope(_rms(q_ref[:, g * HD:(g + 1) * HD].astype(F32), qn_ref[...]), cos, sin)
        q = (q * ATT_SCALE_LOG2).astype(BF16)
        s1 = lax.dot_general(q, kl, NT_DIMS, preferred_element_type=F32)
        s2 = lax.dot_general(q, ck, NT_DIMS, preferred_element_type=F32)
        m = jnp.maximum(jnp.max(s1, axis=-1, keepdims=True), jnp.max(s2, axis=-1, keepdims=True))
        e1 = jnp.exp2(s1 - m).astype(BF16)
        e2 = jnp.exp2(s2 - m).astype(BF16)
        oe = jnp.dot(e1, ve, preferred_element_type=F32) + jnp.dot(e2, cve, preferred_element_type=F32)
        o_ref[:, g * HD:(g + 1) * HD] = (oe[:, :HD] * (1.0 / oe[:, HD:HD + 1])).astype(BF16)


def _lat_gqa(u, ck, cv, qn, kn, cos_t, sin_t, oc, l, tq=1024):
    nq = L_LAT // tq
    r0 = T_CTX // tq
    s0 = T_CTX // L_LAT
    wq = GQA_G * HD
    full = lambda b, kv, i: (0, 0)
    return pl.pallas_call(
        _lat_gqa_kernel,
        grid=(B_LAT, GQA_KV, nq),
        in_specs=[
            pl.BlockSpec((tq, wq), lambda b, kv, i: (r0 + b * nq + i, C_QC // wq + kv)),
            pl.BlockSpec((L_LAT, HD), lambda b, kv, i: (s0 + b, C_KC // HD + kv)),
            pl.BlockSpec((L_LAT, HD), lambda b, kv, i: (s0 + b, C_VC // HD + kv)),
            pl.BlockSpec((None, None, PAST, HD), lambda b, kv, i: (b, l, 0, kv)),
            pl.BlockSpec((None, None, PAST, HD), lambda b, kv, i: (b, l, 0, kv)),
            pl.BlockSpec((None, 1, HD), lambda b, kv, i: (l, 0, 0)),
            pl.BlockSpec((None, 1, HD), lambda b, kv, i: (l, 0, 0)),
            pl.BlockSpec((L_LAT, HD), full),
            pl.BlockSpec((L_LAT, HD), full),
            pl.BlockSpec(memory_space=pl.ANY),
        ],
        out_specs=pl.BlockSpec((tq, wq), lambda b, kv, i: (r0 + b * nq + i, kv)),
        out_shape=jax.ShapeDtypeStruct((T, GQA_H * HD), BF16),
        scratch_shapes=[pltpu.VMEM((L_LAT, HD), BF16), pltpu.VMEM((L_LAT, 2 * HD), BF16)],
        input_output_aliases={9: 0},
        compiler_params=_cp(3),
        name="lat_gqa",
    )(u, u, u, ck, cv, qn.reshape(DEPTH, 1, HD), kn.reshape(DEPTH, 1, HD), cos_t, sin_t, oc)


NA_QROWS = 4
NA_UROWS = 12
NA_TQ = NA_QROWS * GRID_W
NA_TK = NA_UROWS * GRID_W
NA_HP = 8


def _lat_na_kernel(q_ref, k_ref, v_ref, ck_ref, cv_ref, t2_ref, oa_in_ref, o_ref, bias_ref):
    del oa_in_ref
    j = pl.program_id(2)
    ustart = jnp.clip(NA_QROWS * j - NA_ROWS // 2, 0, GRID_R - NA_UROWS)
    left = lax.broadcasted_iota(jnp.int32, (GRID_W, 2 * GRID_W), 1) < GRID_W
    for a in range(NA_QROWS):
        r = NA_QROWS * j + a
        rs = jnp.clip(r - NA_ROWS // 2, 0, GRID_R - NA_ROWS)
        for m in range(NA_UROWS // 2):
            kr0 = ustart + 2 * m
            v0 = ((kr0 >= rs) & (kr0 < rs + NA_ROWS)).astype(jnp.int32)
            v1 = ((kr0 + 1 >= rs) & (kr0 + 1 < rs + NA_ROWS)).astype(jnp.int32)
            idx = jnp.clip(kr0 - r + NA_ROWS, 0, 2 * NA_ROWS - 1)
            ok = jnp.where(left, v0, v1) > 0
            for hh in range(NA_HP):
                bias_ref[hh, a * GRID_W:(a + 1) * GRID_W, m * 2 * GRID_W:(m + 1) * 2 * GRID_W] = jnp.where(
                    ok, t2_ref[hh, idx], NEG)
    start = pl.multiple_of(ustart * GRID_W, GRID_W)
    for hh in range(NA_HP):
        cs = slice(hh * HD, (hh + 1) * HD)
        q = q_ref[:, cs]
        kl = k_ref[pl.ds(start, NA_TK), cs]
        vl = v_ref[pl.ds(start, NA_TK), cs]
        s_loc = lax.dot_general(q, kl, NT_DIMS, preferred_element_type=F32) + bias_ref[hh]
        s_ctx = lax.dot_general(q, ck_ref[:, cs], NT_DIMS, preferred_element_type=F32)
        o_ref[:, cs] = _softmax_pv2(s_loc, vl, s_ctx, cv_ref[:, cs]).astype(BF16)


def _lat_na(u, ck, cv, t2, oa, l):
    nj = GRID_R // NA_QROWS
    r0 = T_CTX // NA_TQ
    s0 = T_CTX // L_LAT
    w = NA_HP * HD
    return pl.pallas_call(
        _lat_na_kernel,
        grid=(B_LAT, NA_H // NA_HP, nj),
        in_specs=[
            pl.BlockSpec((NA_TQ, w), lambda b, h, j: (r0 + b * nj + j, C_QA // w + h)),
            pl.BlockSpec((L_LAT, w), lambda b, h, j: (s0 + b, C_KA // w + h)),
            pl.BlockSpec((L_LAT, w), lambda b, h, j: (s0 + b, C_VA // w + h)),
            pl.BlockSpec((None, None, PAST, w), lambda b, h, j: (b, l, 0, h)),
            pl.BlockSpec((None, None, PAST, w), lambda b, h, j: (b, l, 0, h)),
            pl.BlockSpec((None, NA_HP, 2 * NA_ROWS, GRID_W, 2 * GRID_W), lambda b, h, j: (l, h, 0, 0, 0)),
            pl.BlockSpec(memory_space=pl.ANY),
        ],
        out_specs=pl.BlockSpec((NA_TQ, w), lambda b, h, j: (r0 + b * nj + j, h)),
        out_shape=jax.ShapeDtypeStruct((T, NA_H * HD), BF16),
        scratch_shapes=[pltpu.VMEM((NA_HP, NA_TQ, NA_TK), F32)],
        input_output_aliases={6: 0},
        compiler_params=_cp(3),
        name="lat_na",
    )(u, u, u, ck, cv, t2, oa)


def _na_bias_tiles(rpb):
    w = jnp.arange(GRID_W)
    cs = jnp.clip(w - NA_COLS // 2, 0, GRID_W - NA_COLS)
    cmask = (w[None, :] >= cs[:, None]) & (w[None, :] < cs[:, None] + NA_COLS)
    dc = jnp.clip(w[None, :] - w[:, None] + NA_COLS - 1, 0, 2 * NA_COLS - 2)
    tm = jnp.where(cmask, rpb[:, :, :, dc] / ATT_SCALE, NEG)
    pad = jnp.full(tm.shape[:2] + (1,) + tm.shape[3:], NEG, F32)
    lo = jnp.concatenate([pad, tm], axis=2)
    hi = jnp.concatenate([tm, pad], axis=2)
    return jnp.concatenate([lo, hi], axis=-1).astype(F32)


def _rope_tables():
    nf = HD // 4
    t = jnp.arange(L_LAT)
    inv = ROPE_BASE ** (-jnp.arange(nf, dtype=F32) / nf)
    ar = (t // GRID_W).astype(F32)[:, None] * inv[None, :]
    ac = (t % GRID_W).astype(F32)[:, None] * inv[None, :]
    cos_t = jnp.concatenate([jnp.cos(ar), jnp.cos(ar), jnp.cos(ac), jnp.cos(ac)], axis=-1)
    sin_t = jnp.concatenate([jnp.sin(ar), jnp.sin(ar), jnp.sin(ac), jnp.sin(ac)], axis=-1)
    return cos_t, sin_t


def _conv_kernel(x_ref, w_ref, b_ref, o_ref):
    x = x_ref[...].astype(F32)
    n = x.shape[0]
    row = lax.broadcasted_iota(jnp.int32, (SUBLANES, x.shape[1]), 0)
    xm = pltpu.roll(x, 1, 0)
    xm = jnp.concatenate([jnp.where(row == 0, 0.0, xm[:SUBLANES]), xm[SUBLANES:]], axis=0)
    xp = pltpu.roll(x, n - 1, 0)
    xp = jnp.concatenate([xp[:n - SUBLANES], jnp.where(row == SUBLANES - 1, 0.0, xp[n - SUBLANES:])], axis=0)
    y = w_ref[0:1, :] * xm + w_ref[1:2, :] * x + w_ref[2:3, :] * xp + b_ref[...]
    hy = 0.5 * y
    o_ref[...] = (hy * (1.0 + jnp.tanh(hy))).astype(o_ref.dtype)


def _conv(u, conv_w, conv_b, l, nseq, seq_len, row0):
    tc = 512
    nc = CONV_CH // tc
    rb0 = row0 // seq_len
    cb0 = C_XBC // tc
    return pl.pallas_call(
        _conv_kernel,
        grid=(nseq, nc),
        in_specs=[
            pl.BlockSpec((seq_len, tc), lambda b, j: (rb0 + b, cb0 + j)),
            pl.BlockSpec((None, 3, tc), lambda b, j: (l, 0, j)),
            pl.BlockSpec((None, 1, tc), lambda b, j: (l, 0, j)),
        ],
        out_specs=pl.BlockSpec((seq_len, tc), lambda b, j: (b, j)),
        out_shape=jax.ShapeDtypeStruct((nseq * seq_len, CONV_CH), F32),
        compiler_params=_cp(2),
        name="conv",
    )(u, conv_w, conv_b.reshape(DEPTH, 1, CONV_CH))


def _softplus(x):
    return jnp.maximum(x, 0.0) + jnp.log1p(jnp.exp(-jnp.abs(x)))


def _expand_heads(v, expand, terms):
    out = None
    rem = v
    for _ in range(terms):
        piece = rem.astype(BF16)
        rem = rem - piece.astype(F32)
        t = jnp.dot(piece, expand, preferred_element_type=F32)
        out = t if out is None else out + t
    return out


def _ssd_chunk(xbc, dt_col, dt_row, a_row, a_col, h_ref, reverse):
    x = xbc[:, :SSD_INNER]
    bm = xbc[:, SSD_INNER:SSD_INNER + SSD_G * SSD_N]
    cm = xbc[:, SSD_INNER + SSD_G * SSD_N:]
    li = lax.broadcasted_iota(jnp.int32, (CHUNK, CHUNK), 0)
    si = lax.broadcasted_iota(jnp.int32, (CHUNK, CHUNK), 1)
    mask = (si >= li) if reverse else (si <= li)
    tri = mask.astype(F32)
    tri_t = ((li >= si) if reverse else (li <= si)).astype(F32)
    cum_col = jnp.dot(tri, dt_col * a_row, precision=HI, preferred_element_type=F32)
    cum_row = jnp.dot(dt_row * a_col, tri_t, precision=HI, preferred_element_type=F32)
    tot_row = cum_col[0:1, :] if reverse else cum_col[CHUNK - 1:CHUNK, :]
    col = lax.broadcasted_iota(jnp.int32, (SSD_H, SSD_INNER), 1)
    lo_col = lax.broadcasted_iota(jnp.int32, (SSD_H, SSD_INNER), 0) * SSD_P
    expand = ((col >= lo_col) & (col < lo_col + SSD_P)).astype(BF16)
    w_exp = _expand_heads(jnp.exp(tot_row - cum_col) * dt_col, expand, 1)
    e_exp = _expand_heads(jnp.exp(cum_col), expand, 1)
    dec = _expand_heads(jnp.broadcast_to(jnp.exp(tot_row), (8, SSD_H)), expand, 3)[0:1, :]
    hpg = SSD_H // SSD_G
    xb = x.astype(BF16)
    bgs = [bm[:, g * SSD_N:(g + 1) * SSD_N].astype(BF16) for g in range(SSD_G)]
    cgs = [cm[:, g * SSD_N:(g + 1) * SSD_N].astype(BF16) for g in range(SSD_G)]
    gmat = [lax.dot_general(cgs[g], bgs[g], NT_DIMS, preferred_element_type=F32) for g in range(SSD_G)]
    lo = lax.broadcasted_iota(jnp.int32, (CHUNK, 2 * SSD_P), 1) < SSD_P
    zero = jnp.zeros((CHUNK, 2 * SSD_P), BF16)
    ys = []
    for pr in range(SSD_H // 2):
        g = (2 * pr) // hpg
        ms = []
        for h in (2 * pr, 2 * pr + 1):
            seg = cum_col[:, h:h + 1] - cum_row[h:h + 1, :]
            lm = jnp.exp(jnp.where(mask, seg, NEG))
            ms.append((gmat[g] * lm * dt_row[h:h + 1, :]).astype(BF16))
        xp = xb[:, pr * 2 * SSD_P:(pr + 1) * 2 * SSD_P]
        rhs = jnp.concatenate([jnp.where(lo, xp, zero), jnp.where(lo, zero, xp)], axis=0)
        ys.append(jnp.dot(jnp.concatenate(ms, axis=1), rhs, preferred_element_type=F32))
    y = jnp.concatenate(ys, axis=1)
    h_t = h_ref[...]
    hb = h_t.astype(BF16)
    xw = (x * w_exp).astype(BF16)
    wpg = SSD_INNER // SSD_G
    yo = [jnp.dot(cgs[g], hb[:, g * wpg:(g + 1) * wpg], preferred_element_type=F32) for g in range(SSD_G)]
    st = [lax.dot_general(bgs[g], xw[:, g * wpg:(g + 1) * wpg], TN_DIMS, preferred_element_type=F32)
          for g in range(SSD_G)]
    h_ref[...] = h_t * dec + jnp.concatenate(st, axis=1)
    return y + jnp.concatenate(yo, axis=1) * e_exp


SSD_STEP_CHUNKS = 4


def _ssd_kernel(xf_ref, xb_ref, dtf_ref, dtb_ref, dttf_ref, dttb_ref, prow_ref, pcol_ref, dtot_ref, *rest,
                context):
    hf_ref, hb_ref = rest[-2:]
    if context:
        yf_ref, yb_ref, hfin_ref = rest[-5:-2]
    else:
        h0_ref, yf_ref, yb_ref = rest[-5:-2]
    c = pl.program_id(1)

    @pl.when(c == 0)
    def _():
        if context:
            hf_ref[...] = jnp.zeros_like(hf_ref)
            hb_ref[...] = jnp.zeros_like(hb_ref)
        else:
            hf_ref[...] = h0_ref[0].T
            hb_ref[...] = h0_ref[1].T

    prow = prow_ref[...]
    pcol = pcol_ref[...]
    step_chunks = xf_ref.shape[0] // CHUNK
    for s in range(step_chunks):
        r = slice(s * CHUNK, (s + 1) * CHUNK)
        xf = xf_ref[r, :]
        dt_col = _softplus(dtf_ref[r, 0:SSD_H] + prow[0:1, :])
        dt_row = _softplus(dttf_ref[0:SSD_H, r] + pcol[:, 0:1])
        yf = _ssd_chunk(xf, dt_col, dt_row, -jnp.exp(prow[2:3, :]), -jnp.exp(pcol[:, 2:3]), hf_ref, False)
        yf_ref[r, :] = (yf + dtot_ref[...] * xf[:, :SSD_INNER]).astype(yf_ref.dtype)
    for s in reversed(range(step_chunks)):
        r = slice(s * CHUNK, (s + 1) * CHUNK)
        dt_col = _softplus(dtb_ref[r, SSD_H:2 * SSD_H] + prow[1:2, :])
        dt_row = _softplus(dttb_ref[SSD_H:2 * SSD_H, r] + pcol[:, 1:2])
        yb_ref[r, :] = _ssd_chunk(xb_ref[r, :], dt_col, dt_row, -jnp.exp(prow[3:4, :]), -jnp.exp(pcol[:, 3:4]),
                                  hb_ref, True).astype(yb_ref.dtype)

    if context:
        @pl.when(c == pl.num_programs(1) - 1)
        def _():
            hfin_ref[0] = hf_ref[...].T
            hfin_ref[1] = hb_ref[...].T


def _ssd(xbc, u_dt, dt_t, prow, pcol, dtot, l, nseq, seq_len, row0, states, context):
    blk = min(SSD_STEP_CHUNKS * CHUNK, seq_len)
    nc = seq_len // blk
    cb0 = row0 // blk
    rows = nseq * seq_len
    fwd = lambda b, c: b * nc + c
    bwd = lambda b, c: b * nc + (nc - 1 - c)
    state_spec = pl.BlockSpec((None, None, 2, SSD_INNER, SSD_N), lambda b, c: (b, l, 0, 0, 0))
    in_specs = [
        pl.BlockSpec((blk, CONV_CH), lambda b, c: (fwd(b, c), 0)),
        pl.BlockSpec((blk, CONV_CH), lambda b, c: (bwd(b, c), 0)),
        pl.BlockSpec((blk, DT_PAD), lambda b, c: (cb0 + fwd(b, c), 0)),
        pl.BlockSpec((blk, DT_PAD), lambda b, c: (cb0 + bwd(b, c), 0)),
        pl.BlockSpec((DT_PAD, blk), lambda b, c: (0, cb0 + fwd(b, c))),
        pl.BlockSpec((DT_PAD, blk), lambda b, c: (0, cb0 + bwd(b, c))),
        pl.BlockSpec((4, SSD_H), lambda b, c: (0, 0)),
        pl.BlockSpec((SSD_H, 4), lambda b, c: (0, 0)),
        pl.BlockSpec((1, SSD_INNER), lambda b, c: (0, 0)),
    ]
    args = [xbc, xbc, u_dt, u_dt, dt_t, dt_t, prow, pcol, dtot]
    out_specs = [
        pl.BlockSpec((blk, SSD_INNER), lambda b, c: (fwd(b, c), 0)),
        pl.BlockSpec((blk, SSD_INNER), lambda b, c: (bwd(b, c), 0)),
    ]
    out_shape = [jax.ShapeDtypeStruct((rows, SSD_INNER), BF16), jax.ShapeDtypeStruct((rows, SSD_INNER), BF16)]
    aliases = {}
    if context:
        out_specs.append(state_spec)
        out_shape.append(jax.ShapeDtypeStruct((nseq, DEPTH, 2, SSD_INNER, SSD_N), F32))
        if states is not None:
            in_specs.append(pl.BlockSpec(memory_space=pl.ANY))
            args.append(states)
            aliases = {len(args) - 1: 2}
    else:
        in_specs.append(state_spec)
        args.append(states)
    return pl.pallas_call(
        functools.partial(_ssd_kernel, context=context),
        grid=(nseq, nc),
        in_specs=in_specs,
        out_specs=out_specs,
        out_shape=out_shape,
        scratch_shapes=[pltpu.VMEM((SSD_N, SSD_INNER), F32), pltpu.VMEM((SSD_N, SSD_INNER), F32)],
        input_output_aliases=aliases,
        compiler_params=_cp(2),
        name="ssd",
    )(*args)


def _ssd_gate_kernel(yf_ref, yb_ref, z_ref, g_ref, *rest):
    o_ref = rest[-1]
    y = (yf_ref[...].astype(F32) + yb_ref[...].astype(F32)) * _silu(z_ref[...].astype(F32))
    o_ref[...] = _rms(y, g_ref[...]).astype(BF16)


def _ssd_gate(yf, yb, u, norm_w, l, row0, ob=None):
    tm = ROW_TILE
    rows = yf.shape[0]
    rb0 = row0 // tm
    row = pl.BlockSpec((tm, SSD_INNER), lambda i: (i, 0))
    in_specs = [
        row, row,
        pl.BlockSpec((pl.Element(tm), pl.Element(SSD_INNER)), lambda i: (tm * (rb0 + i), C_Z)),
        pl.BlockSpec((None, 1, SSD_INNER), lambda i: (l, 0, 0)),
    ]
    args = [yf, yb, u, norm_w.reshape(DEPTH, 1, SSD_INNER)]
    aliases = {}
    if ob is not None:
        in_specs.append(pl.BlockSpec(memory_space=pl.ANY))
        args.append(ob)
        aliases = {4: 0}
    return pl.pallas_call(
        _ssd_gate_kernel,
        grid=(rows // tm,),
        in_specs=in_specs,
        out_specs=pl.BlockSpec((tm, SSD_INNER), lambda i: (rb0 + i, 0)),
        out_shape=jax.ShapeDtypeStruct((T, SSD_INNER), BF16),
        input_output_aliases=aliases,
        compiler_params=_cp(1),
        name="ssd_gate",
    )(*args)


def kernel(x_prompt, x_sample, cache_na_k, cache_na_v, cache_gqa_k, cache_gqa_v, state_ssd, c, c_ctx, ada_w, ada_b, norm_mix_pre, norm_mix_post, norm_ffn_pre, norm_ffn_post, w_in, gate_b, na_rpb, gqa_q_norm, gqa_k_norm, ssd_conv_w, ssd_conv_b, ssd_dt_bias, ssd_a_log, ssd_d, ssd_norm_w, w_branch, w_out, ffn_w_up, ffn_w_down):
    cvec = jnp.concatenate([c_ctx[None, :], c, jnp.zeros((8 - 1 - B_LAT, D), F32)], axis=0)
    mods = _compute_mods(cvec, ada_w, ada_b).reshape(DEPTH, 8, 6, 1, D)
    wt_in = jnp.swapaxes(w_in, 1, 2)
    t2 = _na_bias_tiles(na_rpb)
    cos_t, sin_t = _rope_tables()
    ck_a = cache_na_k.reshape(B_LAT, DEPTH, PAST, NA_H * HD).astype(BF16)
    cv_a = cache_na_v.reshape(B_LAT, DEPTH, PAST, NA_H * HD).astype(BF16)
    ck_c = cache_gqa_k.reshape(B_LAT, DEPTH, PAST, GQA_KV * HD).astype(BF16)
    cv_c = cache_gqa_v.reshape(B_LAT, DEPTH, PAST, GQA_KV * HD).astype(BF16)
    h0_lat = state_ssd.reshape(B_LAT, DEPTH, 2, SSD_INNER, SSD_N)
    prow = jnp.concatenate([ssd_dt_bias, ssd_a_log], axis=1)
    pcol = prow.transpose(0, 2, 1)
    dtot = jnp.repeat(ssd_d[:, 0] + ssd_d[:, 1], SSD_P, axis=-1)[:, None, :]

    x, h = _normmod(x_prompt.reshape(T_CTX, D), x_sample.reshape(T_LAT, D), norm_mix_pre, mods, 0, 0, 1)
    caches = None
    ssd_fin = None
    for l in range(DEPTH):
        u, u_dt, dt_t = _in_proj(h, wt_in, l)
        oa, oc, caches = _ctx_attn(u, gqa_q_norm, gqa_k_norm, l, caches)
        oc = _lat_gqa(u, ck_c, cv_c, gqa_q_norm, gqa_k_norm, cos_t, sin_t, oc, l)
        oa = _lat_na(u, ck_a, cv_a, t2, oa, l)
        xbc_c = _conv(u, ssd_conv_w, ssd_conv_b, l, B_CTX, L_CTX, 0)
        xbc_l = _conv(u, ssd_conv_w, ssd_conv_b, l, B_LAT, L_LAT, T_CTX)
        yf_c, yb_c, ssd_fin = _ssd(xbc_c, u_dt, dt_t, prow[l], pcol[l], dtot[l], l, B_CTX, L_CTX, 0, ssd_fin, True)
        yf_l, yb_l = _ssd(xbc_l, u_dt, dt_t, prow[l], pcol[l], dtot[l], l, B_LAT, L_LAT, T_CTX, h0_lat, False)
        ob = _ssd_gate(yf_c, yb_c, u, ssd_norm_w, l, 0)
        ob = _ssd_gate(yf_l, yb_l, u, ssd_norm_w, l, T_CTX, ob)
        mix = _merge(h, oa, ob, oc, wt_in, w_branch, gate_b, l)
        x, h2 = _proj_resid(mix, w_out, l, x, norm_mix_post, mods, l, 2, 512, (norm_ffn_pre, l, 3, 4), "out_proj")
        act, w_down = _matmul_swiglu(h2, ffn_w_up, ffn_w_down, l)
        nxt = (norm_mix_pre, l + 1, 0, 1) if l + 1 < DEPTH else None
        x, h = _proj_resid(act, w_down, 0, x, norm_ffn_post, mods, l, 5, 256, nxt, "ffn_down")
    na_k, na_v, gqa_k, gqa_v = caches
    return (x.reshape(B_CTX, L_CTX, D), h.reshape(B_LAT, L_LAT, D),
            na_k.reshape(B_CTX, DEPTH, L_CTX, NA_H, HD), na_v.reshape(B_CTX, DEPTH, L_CTX, NA_H, HD),
            gqa_k.reshape(B_CTX, DEPTH, L_CTX, GQA_KV, HD), gqa_v.reshape(B_CTX, DEPTH, L_CTX, GQA_KV, HD),
            ssd_fin.reshape(B_CTX, DEPTH, 2, SSD_H, SSD_P, SSD_N))
```

```python
import functools
import math

import jax
import jax.numpy as jnp
from jax import lax
from jax.experimental import pallas as pl
from jax.experimental.pallas import tpu as pltpu

F32 = jnp.float32
BF16 = jnp.bfloat16
HI = lax.Precision.HIGHEST
EPS = 1e-6
NEG = -1e30

D = 2048
DEPTH = 4
B_CTX, L_CTX = 16, 256
B_LAT, L_LAT = 4, 2048
PAST = 512
GRID_W = 64
GRID_R = L_LAT // GRID_W
NA_H = 8
HD = 128
NA_ROWS, NA_COLS = 8, 16
GQA_H, GQA_KV = 8, 2
GQA_G = GQA_H // GQA_KV
ROPE_BASE = 10000.0
SSD_INNER = 1024
SSD_P = 64
SSD_H = SSD_INNER // SSD_P
SSD_N = 128
SSD_G = 2
CHUNK = 128
CONV_CH = SSD_INNER + 2 * SSD_G * SSD_N
BRANCH_W = 1024
FFN_H = 5632
T_CTX = B_CTX * L_CTX
T_LAT = B_LAT * L_LAT
T = T_CTX + T_LAT

C_QA, C_KA, C_VA = 0, NA_H * HD, 2 * NA_H * HD
C_QC = 3 * NA_H * HD
C_KC = C_QC + GQA_H * HD
C_VC = C_KC + GQA_KV * HD
C_Z = C_VC + GQA_KV * HD
C_XBC = C_Z + SSD_INNER
C_DT = C_XBC + CONV_CH
C_GL = C_DT + 2 * SSD_H
N_MAIN = C_DT

NT_DIMS = (((1,), (1,)), ((), ()))
TN_DIMS = (((0,), (0,)), ((), ()))

ROW_TILE = 256
SUBLANES = 8
VMEM_BIG = 56


def _cp(n_axes, vmem_mb=None):
    kw = dict(dimension_semantics=("arbitrary",) * n_axes)
    if vmem_mb is not None:
        kw["vmem_limit_bytes"] = vmem_mb * 1024 * 1024
    return pltpu.CompilerParams(**kw)


def _mod_row(i, tm):
    n_ctx = T_CTX // tm
    n_lat = L_LAT // tm
    return jnp.where(i < n_ctx, 0, 1 + (i - n_ctx) // n_lat)


def _rms(x, g):
    return x * lax.rsqrt(jnp.mean(x * x, axis=-1, keepdims=True) + EPS) * g


def _silu(x):
    return x * jax.nn.sigmoid(x)


def _mods_kernel(c_ref, w_ref, b_ref, o_ref):
    s = _silu(c_ref[...]).astype(BF16)
    o_ref[...] = jnp.dot(s, w_ref[...].astype(BF16), preferred_element_type=F32) + b_ref[...]


def _compute_mods(cvec, ada_w, ada_b):
    tn = 1024
    n = 6 * D
    return pl.pallas_call(
        _mods_kernel,
        grid=(DEPTH, n // tn),
        in_specs=[
            pl.BlockSpec((8, D), lambda l, j: (0, 0)),
            pl.BlockSpec((None, D, tn), lambda l, j: (l, 0, j)),
            pl.BlockSpec((None, 1, tn), lambda l, j: (l, 0, j)),
        ],
        out_specs=pl.BlockSpec((None, 8, tn), lambda l, j: (l, 0, j)),
        out_shape=jax.ShapeDtypeStruct((DEPTH, 8, n), F32),
        compiler_params=_cp(2),
        name="mods",
    )(cvec, ada_w, ada_b.reshape(DEPTH, 1, n))


def _mod_spec(l, k, tm):
    return pl.BlockSpec((None, None, None, 1, D), lambda i: (l, _mod_row(i, tm), k, 0, 0))


def _gain_spec(l):
    return pl.BlockSpec((None, 1, D), lambda i: (l, 0, 0))


def _normmod_kernel(xc_ref, xl_ref, g_ref, sh_ref, sc_ref, x_ref, h_ref):
    x = jnp.where(pl.program_id(0) < T_CTX // ROW_TILE, xc_ref[...], xl_ref[...])
    x_ref[...] = x
    h_ref[...] = (_rms(x, g_ref[...]) * (1.0 + sc_ref[...]) + sh_ref[...]).astype(BF16)


def _normmod(x_ctx, x_lat, gains, mods, l, k_sh, k_sc):
    tm = ROW_TILE
    n_ctx = T_CTX // tm
    row = pl.BlockSpec((tm, D), lambda i: (i, 0))
    return pl.pallas_call(
        _normmod_kernel,
        grid=(T // tm,),
        in_specs=[
            pl.BlockSpec((tm, D), lambda i: (jnp.minimum(i, n_ctx - 1), 0)),
            pl.BlockSpec((tm, D), lambda i: (jnp.maximum(i - n_ctx, 0), 0)),
            _gain_spec(l),
            _mod_spec(l, k_sh, tm),
            _mod_spec(l, k_sc, tm),
        ],
        out_specs=[row, row],
        out_shape=[jax.ShapeDtypeStruct((T, D), F32), jax.ShapeDtypeStruct((T, D), BF16)],
        compiler_params=_cp(1),
        name="normmod",
    )(x_ctx, x_lat, gains.reshape(DEPTH, 1, D), mods, mods)


DT_PAD = 128


def _in_proj_kernel(x_ref, wt_ref, wdt_ref, o_ref, odt_ref, odtt_ref, wb_ref):
    @pl.when(pl.program_id(1) == 0)
    def _():
        wb_ref[...] = wt_ref[...].astype(BF16).T

    x = x_ref[...]
    o_ref[...] = jnp.dot(x, wb_ref[...], preferred_element_type=F32).astype(o_ref.dtype)

    @pl.when(pl.program_id(0) == pl.num_programs(0) - 1)
    def _():
        dt = lax.dot_general(x, wdt_ref[...].astype(BF16), NT_DIMS, preferred_element_type=F32)
        odt_ref[...] = dt
        odtt_ref[...] = dt.T


def _in_proj(x, wt, l, tn=1024, tm=1024):
    m, k = x.shape
    nj = N_MAIN // tn
    last = lambda j, i: jnp.where(j == nj - 1, i, 0)
    return pl.pallas_call(
        _in_proj_kernel,
        grid=(nj, m // tm),
        in_specs=[
            pl.BlockSpec((tm, k), lambda j, i: (i, 0)),
            pl.BlockSpec((None, tn, k), lambda j, i: (l, j, 0)),
            pl.BlockSpec((None, DT_PAD, k), lambda j, i: (l, C_DT // DT_PAD, 0)),
        ],
        out_specs=[
            pl.BlockSpec((tm, tn), lambda j, i: (i, j)),
            pl.BlockSpec((tm, DT_PAD), lambda j, i: (last(j, i), 0)),
            pl.BlockSpec((DT_PAD, tm), lambda j, i: (0, last(j, i))),
        ],
        out_shape=[
            jax.ShapeDtypeStruct((m, N_MAIN), BF16),
            jax.ShapeDtypeStruct((m, DT_PAD), F32),
            jax.ShapeDtypeStruct((DT_PAD, m), F32),
        ],
        scratch_shapes=[pltpu.VMEM((k, tn), BF16)],
        compiler_params=_cp(2, 40),
        name="in_proj",
    )(x, wt, wt)


WD_COLS = 256


def _mm_swiglu_kernel(x_ref, wg_ref, wu_ref, wd_ref, o_ref, owd_ref, wgb_ref, wub_ref):
    @pl.when(pl.program_id(1) == 0)
    def _():
        wgb_ref[...] = wg_ref[...].astype(BF16)
        wub_ref[...] = wu_ref[...].astype(BF16)

    x = x_ref[...]
    g = jnp.dot(x, wgb_ref[...], preferred_element_type=F32)
    u = jnp.dot(x, wub_ref[...], preferred_element_type=F32)
    o_ref[...] = (_silu(g) * u).astype(o_ref.dtype)

    @pl.when(pl.program_id(1) < D // WD_COLS)
    def _():
        owd_ref[...] = wd_ref[...].astype(BF16)


def _matmul_swiglu(x, w, w_down, l, tn=512, tm=1024):
    m, k = x.shape
    nb = FFN_H // tn
    ncb = D // WD_COLS
    assert m // tm >= ncb
    wd_idx = lambda j, i: (j, jnp.minimum(i, ncb - 1))
    return pl.pallas_call(
        _mm_swiglu_kernel,
        grid=(nb, m // tm),
        in_specs=[
            pl.BlockSpec((tm, k), lambda j, i: (i, 0)),
            pl.BlockSpec((None, k, tn), lambda j, i: (l, 0, j)),
            pl.BlockSpec((None, k, tn), lambda j, i: (l, 0, nb + j)),
            pl.BlockSpec((None, tn, WD_COLS), lambda j, i: (l,) + wd_idx(j, i)),
        ],
        out_specs=[
            pl.BlockSpec((tm, tn), lambda j, i: (i, j)),
            pl.BlockSpec((None, tn, WD_COLS), lambda j, i: (0,) + wd_idx(j, i)),
        ],
        out_shape=[jax.ShapeDtypeStruct((m, FFN_H), BF16), jax.ShapeDtypeStruct((1, FFN_H, D), BF16)],
        scratch_shapes=[pltpu.VMEM((k, tn), BF16), pltpu.VMEM((k, tn), BF16)],
        compiler_params=_cp(2, 40),
        name="ffn_up",
    )(x, w, w, w_down)


def _merge_kernel(h_ref, oa_ref, ob_ref, oc_ref, wg0, wg1, wg2, wb0, wb1, wb2, gb0, gb1, gb2, o_ref, wgb_ref, wbb_ref):
    wg = (wg0, wg1, wg2)
    wb = (wb0, wb1, wb2)
    gb = (gb0, gb1, gb2)

    @pl.when(pl.program_id(1) == 0)
    def _():
        for g in range(3):
            wgb_ref[g] = wg[g][0].astype(BF16).T
            wbb_ref[g] = wb[g][...].astype(BF16)

    h = h_ref[...]
    acc = None
    for g, o_r in enumerate((oa_ref, ob_ref, oc_ref)):
        gl = jnp.dot(h, wgb_ref[g], preferred_element_type=F32) + gb[g][...]
        p = jnp.dot(o_r[...], wbb_ref[g], preferred_element_type=F32)
        t = jax.nn.sigmoid(gl) * p
        acc = t if acc is None else acc + t
    o_ref[...] = acc.astype(BF16)


def _merge(h, oa, ob, oc, wt_in, w_branch, gate_b, l, tn=512, tm=512):
    nb = D // tn
    xrow = pl.BlockSpec((tm, D), lambda j, i: (i, 0))
    orow = pl.BlockSpec((tm, BRANCH_W), lambda j, i: (i, 0))
    once = pl.Buffered(1)
    assert C_GL % SUBLANES == 0 and tn % SUBLANES == 0

    def wg_spec(g):
        row0 = (C_GL + g * D) // SUBLANES
        return pl.BlockSpec((pl.Element(1), pl.Element(tn), pl.Element(D)),
                            lambda j, i: (l, SUBLANES * (row0 + j * (tn // SUBLANES)), 0), pipeline_mode=once)

    def wb_spec(g):
        return pl.BlockSpec((None, None, BRANCH_W, tn), lambda j, i: (l, g, 0, j), pipeline_mode=once)

    def gb_spec(g):
        return pl.BlockSpec((None, 1, tn), lambda j, i: (l, 0, g * nb + j))

    gate_b3 = gate_b.reshape(DEPTH, 1, 3 * D)
    return pl.pallas_call(
        _merge_kernel,
        grid=(nb, T // tm),
        in_specs=[xrow, orow, orow, orow]
        + [wg_spec(g) for g in range(3)]
        + [wb_spec(g) for g in range(3)]
        + [gb_spec(g) for g in range(3)],
        out_specs=pl.BlockSpec((tm, tn), lambda j, i: (i, j)),
        out_shape=jax.ShapeDtypeStruct((T, D), BF16),
        scratch_shapes=[pltpu.VMEM((3, D, tn), BF16), pltpu.VMEM((3, BRANCH_W, tn), BF16)],
        compiler_params=_cp(2, 48),
        name="merge",
    )(h, oa, ob, oc, wt_in, wt_in, wt_in, w_branch, w_branch, w_branch, gate_b3, gate_b3, gate_b3)


EPILOGUE_ROWS = 256


def _proj_resid_kernel(a_ref, w_ref, x_ref, gpost_ref, gate_ref, *rest, with_h, cast_w):
    if cast_w:
        wb_ref = rest[-1]
        rest = rest[:-1]

        @pl.when(pl.program_id(0) == 0)
        def _():
            wb_ref[...] = w_ref[...].astype(BF16)

        w_src = wb_ref
    else:
        w_src = w_ref
    tm = a_ref.shape[0]
    if with_h and tm > EPILOGUE_ROWS:
        gpre_ref, sh_ref, sc_ref, xo_ref, h_ref = rest
        for s in range(tm // EPILOGUE_ROWS):
            r = slice(s * EPILOGUE_ROWS, (s + 1) * EPILOGUE_ROWS)
            y = jnp.dot(a_ref[r, :], w_src[...], preferred_element_type=F32)
            xn = x_ref[r, :] + gate_ref[...] * _rms(y, gpost_ref[...])
            xo_ref[r, :] = xn
            h_ref[r, :] = (_rms(xn, gpre_ref[...]) * (1.0 + sc_ref[...]) + sh_ref[...]).astype(BF16)
        return
    y = jnp.dot(a_ref[...], w_src[...], preferred_element_type=F32)
    xn = x_ref[...] + gate_ref[...] * _rms(y, gpost_ref[...])
    if with_h:
        gpre_ref, sh_ref, sc_ref, xo_ref, h_ref = rest
        xo_ref[...] = xn
        h_ref[...] = (_rms(xn, gpre_ref[...]) * (1.0 + sc_ref[...]) + sh_ref[...]).astype(BF16)
    else:
        oc_ref, ol_ref = rest
        n_ctx = T_CTX // xn.shape[0]

        @pl.when(pl.program_id(0) < n_ctx)
        def _():
            oc_ref[...] = xn

        @pl.when(pl.program_id(0) >= n_ctx)
        def _():
            ol_ref[...] = xn


def _proj_resid(a, w, wl, x, post_gains, mods, l, k_gate, tm, nxt, name):
    k = a.shape[1]
    cast_w = w.dtype != BF16
    row = pl.BlockSpec((tm, D), lambda i: (i, 0))
    in_specs = [
        pl.BlockSpec((tm, k), lambda i: (i, 0)),
        pl.BlockSpec((None, k, D), lambda i: (wl, 0, 0), pipeline_mode=pl.Buffered(1)),
        row, _gain_spec(l), _mod_spec(l, k_gate, tm),
    ]
    args = [a, w, x, post_gains.reshape(DEPTH, 1, D), mods]
    out_specs = [row]
    out_shape = [jax.ShapeDtypeStruct((T, D), F32)]
    if nxt is not None:
        gains, ln, k_sh, k_sc = nxt
        in_specs += [_gain_spec(ln), _mod_spec(ln, k_sh, tm), _mod_spec(ln, k_sc, tm)]
        args += [gains.reshape(DEPTH, 1, D), mods, mods]
        out_specs.append(row)
        out_shape.append(jax.ShapeDtypeStruct((T, D), BF16))
    else:
        n_ctx = T_CTX // tm
        out_specs = [pl.BlockSpec((tm, D), lambda i: (jnp.minimum(i, n_ctx - 1), 0)),
                     pl.BlockSpec((tm, D), lambda i: (jnp.maximum(i - n_ctx, 0), 0))]
        out_shape = [jax.ShapeDtypeStruct((T_CTX, D), F32), jax.ShapeDtypeStruct((T_LAT, D), F32)]
    out = pl.pallas_call(
        functools.partial(_proj_resid_kernel, with_h=nxt is not None, cast_w=cast_w),
        grid=(T // tm,),
        in_specs=in_specs,
        out_specs=out_specs,
        out_shape=out_shape,
        scratch_shapes=[pltpu.VMEM((k, D), BF16)] if cast_w else [],
        compiler_params=_cp(1, VMEM_BIG),
        name=name,
    )(*args)
    return out


ATT_SCALE = HD ** -0.5
ATT_SCALE_LOG2 = ATT_SCALE * math.log2(math.e)


def _softmax_pv(s, v):
    m = jnp.max(s, axis=-1, keepdims=True) * ATT_SCALE_LOG2
    e = jnp.exp2(s * ATT_SCALE_LOG2 - m)
    o = jnp.dot(e.astype(BF16), v, preferred_element_type=F32)
    return o * (1.0 / jnp.sum(e, axis=-1, keepdims=True))


def _softmax_pv2(s1, v1, s2, v2):
    m = jnp.maximum(jnp.max(s1, axis=-1, keepdims=True), jnp.max(s2, axis=-1, keepdims=True)) * ATT_SCALE_LOG2
    e1 = jnp.exp2(s1 * ATT_SCALE_LOG2 - m)
    e2 = jnp.exp2(s2 * ATT_SCALE_LOG2 - m)
    tot = jnp.sum(e1, axis=-1, keepdims=True) + jnp.sum(e2, axis=-1, keepdims=True)
    o = jnp.dot(e1.astype(BF16), v1, preferred_element_type=F32) + jnp.dot(e2.astype(BF16), v2,
                                                                           preferred_element_type=F32)
    return o * (1.0 / tot)


def _ctx_attn_kernel(a_ref, c_ref, qn_ref, kn_ref, *rest):
    oa_ref, oc_ref, nk_ref, nv_ref, kcn_ref, vcn_ref = rest[-6:]
    nk_ref[...] = a_ref[:, C_KA:C_KA + NA_H * HD].astype(F32)
    nv_ref[...] = a_ref[:, C_VA:C_VA + NA_H * HD].astype(F32)
    for h in range(NA_H):
        q = a_ref[:, C_QA + h * HD:C_QA + (h + 1) * HD]
        k = a_ref[:, C_KA + h * HD:C_KA + (h + 1) * HD]
        v = a_ref[:, C_VA + h * HD:C_VA + (h + 1) * HD]
        s = lax.dot_general(q, k, NT_DIMS, preferred_element_type=F32)
        oa_ref[:, h * HD:(h + 1) * HD] = _softmax_pv(s, v).astype(BF16)
    kc0 = C_KC - C_QC
    vc0 = C_VC - C_QC
    vcn_ref[...] = c_ref[:, vc0:vc0 + GQA_KV * HD].astype(F32)
    for kv in range(GQA_KV):
        k = _rms(c_ref[:, kc0 + kv * HD:kc0 + (kv + 1) * HD].astype(F32), kn_ref[...])
        kcn_ref[:, kv * HD:(kv + 1) * HD] = k
        kb = k.astype(BF16)
        v = c_ref[:, vc0 + kv * HD:vc0 + (kv + 1) * HD]
        for g in range(GQA_G):
            h = kv * GQA_G + g
            q = _rms(c_ref[:, h * HD:(h + 1) * HD].astype(F32), qn_ref[...]).astype(BF16)
            s = lax.dot_general(q, kb, NT_DIMS, preferred_element_type=F32)
            oc_ref[:, h * HD:(h + 1) * HD] = _softmax_pv(s, v).astype(BF16)


def _ctx_attn(u, qn, kn, l, caches):
    wa = 3 * NA_H * HD
    wc = C_Z - C_QC
    rows = lambda w: pl.BlockSpec((L_CTX, w), lambda b: (b, 0))
    lay = lambda w: pl.BlockSpec((None, None, L_CTX, w), lambda b: (b, l, 0, 0))
    cache_shape = lambda w: jax.ShapeDtypeStruct((B_CTX, DEPTH, L_CTX, w), F32)
    in_specs = [
        pl.BlockSpec((L_CTX, wa), lambda b: (b, 0)),
        pl.BlockSpec((L_CTX, wc), lambda b: (b, C_QC // wc)),
        pl.BlockSpec((None, 1, HD), lambda b: (l, 0, 0)),
        pl.BlockSpec((None, 1, HD), lambda b: (l, 0, 0)),
    ]
    args = [u, u, qn.reshape(DEPTH, 1, HD), kn.reshape(DEPTH, 1, HD)]
    aliases = {}
    if caches is not None:
        in_specs += [pl.BlockSpec(memory_space=pl.ANY)] * 4
        args += list(caches)
        aliases = {4 + n: 2 + n for n in range(4)}
    out = pl.pallas_call(
        _ctx_attn_kernel,
        grid=(B_CTX,),
        in_specs=in_specs,
        out_specs=[rows(BRANCH_W), rows(BRANCH_W), lay(NA_H * HD), lay(NA_H * HD), lay(GQA_KV * HD),
                   lay(GQA_KV * HD)],
        out_shape=[
            jax.ShapeDtypeStruct((T, BRANCH_W), BF16),
            jax.ShapeDtypeStruct((T, BRANCH_W), BF16),
            cache_shape(NA_H * HD), cache_shape(NA_H * HD), cache_shape(GQA_KV * HD), cache_shape(GQA_KV * HD),
        ],
        input_output_aliases=aliases,
        compiler_params=_cp(1),
        name="ctx_attn",
    )(*args)
    return out[0], out[1], tuple(out[2:])


def _rope(x, cos, sin):
    lane = lax.broadcasted_iota(jnp.int32, x.shape, 1)
    first = (lane & (HD // 2 - 1)) < (HD // 4)
    rot = jnp.where(first, -pltpu.roll(x, HD - HD // 4, 1), pltpu.roll(x, HD // 4, 1))
    return x * cos + rot * sin


def _lat_gqa_kernel(q_ref, k_ref, vl_ref, ck_ref, cv_ref, qn_ref, kn_ref, cos_ref, sin_ref, oc_in_ref, o_ref, kr_ref,
                    ve_ref):
    del oc_in_ref
    i = pl.program_id(2)
    tq = q_ref.shape[0]

    @pl.when(i == 0)
    def _():
        k = _rms(k_ref[...].astype(F32), kn_ref[...])
        kr_ref[...] = _rope(k, cos_ref[...], sin_ref[...]).astype(BF16)
        ve_ref[:, :HD] = vl_ref[...]
        ve_ref[:, HD:] = jnp.ones((L_LAT, HD), BF16)

    start = pl.multiple_of(i * tq, tq)
    cos = cos_ref[pl.ds(start, tq), :]
    sin = sin_ref[pl.ds(start, tq), :]
    kl = kr_ref[...]
    ve = ve_ref[...]
    ck = ck_ref[...]
    cve = jnp.concatenate([cv_ref[...], jnp.ones((PAST, HD), BF16)], axis=1)
    for g in range(GQA_G):
        q = _rope(_rms(q_ref[:, g * HD:(g + 1) * HD].astype(F32), qn_ref[...]), cos, sin)
        q = (q * ATT_SCALE_LOG2).astype(BF16)
        s1 = lax.dot_general(q, kl, NT_DIMS, preferred_element_type=F32)
        s2 = lax.dot_general(q, ck, NT_DIMS, preferred_element_type=F32)
        m = jnp.maximum(jnp.max(s1, axis=-1, keepdims=True), jnp.max(s2, axis=-1, keepdims=True))
        e1 = jnp.exp2(s1 - m).astype(BF16)
        e2 = jnp.exp2(s2 - m).astype(BF16)
        oe = jnp.dot(e1, ve, preferred_element_type=F32) + jnp.dot(e2, cve, preferred_element_type=F32)
        o_ref[:, g * HD:(g + 1) * HD] = (oe[:, :HD] * (1.0 / oe[:, HD:HD + 1])).astype(BF16)


def _lat_gqa(u, ck, cv, qn, kn, cos_t, sin_t, oc, l, tq=1024):
    nq = L_LAT // tq
    r0 = T_CTX // tq
    s0 = T_CTX // L_LAT
    wq = GQA_G * HD
    full = lambda b, kv, i: (0, 0)
    return pl.pallas_call(
        _lat_gqa_kernel,
        grid=(B_LAT, GQA_KV, nq),
        in_specs=[
            pl.BlockSpec((tq, wq), lambda b, kv, i: (r0 + b * nq + i, C_QC // wq + kv)),
            pl.BlockSpec((L_LAT, HD), lambda b, kv, i: (s0 + b, C_KC // HD + kv)),
            pl.BlockSpec((L_LAT, HD), lambda b, kv, i: (s0 + b, C_VC // HD + kv)),
            pl.BlockSpec((None, None, PAST, HD), lambda b, kv, i: (b, l, 0, kv)),
            pl.BlockSpec((None, None, PAST, HD), lambda b, kv, i: (b, l, 0, kv)),
            pl.BlockSpec((None, 1, HD), lambda b, kv, i: (l, 0, 0)),
            pl.BlockSpec((None, 1, HD), lambda b, kv, i: (l, 0, 0)),
            pl.BlockSpec((L_LAT, HD), full),
            pl.BlockSpec((L_LAT, HD), full),
            pl.BlockSpec(memory_space=pl.ANY),
        ],
        out_specs=pl.BlockSpec((tq, wq), lambda b, kv, i: (r0 + b * nq + i, kv)),
        out_shape=jax.ShapeDtypeStruct((T, GQA_H * HD), BF16),
        scratch_shapes=[pltpu.VMEM((L_LAT, HD), BF16), pltpu.VMEM((L_LAT, 2 * HD), BF16)],
        input_output_aliases={9: 0},
        compiler_params=_cp(3),
        name="lat_gqa",
    )(u, u, u, ck, cv, qn.reshape(DEPTH, 1, HD), kn.reshape(DEPTH, 1, HD), cos_t, sin_t, oc)


NA_QROWS = 4
NA_UROWS = 12
NA_TQ = NA_QROWS * GRID_W
NA_TK = NA_UROWS * GRID_W
NA_HP = 8


def _lat_na_kernel(q_ref, k_ref, v_ref, ck_ref, cv_ref, t2_ref, oa_in_ref, o_ref, bias_ref):
    del oa_in_ref
    j = pl.program_id(2)
    ustart = jnp.clip(NA_QROWS * j - NA_ROWS // 2, 0, GRID_R - NA_UROWS)
    left = lax.broadcasted_iota(jnp.int32, (GRID_W, 2 * GRID_W), 1) < GRID_W
    for a in range(NA_QROWS):
        r = NA_QROWS * j + a
        rs = jnp.clip(r - NA_ROWS // 2, 0, GRID_R - NA_ROWS)
        for m in range(NA_UROWS // 2):
            kr0 = ustart + 2 * m
            v0 = ((kr0 >= rs) & (kr0 < rs + NA_ROWS)).astype(jnp.int32)
            v1 = ((kr0 + 1 >= rs) & (kr0 + 1 < rs + NA_ROWS)).astype(jnp.int32)
            idx = jnp.clip(kr0 - r + NA_ROWS, 0, 2 * NA_ROWS - 1)
            ok = jnp.where(left, v0, v1) > 0
            for hh in range(NA_HP):
                bias_ref[hh, a * GRID_W:(a + 1) * GRID_W, m * 2 * GRID_W:(m + 1) * 2 * GRID_W] = jnp.where(
                    ok, t2_ref[hh, idx], NEG)
    start = pl.multiple_of(ustart * GRID_W, GRID_W)
    for hh in range(NA_HP):
        cs = slice(hh * HD, (hh + 1) * HD)
        q = q_ref[:, cs]
        kl = k_ref[pl.ds(start, NA_TK), cs]
        vl = v_ref[pl.ds(start, NA_TK), cs]
        s_loc = lax.dot_general(q, kl, NT_DIMS, preferred_element_type=F32) + bias_ref[hh]
        s_ctx = lax.dot_general(q, ck_ref[:, cs], NT_DIMS, preferred_element_type=F32)
        o_ref[:, cs] = _softmax_pv2(s_loc, vl, s_ctx, cv_ref[:, cs]).astype(BF16)


def _lat_na(u, ck, cv, t2, oa, l):
    nj = GRID_R // NA_QROWS
    r0 = T_CTX // NA_TQ
    s0 = T_CTX // L_LAT
    w = NA_HP * HD
    return pl.pallas_call(
        _lat_na_kernel,
        grid=(B_LAT, NA_H // NA_HP, nj),
        in_specs=[
            pl.BlockSpec((NA_TQ, w), lambda b, h, j: (r0 + b * nj + j, C_QA // w + h)),
            pl.BlockSpec((L_LAT, w), lambda b, h, j: (s0 + b, C_KA // w + h)),
            pl.BlockSpec((L_LAT, w), lambda b, h, j: (s0 + b, C_VA // w + h)),
            pl.BlockSpec((None, None, PAST, w), lambda b, h, j: (b, l, 0, h)),
            pl.BlockSpec((None, None, PAST, w), lambda b, h, j: (b, l, 0, h)),
            pl.BlockSpec((None, NA_HP, 2 * NA_ROWS, GRID_W, 2 * GRID_W), lambda b, h, j: (l, h, 0, 0, 0)),
            pl.BlockSpec(memory_space=pl.ANY),
        ],
        out_specs=pl.BlockSpec((NA_TQ, w), lambda b, h, j: (r0 + b * nj + j, h)),
        out_shape=jax.ShapeDtypeStruct((T, NA_H * HD), BF16),
        scratch_shapes=[pltpu.VMEM((NA_HP, NA_TQ, NA_TK), F32)],
        input_output_aliases={6: 0},
        compiler_params=_cp(3),
        name="lat_na",
    )(u, u, u, ck, cv, t2, oa)


def _na_bias_tiles(rpb):
    w = jnp.arange(GRID_W)
    cs = jnp.clip(w - NA_COLS // 2, 0, GRID_W - NA_COLS)
    cmask = (w[None, :] >= cs[:, None]) & (w[None, :] < cs[:, None] + NA_COLS)
    dc = jnp.clip(w[None, :] - w[:, None] + NA_COLS - 1, 0, 2 * NA_COLS - 2)
    tm = jnp.where(cmask, rpb[:, :, :, dc] / ATT_SCALE, NEG)
    pad = jnp.full(tm.shape[:2] + (1,) + tm.shape[3:], NEG, F32)
    lo = jnp.concatenate([pad, tm], axis=2)
    hi = jnp.concatenate([tm, pad], axis=2)
    return jnp.concatenate([lo, hi], axis=-1).astype(F32)


def _rope_tables():
    nf = HD // 4
    t = jnp.arange(L_LAT)
    inv = ROPE_BASE ** (-jnp.arange(nf, dtype=F32) / nf)
    ar = (t // GRID_W).astype(F32)[:, None] * inv[None, :]
    ac = (t % GRID_W).astype(F32)[:, None] * inv[None, :]
    cos_t = jnp.concatenate([jnp.cos(ar), jnp.cos(ar), jnp.cos(ac), jnp.cos(ac)], axis=-1)
    sin_t = jnp.concatenate([jnp.sin(ar), jnp.sin(ar), jnp.sin(ac), jnp.sin(ac)], axis=-1)
    return cos_t, sin_t


def _conv_kernel(x_ref, w_ref, b_ref, o_ref):
    x = x_ref[...].astype(F32)
    n = x.shape[0]
    row = lax.broadcasted_iota(jnp.int32, (SUBLANES, x.shape[1]), 0)
    xm = pltpu.roll(x, 1, 0)
    xm = jnp.concatenate([jnp.where(row == 0, 0.0, xm[:SUBLANES]), xm[SUBLANES:]], axis=0)
    xp = pltpu.roll(x, n - 1, 0)
    xp = jnp.concatenate([xp[:n - SUBLANES], jnp.where(row == SUBLANES - 1, 0.0, xp[n - SUBLANES:])], axis=0)
    y = w_ref[0:1, :] * xm + w_ref[1:2, :] * x + w_ref[2:3, :] * xp + b_ref[...]
    hy = 0.5 * y
    o_ref[...] = (hy * (1.0 + jnp.tanh(hy))).astype(o_ref.dtype)


def _conv(u, conv_w, conv_b, l, nseq, seq_len, row0):
    tc = 512
    nc = CONV_CH // tc
    rb0 = row0 // seq_len
    cb0 = C_XBC // tc
    return pl.pallas_call(
        _conv_kernel,
        grid=(nseq, nc),
        in_specs=[
            pl.BlockSpec((seq_len, tc), lambda b, j: (rb0 + b, cb0 + j)),
            pl.BlockSpec((None, 3, tc), lambda b, j: (l, 0, j)),
            pl.BlockSpec((None, 1, tc), lambda b, j: (l, 0, j)),
        ],
        out_specs=pl.BlockSpec((seq_len, tc), lambda b, j: (b, j)),
        out_shape=jax.ShapeDtypeStruct((nseq * seq_len, CONV_CH), F32),
        compiler_params=_cp(2),
        name="conv",
    )(u, conv_w, conv_b.reshape(DEPTH, 1, CONV_CH))


def _softplus(x):
    return jnp.maximum(x, 0.0) + jnp.log1p(jnp.exp(-jnp.abs(x)))


def _expand_heads(v, expand, terms):
    out = None
    rem = v
    for _ in range(terms):
        piece = rem.astype(BF16)
        rem = rem - piece.astype(F32)
        t = jnp.dot(piece, expand, preferred_element_type=F32)
        out = t if out is None else out + t
    return out


def _ssd_chunk(xbc, dt_col, dt_row, a_row, a_col, h_ref, reverse):
    x = xbc[:, :SSD_INNER]
    bm = xbc[:, SSD_INNER:SSD_INNER + SSD_G * SSD_N]
    cm = xbc[:, SSD_INNER + SSD_G * SSD_N:]
    li = lax.broadcasted_iota(jnp.int32, (CHUNK, CHUNK), 0)
    si = lax.broadcasted_iota(jnp.int32, (CHUNK, CHUNK), 1)
    mask = (si >= li) if reverse else (si <= li)
    tri = mask.astype(F32)
    tri_t = ((li >= si) if reverse else (li <= si)).astype(F32)
    cum_col = jnp.dot(tri, dt_col * a_row, precision=HI, preferred_element_type=F32)
    cum_row = jnp.dot(dt_row * a_col, tri_t, precision=HI, preferred_element_type=F32)
    tot_row = cum_col[0:1, :] if reverse else cum_col[CHUNK - 1:CHUNK, :]
    col = lax.broadcasted_iota(jnp.int32, (SSD_H, SSD_INNER), 1)
    lo_col = lax.broadcasted_iota(jnp.int32, (SSD_H, SSD_INNER), 0) * SSD_P
    expand = ((col >= lo_col) & (col < lo_col + SSD_P)).astype(BF16)
    w_exp = _expand_heads(jnp.exp(tot_row - cum_col) * dt_col, expand, 1)
    e_exp = _expand_heads(jnp.exp(cum_col), expand, 1)
    dec = _expand_heads(jnp.broadcast_to(jnp.exp(tot_row), (8, SSD_H)), expand, 3)[0:1, :]
    hpg = SSD_H // SSD_G
    xb = x.astype(BF16)
    bgs = [bm[:, g * SSD_N:(g + 1) * SSD_N].astype(BF16) for g in range(SSD_G)]
    cgs = [cm[:, g * SSD_N:(g + 1) * SSD_N].astype(BF16) for g in range(SSD_G)]
    gmat = [lax.dot_general(cgs[g], bgs[g], NT_DIMS, preferred_element_type=F32) for g in range(SSD_G)]
    lo = lax.broadcasted_iota(jnp.int32, (CHUNK, 2 * SSD_P), 1) < SSD_P
    zero = jnp.zeros((CHUNK, 2 * SSD_P), BF16)
    ys = []
    for pr in range(SSD_H // 2):
        g = (2 * pr) // hpg
        ms = []
        for h in (2 * pr, 2 * pr + 1):
            seg = cum_col[:, h:h + 1] - cum_row[h:h + 1, :]
            lm = jnp.exp(jnp.where(mask, seg, NEG))
            ms.append((gmat[g] * lm * dt_row[h:h + 1, :]).astype(BF16))
        xp = xb[:, pr * 2 * SSD_P:(pr + 1) * 2 * SSD_P]
        rhs = jnp.concatenate([jnp.where(lo, xp, zero), jnp.where(lo, zero, xp)], axis=0)
        ys.append(jnp.dot(jnp.concatenate(ms, axis=1), rhs, preferred_element_type=F32))
    y = jnp.concatenate(ys, axis=1)
    h_t = h_ref[...]
    hb = h_t.astype(BF16)
    xw = (x * w_exp).astype(BF16)
    wpg = SSD_INNER // SSD_G
    yo = [jnp.dot(cgs[g], hb[:, g * wpg:(g + 1) * wpg], preferred_element_type=F32) for g in range(SSD_G)]
    st = [lax.dot_general(bgs[g], xw[:, g * wpg:(g + 1) * wpg], TN_DIMS, preferred_element_type=F32)
          for g in range(SSD_G)]
    h_ref[...] = h_t * dec + jnp.concatenate(st, axis=1)
    return y + jnp.concatenate(yo, axis=1) * e_exp


SSD_STEP_CHUNKS = 4


def _ssd_kernel(xf_ref, xb_ref, dtf_ref, dtb_ref, dttf_ref, dttb_ref, prow_ref, pcol_ref, dtot_ref, *rest,
                context):
    hf_ref, hb_ref = rest[-2:]
    if context:
        yf_ref, yb_ref, hfin_ref = rest[-5:-2]
    else:
        h0_ref, yf_ref, yb_ref = rest[-5:-2]
    c = pl.program_id(1)

    @pl.when(c == 0)
    def _():
        if context:
            hf_ref[...] = jnp.zeros_like(hf_ref)
            hb_ref[...] = jnp.zeros_like(hb_ref)
        else:
            hf_ref[...] = h0_ref[0].T
            hb_ref[...] = h0_ref[1].T

    prow = prow_ref[...]
    pcol = pcol_ref[...]
    step_chunks = xf_ref.shape[0] // CHUNK
    for s in range(step_chunks):
        r = slice(s * CHUNK, (s + 1) * CHUNK)
        xf = xf_ref[r, :]
        dt_col = _softplus(dtf_ref[r, 0:SSD_H] + prow[0:1, :])
        dt_row = _softplus(dttf_ref[0:SSD_H, r] + pcol[:, 0:1])
        yf = _ssd_chunk(xf, dt_col, dt_row, -jnp.exp(prow[2:3, :]), -jnp.exp(pcol[:, 2:3]), hf_ref, False)
        yf_ref[r, :] = (yf + dtot_ref[...] * xf[:, :SSD_INNER]).astype(yf_ref.dtype)
    for s in reversed(range(step_chunks)):
        r = slice(s * CHUNK, (s + 1) * CHUNK)
        dt_col = _softplus(dtb_ref[r, SSD_H:2 * SSD_H] + prow[1:2, :])
        dt_row = _softplus(dttb_ref[SSD_H:2 * SSD_H, r] + pcol[:, 1:2])
        yb_ref[r, :] = _ssd_chunk(xb_ref[r, :], dt_col, dt_row, -jnp.exp(prow[3:4, :]), -jnp.exp(pcol[:, 3:4]),
                                  hb_ref, True).astype(yb_ref.dtype)

    if context:
        @pl.when(c == pl.num_programs(1) - 1)
        def _():
            hfin_ref[0] = hf_ref[...].T
            hfin_ref[1] = hb_ref[...].T


def _ssd(xbc, u_dt, dt_t, prow, pcol, dtot, l, nseq, seq_len, row0, states, context):
    blk = min(SSD_STEP_CHUNKS * CHUNK, seq_len)
    nc = seq_len // blk
    cb0 = row0 // blk
    rows = nseq * seq_len
    fwd = lambda b, c: b * nc + c
    bwd = lambda b, c: b * nc + (nc - 1 - c)
    state_spec = pl.BlockSpec((None, None, 2, SSD_INNER, SSD_N), lambda b, c: (b, l, 0, 0, 0))
    in_specs = [
        pl.BlockSpec((blk, CONV_CH), lambda b, c: (fwd(b, c), 0)),
        pl.BlockSpec((blk, CONV_CH), lambda b, c: (bwd(b, c), 0)),
        pl.BlockSpec((blk, DT_PAD), lambda b, c: (cb0 + fwd(b, c), 0)),
        pl.BlockSpec((blk, DT_PAD), lambda b, c: (cb0 + bwd(b, c), 0)),
        pl.BlockSpec((DT_PAD, blk), lambda b, c: (0, cb0 + fwd(b, c))),
        pl.BlockSpec((DT_PAD, blk), lambda b, c: (0, cb0 + bwd(b, c))),
        pl.BlockSpec((4, SSD_H), lambda b, c: (0, 0)),
        pl.BlockSpec((SSD_H, 4), lambda b, c: (0, 0)),
        pl.BlockSpec((1, SSD_INNER), lambda b, c: (0, 0)),
    ]
    args = [xbc, xbc, u_dt, u_dt, dt_t, dt_t, prow, pcol, dtot]
    out_specs = [
        pl.BlockSpec((blk, SSD_INNER), lambda b, c: (fwd(b, c), 0)),
        pl.BlockSpec((blk, SSD_INNER), lambda b, c: (bwd(b, c), 0)),
    ]
    out_shape = [jax.ShapeDtypeStruct((rows, SSD_INNER), BF16), jax.ShapeDtypeStruct((rows, SSD_INNER), BF16)]
    aliases = {}
    if context:
        out_specs.append(state_spec)
        out_shape.append(jax.ShapeDtypeStruct((nseq, DEPTH, 2, SSD_INNER, SSD_N), F32))
        if states is not None:
            in_specs.append(pl.BlockSpec(memory_space=pl.ANY))
            args.append(states)
            aliases = {len(args) - 1: 2}
    else:
        in_specs.append(state_spec)
        args.append(states)
    return pl.pallas_call(
        functools.partial(_ssd_kernel, context=context),
        grid=(nseq, nc),
        in_specs=in_specs,
        out_specs=out_specs,
        out_shape=out_shape,
        scratch_shapes=[pltpu.VMEM((SSD_N, SSD_INNER), F32), pltpu.VMEM((SSD_N, SSD_INNER), F32)],
        input_output_aliases=aliases,
        compiler_params=_cp(2),
        name="ssd",
    )(*args)


def _ssd_gate_kernel(yf_ref, yb_ref, z_ref, g_ref, *rest):
    o_ref = rest[-1]
    y = (yf_ref[...].astype(F32) + yb_ref[...].astype(F32)) * _silu(z_ref[...].astype(F32))
    o_ref[...] = _rms(y, g_ref[...]).astype(BF16)


def _ssd_gate(yf, yb, u, norm_w, l, row0, ob=None):
    tm = ROW_TILE
    rows = yf.shape[0]
    rb0 = row0 // tm
    row = pl.BlockSpec((tm, SSD_INNER), lambda i: (i, 0))
    in_specs = [
        row, row,
        pl.BlockSpec((pl.Element(tm), pl.Element(SSD_INNER)), lambda i: (tm * (rb0 + i), C_Z)),
        pl.BlockSpec((None, 1, SSD_INNER), lambda i: (l, 0, 0)),
    ]
    args = [yf, yb, u, norm_w.reshape(DEPTH, 1, SSD_INNER)]
    aliases = {}
    if ob is not None:
        in_specs.append(pl.BlockSpec(memory_space=pl.ANY))
        args.append(ob)
        aliases = {4: 0}
    return pl.pallas_call(
        _ssd_gate_kernel,
        grid=(rows // tm,),
        in_specs=in_specs,
        out_specs=pl.BlockSpec((tm, SSD_INNER), lambda i: (rb0 + i, 0)),
        out_shape=jax.ShapeDtypeStruct((T, SSD_INNER), BF16),
        input_output_aliases=aliases,
        compiler_params=_cp(1),
        name="ssd_gate",
    )(*args)


def kernel(x_prompt, x_sample, cache_na_k, cache_na_v, cache_gqa_k, cache_gqa_v, state_ssd, c, c_ctx, ada_w, ada_b, norm_mix_pre, norm_mix_post, norm_ffn_pre, norm_ffn_post, w_in, gate_b, na_rpb, gqa_q_norm, gqa_k_norm, ssd_conv_w, ssd_conv_b, ssd_dt_bias, ssd_a_log, ssd_d, ssd_norm_w, w_branch, w_out, ffn_w_up, ffn_w_down):
    cvec = jnp.concatenate([c_ctx[None, :], c, jnp.zeros((8 - 1 - B_LAT, D), F32)], axis=0)
    mods = _compute_mods(cvec, ada_w, ada_b).reshape(DEPTH, 8, 6, 1, D)
    wt_in = jnp.swapaxes(w_in, 1, 2)
    t2 = _na_bias_tiles(na_rpb)
    cos_t, sin_t = _rope_tables()
    ck_a = cache_na_k.reshape(B_LAT, DEPTH, PAST, NA_H * HD).astype(BF16)
    cv_a = cache_na_v.reshape(B_LAT, DEPTH, PAST, NA_H * HD).astype(BF16)
    ck_c = cache_gqa_k.reshape(B_LAT, DEPTH, PAST, GQA_KV * HD).astype(BF16)
    cv_c = cache_gqa_v.reshape(B_LAT, DEPTH, PAST, GQA_KV * HD).astype(BF16)
    h0_lat = state_ssd.reshape(B_LAT, DEPTH, 2, SSD_INNER, SSD_N)
    prow = jnp.concatenate([ssd_dt_bias, ssd_a_log], axis=1)
    pcol = prow.transpose(0, 2, 1)
    dtot = jnp.repeat(ssd_d[:, 0] + ssd_d[:, 1], SSD_P, axis=-1)[:, None, :]

    x, h = _normmod(x_prompt.reshape(T_CTX, D), x_sample.reshape(T_LAT, D), norm_mix_pre, mods, 0, 0, 1)
    caches = None
    ssd_fin = None
    for l in range(DEPTH):
        u, u_dt, dt_t = _in_proj(h, wt_in, l)
        oa, oc, caches = _ctx_attn(u, gqa_q_norm, gqa_k_norm, l, caches)
        oc = _lat_gqa(u, ck_c, cv_c, gqa_q_norm, gqa_k_norm, cos_t, sin_t, oc, l)
        oa = _lat_na(u, ck_a, cv_a, t2, oa, l)
        xbc_c = _conv(u, ssd_conv_w, ssd_conv_b, l, B_CTX, L_CTX, 0)
        xbc_l = _conv(u, ssd_conv_w, ssd_conv_b, l, B_LAT, L_LAT, T_CTX)
        yf_c, yb_c, ssd_fin = _ssd(xbc_c, u_dt, dt_t, prow[l], pcol[l], dtot[l], l, B_CTX, L_CTX, 0, ssd_fin, True)
        yf_l, yb_l = _ssd(xbc_l, u_dt, dt_t, prow[l], pcol[l], dtot[l], l, B_LAT, L_LAT, T_CTX, h0_lat, False)
        ob = _ssd_gate(yf_c, yb_c, u, ssd_norm_w, l, 0)
        ob = _ssd_gate(yf_l, yb_l, u, ssd_norm_w, l, T_CTX, ob)
        mix = _merge(h, oa, ob, oc, wt_in, w_branch, gate_b, l)
        x, h2 = _proj_resid(mix, w_out, l, x, norm_mix_post, mods, l, 2, 512, (norm_ffn_pre, l, 3, 4), "out_proj")
        act, w_down = _matmul_swiglu(h2, ffn_w_up, ffn_w_down, l)
        nxt = (norm_mix_pre, l + 1, 0, 1) if l + 1 < DEPTH else None
        x, h = _proj_resid(act, w_down, 0, x, norm_ffn_post, mods, l, 5, 256, nxt, "ffn_down")
    na_k, na_v, gqa_k, gqa_v = caches
    return (x.reshape(B_CTX, L_CTX, D), h.reshape(B_LAT, L_LAT, D),
            na_k.reshape(B_CTX, DEPTH, L_CTX, NA_H, HD), na_v.reshape(B_CTX, DEPTH, L_CTX, NA_H, HD),
            gqa_k.reshape(B_CTX, DEPTH, L_CTX, GQA_KV, HD), gqa_v.reshape(B_CTX, DEPTH, L_CTX, GQA_KV, HD),
            ssd_fin.reshape(B_CTX, DEPTH, 2, SSD_H, SSD_P, SSD_N))
```
